```python
import math
import jax, jax.numpy as jnp
from jax import lax
import numpy as np

D_MODEL = 1024
BATCH = 4
SEQ = 8192
DEPTH = 4

N_HEADS = 16
N_KV_HEADS = 4
HEAD_DIM = D_MODEL // N_HEADS
KV_GROUP = N_HEADS // N_KV_HEADS
QKV_DIM = (N_HEADS + 2 * N_KV_HEADS) * HEAD_DIM
SCALE = HEAD_DIM ** -0.5
GRID_W = 64
ROPE_THETA = 10000.0
ROPE_AXIS_DIM = HEAD_DIM // 2
Q_BLOCK = 128
WINDOW = 128
N_MIXERS = 2
N_REL_BUCKETS = 32
REL_MAX_DIST = 128
N_GROUPS = 4
EXPERTS_PER_GROUP = 4
N_EXPERTS = N_GROUPS * EXPERTS_PER_GROUP
TOP_K_IN_GROUP = 2
D_EXPERT = D_MODEL // 2
NORM_EPS = 1e-6
NEG_INF = -1e30

kernel_name = 'hybrid_axial_window_gqa_hier_moe_encoder'


def rmsnorm(x, g):
    xf = x.astype(jnp.float32)
    y = xf * lax.rsqrt(jnp.mean(xf * xf, axis=-1, keepdims=True) + NORM_EPS)
    return (y * g.astype(jnp.float32)).astype(x.dtype)


def rope_tables(pos):
    half = ROPE_AXIS_DIM // 2
    freqs = ROPE_THETA ** (-(jnp.arange(half, dtype=jnp.float32) * 2.0 / ROPE_AXIS_DIM))
    ang = pos.astype(jnp.float32)[:, None] * freqs[None, :]
    return jnp.cos(ang), jnp.sin(ang)


def apply_rope_1d(x, cos, sin):
    half = x.shape[-1] // 2
    xf = x.astype(jnp.float32)
    x1, x2 = xf[..., :half], xf[..., half:]
    c = cos[None, :, None, :]
    s = sin[None, :, None, :]
    return jnp.concatenate([x1 * c - x2 * s, x2 * c + x1 * s], axis=-1).astype(x.dtype)


def apply_axial_rope(x, row_cs, col_cs):
    xr = apply_rope_1d(x[..., :ROPE_AXIS_DIM], *row_cs)
    xc = apply_rope_1d(x[..., ROPE_AXIS_DIM:], *col_cs)
    return jnp.concatenate([xr, xc], axis=-1)


def t5_bucket(rel):
    nb = N_REL_BUCKETS // 2
    max_exact = nb // 2
    ret = jnp.where(rel > 0, nb, 0)
    n = jnp.abs(rel)
    large = max_exact + (jnp.log(jnp.maximum(n, 1).astype(jnp.float32) / max_exact)
                         / math.log(REL_MAX_DIST / max_exact) * (nb - max_exact)).astype(jnp.int32)
    large = jnp.minimum(large, nb - 1)
    return ret + jnp.where(n < max_exact, n, large)


def global_attention(q, k, v):
    B, S = q.shape[0], q.shape[1]
    nb = S // Q_BLOCK
    qb = q.reshape(B, nb, Q_BLOCK, N_KV_HEADS, KV_GROUP, HEAD_DIM).transpose(1, 0, 2, 3, 4, 5)

    def block(qblk):
        s = jnp.einsum('bqhgd,bkhd->bhgqk', qblk, k, preferred_element_type=jnp.float32) * SCALE
        p = jax.nn.softmax(s, axis=-1)
        return jnp.einsum('bhgqk,bkhd->bqhgd', p.astype(v.dtype), v)

    o = lax.map(block, qb)
    return o.transpose(1, 0, 2, 3, 4, 5).reshape(B, S, N_HEADS * HEAD_DIM)


def _band(t, nb):
    tb = t.reshape(nb, Q_BLOCK, N_KV_HEADS, HEAD_DIM)
    tp = jnp.pad(tb, ((1, 1), (0, 0), (0, 0), (0, 0)))
    return jnp.concatenate([tp[:-2], tp[1:-1], tp[2:]], axis=1)


def window_attention(q, k, v, rel_bias, sink):
    B, S = q.shape[0], q.shape[1]
    nb = S // Q_BLOCK
    rel = jnp.arange(3 * Q_BLOCK)[None, :] - Q_BLOCK - jnp.arange(Q_BLOCK)[:, None]
    bias = rel_bias.astype(jnp.float32)[:, t5_bucket(rel)]
    bias = bias.reshape(N_KV_HEADS, KV_GROUP, Q_BLOCK, 3 * Q_BLOCK)
    key_pos = jnp.arange(nb)[:, None] * Q_BLOCK + jnp.arange(3 * Q_BLOCK)[None, :] - Q_BLOCK
    valid = (jnp.abs(rel) <= WINDOW)[None] & ((key_pos >= 0) & (key_pos < S))[:, None, :]
    sink_f = sink.astype(jnp.float32).reshape(1, N_KV_HEADS, KV_GROUP, 1, 1)

    def one_seq(args):
        qs, ks, vs = args
        qb = qs.reshape(nb, Q_BLOCK, N_KV_HEADS, KV_GROUP, HEAD_DIM)
        kb, vb = _band(ks, nb), _band(vs, nb)
        s = jnp.einsum('nqhgd,nkhd->nhgqk', qb, kb, preferred_element_type=jnp.float32) * SCALE
        s = jnp.where(valid[:, None, None], s + bias[None], NEG_INF)
        sink_col = jnp.broadcast_to(sink_f, s.shape[:-1] + (1,))
        p = jax.nn.softmax(jnp.concatenate([s, sink_col], axis=-1), axis=-1)[..., :-1]
        o = jnp.einsum('nhgqk,nkhd->nqhgd', p.astype(vs.dtype), vb)
        return o.reshape(S, N_HEADS * HEAD_DIM)

    return lax.map(one_seq, (q, k, v))


def hier_moe(x, w_group, w_expert, w1, w3, w2):
    B, S, D = x.shape
    t = x.reshape(B * S, D)
    g_prob = jax.nn.softmax(jnp.einsum('td,dg->tg', t, w_group, preferred_element_type=jnp.float32), axis=-1)
    g_top, g_idx = lax.top_k(g_prob, 1)
    e_logits = jnp.einsum('td,de->te', t, w_expert, preferred_element_type=jnp.float32)
    e_logits = e_logits.reshape(-1, N_GROUPS, EXPERTS_PER_GROUP)
    e_in = jnp.einsum('tge,tg->te', e_logits, jax.nn.one_hot(g_idx[:, 0], N_GROUPS, dtype=jnp.float32))
    e_top, e_idx = lax.top_k(e_in, TOP_K_IN_GROUP)
    e_w = jax.nn.softmax(e_top, axis=-1) * g_top
    expert_id = g_idx * EXPERTS_PER_GROUP + e_idx
    combine = jnp.einsum('tk,tke->te', e_w, jax.nn.one_hot(expert_id, N_EXPERTS, dtype=jnp.float32)).astype(t.dtype)
    y = jnp.zeros_like(t)
    for e in range(N_EXPERTS):
        h = jax.nn.silu(t @ w1[e]) * (t @ w3[e])
        y = y + combine[:, e:e + 1] * (h @ w2[e])
    return y.reshape(B, S, D)


def setup_inputs(seed: int = 0) -> dict:
    key = jax.random.key(seed)
    ks = jax.random.split(key, 16)
    f32 = jnp.float32
    nrm = lambda k, shape, scale: jax.random.normal(k, shape, f32) * scale
    return {
        'x': nrm(ks[0], (BATCH, SEQ, D_MODEL), 1.0),
        'ln_mix': 1.0 + nrm(ks[1], (DEPTH, D_MODEL), 0.02),
        'w_qkv': nrm(ks[2], (DEPTH, D_MODEL, QKV_DIM), D_MODEL ** -0.5),
        'q_norm': 1.0 + nrm(ks[3], (DEPTH, HEAD_DIM), 0.02),
        'k_norm': 1.0 + nrm(ks[4], (DEPTH, HEAD_DIM), 0.02),
        'w_o': nrm(ks[5], (DEPTH, N_HEADS * HEAD_DIM, D_MODEL), (N_HEADS * HEAD_DIM) ** -0.5),
        'rel_bias': nrm(ks[6], (N_HEADS, N_REL_BUCKETS), 0.5),
        'sinks': nrm(ks[7], (DEPTH // N_MIXERS, N_HEADS), 0.5),
        'ln_ffn': 1.0 + nrm(ks[8], (DEPTH, D_MODEL), 0.02),
        'w_group': nrm(ks[9], (DEPTH, D_MODEL, N_GROUPS), D_MODEL ** -0.5),
        'w_expert': nrm(ks[10], (DEPTH, D_MODEL, N_EXPERTS), D_MODEL ** -0.5),
        'w1': nrm(ks[11], (DEPTH, N_EXPERTS, D_MODEL, D_EXPERT), D_MODEL ** -0.5),
        'w3': nrm(ks[12], (DEPTH, N_EXPERTS, D_MODEL, D_EXPERT), D_MODEL ** -0.5),
        'w2': nrm(ks[13], (DEPTH, N_EXPERTS, D_EXPERT, D_MODEL), D_EXPERT ** -0.5),
    }


def reference(x, ln_mix, w_qkv, q_norm, k_norm, w_o, rel_bias, sinks, ln_ffn, w_group, w_expert, w1, w3, w2):
    B, S, _ = x.shape
    rows_n = S // GRID_W
    rows = jnp.repeat(jnp.arange(rows_n), GRID_W)
    cols = jnp.tile(jnp.arange(GRID_W), rows_n)
    row_cs = rope_tables(rows)
    col_cs = rope_tables(cols)
    nq = N_HEADS * HEAD_DIM
    nk = N_KV_HEADS * HEAD_DIM
    for i in range(DEPTH):
        h = rmsnorm(x, ln_mix[i])
        qkv = h @ w_qkv[i]
        q = rmsnorm(qkv[..., :nq].reshape(B, S, N_HEADS, HEAD_DIM), q_norm[i])
        k = rmsnorm(qkv[..., nq:nq + nk].reshape(B, S, N_KV_HEADS, HEAD_DIM), k_norm[i])
        v = qkv[..., nq + nk:].reshape(B, S, N_KV_HEADS, HEAD_DIM)
        if i % N_MIXERS == 0:
            q = apply_axial_rope(q, row_cs, col_cs)
            k = apply_axial_rope(k, row_cs, col_cs)
            o = global_attention(q, k, v)
        else:
            o = window_attention(q, k, v, rel_bias, sinks[i // N_MIXERS])
        x = x + o @ w_o[i]
        x = x + hier_moe(rmsnorm(x, ln_ffn[i]), w_group[i], w_expert[i], w1[i], w3[i], w2[i])
    return x
```

```python
import functools

import jax
import jax.numpy as jnp
from jax import lax
from jax.experimental import pallas as pl
from jax.experimental.pallas import tpu as pltpu

N_HEADS = 16
N_KV_HEADS = 4
HEAD_DIM = 64
KV_GROUP = N_HEADS // N_KV_HEADS
SCALE = HEAD_DIM ** -0.5
GRID_W = 64
ROPE_THETA = 10000.0
ROPE_AXIS_DIM = HEAD_DIM // 2
Q_BLOCK = 128
WINDOW = 128
N_MIXERS = 2
N_REL_BUCKETS = 32
REL_MAX_DIST = 128
N_GROUPS = 4
EXPERTS_PER_GROUP = 4
N_EXPERTS = N_GROUPS * EXPERTS_PER_GROUP
NORM_EPS = 1e-6
NEG_INF = -1e30

PAIRS_PER_GROUP = 6
N_CLASSES = N_GROUPS * PAIRS_PER_GROUP
PAIR_LO = (0, 0, 0, 1, 1, 2)
PAIR_HI = (1, 2, 3, 2, 3, 3)

LANES = 128
MXU_DIM = 256
VMEM_LIMIT = 48 * 1024 * 1024

ROW_WORDS_PAD = LANES

F32 = jnp.float32
BF16 = jnp.bfloat16
U32 = jnp.uint32
I32 = jnp.int32


def _params(sem):
    return pltpu.CompilerParams(dimension_semantics=sem, vmem_limit_bytes=VMEM_LIMIT)


def _segment_sum_matrix():
    r = lax.broadcasted_iota(I32, (MXU_DIM, MXU_DIM), 0) // HEAD_DIM
    c = lax.broadcasted_iota(I32, (MXU_DIM, MXU_DIM), 1) // HEAD_DIM
    return (r == c).astype(BF16)


def _head_rmsnorm(t, gain, seg):
    outs = []
    for c in range(t.shape[1] // MXU_DIM):
        tc = t[:, c * MXU_DIM:(c + 1) * MXU_DIM]
        sq = tc * tc
        hi = sq.astype(BF16)
        lo = (sq - hi.astype(F32)).astype(BF16)
        ss = (jnp.dot(hi, seg, preferred_element_type=F32)
              + jnp.dot(lo, seg, preferred_element_type=F32))
        outs.append(tc * lax.rsqrt(ss * (1.0 / HEAD_DIM) + NORM_EPS) * gain)
    return outs


def _rope(chunks, cos, sin):
    lane = lax.broadcasted_iota(I32, cos.shape, 1)
    upper = (lane & (ROPE_AXIS_DIM // 2)) != 0
    outs = []
    for tc in chunks:
        halves = []
        for j in range(MXU_DIM // LANES):
            xc = tc[:, j * LANES:(j + 1) * LANES]
            partner = jnp.where(upper,
                                pltpu.roll(xc, ROPE_AXIS_DIM // 2, 1),
                                pltpu.roll(xc, LANES - ROPE_AXIS_DIM // 2, 1))
            halves.append(xc * cos + partner * sin)
        outs.append(jnp.concatenate(halves, axis=-1))
    return outs


def _qkv_kernel(x_ref, g_ref, w_ref, qg_ref, kg_ref, cos_ref, sin_ref,
                q_ref, k_ref, v_ref, *, rope):
    x = x_ref[...]
    ms = jnp.mean(x * x, axis=-1, keepdims=True)
    h = (x * lax.rsqrt(ms + NORM_EPS) * g_ref[...]).astype(BF16)
    qkv = jnp.dot(h, w_ref[...], preferred_element_type=F32)
    nq = N_HEADS * HEAD_DIM
    nk = N_KV_HEADS * HEAD_DIM
    seg = _segment_sum_matrix()
    qs = _head_rmsnorm(qkv[:, :nq], qg_ref[...], seg)
    ks = _head_rmsnorm(qkv[:, nq:nq + nk], kg_ref[...], seg)
    if rope:
        cos = cos_ref[...]
        sin = sin_ref[...]
        qs = _rope(qs, cos, sin)
        ks = _rope(ks, cos, sin)
    heads_per_chunk = MXU_DIM // HEAD_DIM
    for c, qc in enumerate(qs):
        qc = qc * SCALE
        for j in range(heads_per_chunk):
            q_ref[0, c * heads_per_chunk + j] = qc[:, j * HEAD_DIM:(j + 1) * HEAD_DIM].astype(BF16)
    for c, kc in enumerate(ks):
        for j in range(heads_per_chunk):
            k_ref[0, c * heads_per_chunk + j] = kc[:, j * HEAD_DIM:(j + 1) * HEAD_DIM].astype(BF16)
    v = qkv[:, nq + nk:]
    for j in range(N_KV_HEADS):
        v_ref[0, j] = v[:, j * HEAD_DIM:(j + 1) * HEAD_DIM].astype(BF16)


def _qkv_call(x2, gain, w, qg, kg, cos, sin, *, batch, seq, rope, tm):
    T, D = x2.shape
    spt = seq // tm
    qkv_dim = w.shape[1]
    kern = functools.partial(_qkv_kernel, rope=rope)
    head_map = lambda i: (i // spt, 0, i % spt, 0)
    return pl.pallas_call(
        kern,
        grid=(T // tm,),
        in_specs=[
            pl.BlockSpec((tm, D), lambda i: (i, 0)),
            pl.BlockSpec((1, D), lambda i: (0, 0)),
            pl.BlockSpec((D, qkv_dim), lambda i: (0, 0)),
            pl.BlockSpec((1, MXU_DIM), lambda i: (0, 0)),
            pl.BlockSpec((1, MXU_DIM), lambda i: (0, 0)),
            pl.BlockSpec((tm, LANES), lambda i: (i % spt, 0)),
            pl.BlockSpec((tm, LANES), lambda i: (i % spt, 0)),
        ],
        out_specs=[
            pl.BlockSpec((1, N_HEADS, tm, HEAD_DIM), head_map),
            pl.BlockSpec((1, N_KV_HEADS, tm, HEAD_DIM), head_map),
            pl.BlockSpec((1, N_KV_HEADS, tm, HEAD_DIM), head_map),
        ],
        out_shape=[
            jax.ShapeDtypeStruct((batch, N_HEADS, seq, HEAD_DIM), BF16),
            jax.ShapeDtypeStruct((batch, N_KV_HEADS, seq, HEAD_DIM), BF16),
            jax.ShapeDtypeStruct((batch, N_KV_HEADS, seq, HEAD_DIM), BF16),
        ],
        compiler_params=_params(("parallel",)),
        name="qkv_proj",
    )(x2, gain, w, qg, kg, cos, sin)


def _merge_heads(o, rows):
    return jnp.concatenate([o[g * rows:(g + 1) * rows] for g in range(KV_GROUP)], axis=-1)


def _flash_kernel(q_ref, k_ref, v_ref, o_ref, *, tk):
    tq = q_ref.shape[2]
    seq = k_ref.shape[2]
    rows = KV_GROUP * tq
    q = q_ref[0].reshape(rows, HEAD_DIM)

    def body(j, carry):
        m, l, acc = carry
        off = pl.multiple_of(j * tk, tk)
        kc = k_ref[0, 0, pl.ds(off, tk), :]
        vc = v_ref[0, 0, pl.ds(off, tk), :]
        s = lax.dot_general(q, kc, (((1,), (1,)), ((), ())), preferred_element_type=F32)
        m_new = jnp.maximum(m, jnp.max(s, axis=-1, keepdims=True))
        p = jnp.exp(s - m_new)
        alpha = jnp.exp(m - m_new)
        l = alpha * l + jnp.sum(p, axis=-1, keepdims=True)
        acc = alpha * acc + jnp.dot(p.astype(BF16), vc, preferred_element_type=F32)
        return m_new, l, acc

    init = (jnp.full((rows, 1), NEG_INF, F32), jnp.zeros((rows, 1), F32),
            jnp.zeros((rows, HEAD_DIM), F32))
    _, l, acc = lax.fori_loop(0, seq // tk, body, init)
    o_ref[0] = _merge_heads(acc / l, tq).astype(BF16)


def _flash_call(q, k, v, *, tq, tk):
    B, _, S, _ = q.shape
    kern = functools.partial(_flash_kernel, tk=tk)
    return pl.pallas_call(
        kern,
        grid=(B, N_KV_HEADS, S // tq),
        in_specs=[
            pl.BlockSpec((1, KV_GROUP, tq, HEAD_DIM), lambda b, h, i: (b, h, i, 0)),
            pl.BlockSpec((1, 1, S, HEAD_DIM), lambda b, h, i: (b, h, 0, 0)),
            pl.BlockSpec((1, 1, S, HEAD_DIM), lambda b, h, i: (b, h, 0, 0)),
        ],
        out_specs=pl.BlockSpec((1, tq, KV_GROUP * HEAD_DIM), lambda b, h, i: (b, i, h)),
        out_shape=jax.ShapeDtypeStruct((B, S, N_HEADS * HEAD_DIM), BF16),
        compiler_params=_params(("parallel", "parallel", "parallel")),
        name="flash_global",
    )(q, k, v)


def _bias_kernel(rb_ref, bucket_ref, out_ref):
    h = pl.program_id(0)
    b = bucket_ref[...]
    acc = jnp.zeros(b.shape, F32)
    for j in range(N_REL_BUCKETS):
        acc = jnp.where(b == j, rb_ref[h, j], acc)
    out_ref[0] = acc


def _bias_call(rel_bias, bucket):
    return pl.pallas_call(
        _bias_kernel,
        grid=(N_HEADS,),
        in_specs=[
            pl.BlockSpec(memory_space=pltpu.SMEM),
            pl.BlockSpec(bucket.shape, lambda h: (0, 0)),
        ],
        out_specs=pl.BlockSpec((1,) + bucket.shape, lambda h: (h, 0, 0)),
        out_shape=jax.ShapeDtypeStruct((N_HEADS,) + bucket.shape, F32),
        compiler_params=_params(("parallel",)),
        name="rel_bias_table",
    )(rel_bias, bucket)


def _window_kernel(sink_ref, q_ref, k_ref, v_ref, bias_ref, o_ref, *, blocks_per_tile):
    hk = pl.program_id(1)
    tile = pl.program_id(2)
    nb = k_ref.shape[2] // Q_BLOCK
    rows = KV_GROUP * Q_BLOCK
    band_w = 3 * Q_BLOCK

    qpos = lax.broadcasted_iota(I32, (rows, band_w), 0) % Q_BLOCK
    col = lax.broadcasted_iota(I32, (rows, band_w), 1)
    in_band = jnp.abs(col - Q_BLOCK - qpos) <= WINDOW
    bias = bias_ref[...].reshape(rows, band_w)

    head = lax.broadcasted_iota(I32, (rows, 1), 0) // Q_BLOCK
    sink = jnp.zeros((rows, 1), F32)
    for g in range(KV_GROUP):
        sink = jnp.where(head == g, sink_ref[hk * KV_GROUP + g], sink)

    def body(j, carry):
        n = tile * blocks_per_tile + j
        off = pl.multiple_of(j * Q_BLOCK, Q_BLOCK)
        qb = q_ref[0, :, pl.ds(off, Q_BLOCK), :].reshape(rows, HEAD_DIM)
        starts = [pl.multiple_of(jnp.maximum(n - 1, 0) * Q_BLOCK, Q_BLOCK),
                  pl.multiple_of(n * Q_BLOCK, Q_BLOCK),
                  pl.multiple_of(jnp.minimum(n + 1, nb - 1) * Q_BLOCK, Q_BLOCK)]
        kcat = jnp.concatenate([k_ref[0, 0, pl.ds(s, Q_BLOCK), :] for s in starts], axis=0)
        vcat = jnp.concatenate([v_ref[0, 0, pl.ds(s, Q_BLOCK), :] for s in starts], axis=0)
        s = lax.dot_general(qb, kcat, (((1,), (1,)), ((), ())), preferred_element_type=F32)
        valid = (in_band
                 & ((col >= Q_BLOCK) | (n > 0))
                 & ((col < 2 * Q_BLOCK) | (n < nb - 1)))
        s = jnp.where(valid, s + bias, NEG_INF)
        m = jnp.maximum(jnp.max(s, axis=-1, keepdims=True), sink)
        p = jnp.exp(s - m)
        denom = jnp.sum(p, axis=-1, keepdims=True) + jnp.exp(sink - m)
        o = jnp.dot(p.astype(BF16), vcat, preferred_element_type=F32) / denom
        o_ref[0, pl.ds(off, Q_BLOCK), :] = _merge_heads(o, Q_BLOCK).astype(BF16)
        return carry

    lax.fori_loop(0, blocks_per_tile, body, 0)


def _window_call(sink, q, k, v, bias, *, tq):
    B, _, S, _ = q.shape
    kern = functools.partial(_window_kernel, blocks_per_tile=tq // Q_BLOCK)
    return pl.pallas_call(
        kern,
        grid=(B, N_KV_HEADS, S // tq),
        in_specs=[
            pl.BlockSpec(memory_space=pltpu.SMEM),
            pl.BlockSpec((1, KV_GROUP, tq, HEAD_DIM), lambda b, h, i: (b, h, i, 0)),
            pl.BlockSpec((1, 1, S, HEAD_DIM), lambda b, h, i: (b, h, 0, 0)),
            pl.BlockSpec((1, 1, S, HEAD_DIM), lambda b, h, i: (b, h, 0, 0)),
            pl.BlockSpec((KV_GROUP, Q_BLOCK, 3 * Q_BLOCK), lambda b, h, i: (h, 0, 0)),
        ],
        out_specs=pl.BlockSpec((1, tq, KV_GROUP * HEAD_DIM), lambda b, h, i: (b, i, h)),
        out_shape=jax.ShapeDtypeStruct((B, S, N_HEADS * HEAD_DIM), BF16),
        compiler_params=_params(("parallel", "parallel", "parallel")),
        name="window_attn",
    )(sink, q, k, v, bias)


def _f32_bits(x):
    return lax.bitcast_convert_type(x, U32)


def _pack_bf16_pair(lo_half, hi_half):
    lo = lax.shift_right_logical(_f32_bits(lo_half.astype(BF16).astype(F32)), jnp.uint32(16))
    hi = _f32_bits(hi_half.astype(BF16).astype(F32)) & jnp.uint32(0xFFFF0000)
    return lo | hi


def _unpack_bf16_pair(words):
    lo = lax.bitcast_convert_type(lax.shift_left(words, jnp.uint32(16)), F32)
    hi = lax.bitcast_convert_type(words & jnp.uint32(0xFFFF0000), F32)
    return lo, hi


def _post_kernel(x_ref, o_ref, wo_ref, g_ref, wr_ref,
                 xn_ref, row_ref, cls_ref, rank_ref, cnt_ref, run_ref):
    step = pl.program_id(0)
    tm = x_ref.shape[0]
    half = x_ref.shape[1] // 2

    @pl.when(step == 0)
    def _():
        run_ref[...] = jnp.zeros_like(run_ref)

    xn = x_ref[...] + jnp.dot(o_ref[...], wo_ref[...], preferred_element_type=F32)
    xn_ref[...] = xn
    ms = jnp.mean(xn * xn, axis=-1, keepdims=True)
    t = xn * lax.rsqrt(ms + NORM_EPS) * g_ref[...]
    logits = jnp.dot(t, wr_ref[...], preferred_element_type=F32,
                     precision=lax.Precision.HIGHEST)

    lane = lax.broadcasted_iota(I32, (tm, LANES), 1)
    lane_f = lane.astype(F32)
    big = jnp.float32(LANES)

    def first_argmax(vals):
        top = jnp.max(vals, axis=-1, keepdims=True)
        idx = jnp.min(jnp.where(vals == top, lane_f, big), axis=-1, keepdims=True)
        return top, idx

    is_group = lane < N_GROUPS
    g_top, g_idx = first_argmax(jnp.where(is_group, logits, -jnp.inf))
    g_sum = jnp.sum(jnp.where(is_group, jnp.exp(logits - g_top), 0.0), axis=-1, keepdims=True)
    g_prob = 1.0 / g_sum

    base = N_GROUPS + EXPERTS_PER_GROUP * g_idx
    in_group = (lane_f >= base) & (lane_f < base + EXPERTS_PER_GROUP)
    e_logits = jnp.where(in_group, logits, -jnp.inf)
    e1, i1 = first_argmax(e_logits)
    e2, i2 = first_argmax(jnp.where(lane_f == i1, -jnp.inf, e_logits))
    r = jnp.exp(e2 - e1)
    w_first = (1.0 / (1.0 + r)) * g_prob
    w_second = (r / (1.0 + r)) * g_prob

    j1 = i1 - base
    j2 = i2 - base
    first_is_lo = j1 < j2
    a = jnp.minimum(j1, j2)
    b = jnp.maximum(j1, j2)
    pair = a * 3.0 - a * (a - 1.0) * 0.5 + (b - a - 1.0)
    cls_f = g_idx * PAIRS_PER_GROUP + pair
    w_lo = jnp.where(first_is_lo, w_first, w_second)
    w_hi = jnp.where(first_is_lo, w_second, w_first)

    onehot = lane_f == cls_f
    before = (lax.broadcasted_iota(I32, (tm, tm), 0) > lax.broadcasted_iota(I32, (tm, tm), 1))
    earlier = jnp.dot(before.astype(BF16), onehot.astype(BF16), preferred_element_type=F32)
    rank = jnp.sum(jnp.where(onehot, earlier + run_ref[...], 0.0), axis=-1, keepdims=True)
    run_ref[...] += jnp.sum(onehot.astype(F32), axis=0, keepdims=True)
    cnt_ref[...] = run_ref[...]

    cls_ref[...] = jnp.broadcast_to(cls_f, (tm, LANES)).astype(I32)
    rank_ref[...] = jnp.broadcast_to(rank, (tm, LANES)).astype(I32)

    row_ref[:, :half] = _pack_bf16_pair(t[:, :half], t[:, half:])
    weights = jnp.where(lane == 0, w_lo, jnp.where(lane == 1, w_hi, 0.0))
    row_ref[:, half:] = _f32_bits(weights)


def _post_call(x2, o2, wo, gain, wr, *, tm):
    T, D = x2.shape
    row_w = D // 2 + ROW_WORDS_PAD
    return pl.pallas_call(
        _post_kernel,
        grid=(T // tm,),
        in_specs=[
            pl.BlockSpec((tm, D), lambda i: (i, 0)),
            pl.BlockSpec((tm, D), lambda i: (i, 0)),
            pl.BlockSpec((D, D), lambda i: (0, 0)),
            pl.BlockSpec((1, D), lambda i: (0, 0)),
            pl.BlockSpec((D, LANES), lambda i: (0, 0)),
        ],
        out_specs=[
            pl.BlockSpec((tm, D), lambda i: (i, 0)),
            pl.BlockSpec((tm, row_w), lambda i: (i, 0)),
            pl.BlockSpec((tm, LANES), lambda i: (i, 0)),
            pl.BlockSpec((tm, LANES), lambda i: (i, 0)),
            pl.BlockSpec((1, LANES), lambda i: (0, 0)),
        ],
        out_shape=[
            jax.ShapeDtypeStruct((T, D), F32),
            jax.ShapeDtypeStruct((T, row_w), U32),
            jax.ShapeDtypeStruct((T, LANES), I32),
            jax.ShapeDtypeStruct((T, LANES), I32),
            jax.ShapeDtypeStruct((1, LANES), F32),
        ],
        scratch_shapes=[pltpu.VMEM((1, LANES), F32)],
        compiler_params=_params(("arbitrary",)),
        name="post_attn_router",
    )(x2, o2, wo, gain, wr)


def _row_copy(src_ref, src_row, dst_ref, dst_row, sem):
    return pltpu.make_async_copy(src_ref.at[pl.ds(src_row, 1)], dst_ref.at[pl.ds(dst_row, 1)], sem)


def _dispatch_kernel(start_ref, cls_ref, rank_ref, row_ref, init_ref, out_ref, sem):
    del init_ref
    ts = row_ref.shape[0]

    def issue(r, carry):
        dst = start_ref[cls_ref[0, 0, r]] + rank_ref[0, 0, r]
        _row_copy(row_ref, r, out_ref, dst, sem).start()
        return carry

    lax.fori_loop(0, ts, issue, 0)

    def drain(r, carry):
        _row_copy(row_ref, 0, out_ref, 0, sem).wait()
        return carry

    lax.fori_loop(0, ts, drain, 0)


def _dispatch_call(row_start, cls3, rank3, rows, init, *, ts):
    T, row_w = rows.shape
    grid_spec = pltpu.PrefetchScalarGridSpec(
        num_scalar_prefetch=1,
        grid=(T // ts,),
        in_specs=[
            pl.BlockSpec((1, 1, ts), lambda i, s: (i, 0, 0), memory_space=pltpu.SMEM),
            pl.BlockSpec((1, 1, ts), lambda i, s: (i, 0, 0), memory_space=pltpu.SMEM),
            pl.BlockSpec((ts, row_w), lambda i, s: (i, 0)),
            pl.BlockSpec(memory_space=pl.ANY),
        ],
        out_specs=pl.BlockSpec(memory_space=pl.ANY),
        scratch_shapes=[pltpu.SemaphoreType.DMA(())],
    )
    return pl.pallas_call(
        _dispatch_kernel,
        grid_spec=grid_spec,
        out_shape=jax.ShapeDtypeStruct(init.shape, init.dtype),
        input_output_aliases={4: 0},
        compiler_params=_params(("arbitrary",)),
        name="moe_dispatch",
    )(row_start, cls3, rank3, rows, init)


def _moe_kernel(ea_ref, eb_ref, used_ref, row_ref, w1a_ref, w3a_ref, w2a_ref,
                w1b_ref, w3b_ref, w2b_ref, y_ref):
    del ea_ref, eb_ref
    i = pl.program_id(0)
    half = row_ref.shape[1] - ROW_WORDS_PAD

    @pl.when(i < used_ref[0])
    def _():
        lo, hi = _unpack_bf16_pair(row_ref[:, :half])
        x = jnp.concatenate([lo, hi], axis=-1).astype(BF16)
        wts = lax.bitcast_convert_type(row_ref[:, half:], F32)
        w_lo = wts[:, 0:1]
        w_hi = wts[:, 1:2]

        def expert(w1_ref, w3_ref, w2_ref):
            z = jnp.dot(x, w1_ref[0], preferred_element_type=F32)
            u = jnp.dot(x, w3_ref[0], preferred_element_type=F32)
            h = (z * (1.0 / (1.0 + jnp.exp(-z)))) * u
            return jnp.dot(h.astype(BF16), w2_ref[0], preferred_element_type=F32)

        y_ref[...] = (w_lo * expert(w1a_ref, w3a_ref, w2a_ref)
                      + w_hi * expert(w1b_ref, w3b_ref, w2b_ref))

    @pl.when(i >= used_ref[0])
    def _():
        y_ref[...] = jnp.zeros_like(y_ref)


def _moe_call(tile_ea, tile_eb, n_used, rows, w1, w3, w2, *, tmx):
    rows_pad, row_w = rows.shape
    _, D, F = w1.shape
    wa = lambda i, ea, eb, used: (ea[i], 0, 0)
    wb = lambda i, ea, eb, used: (eb[i], 0, 0)
    grid_spec = pltpu.PrefetchScalarGridSpec(
        num_scalar_prefetch=3,
        grid=(rows_pad // tmx,),
        in_specs=[
            pl.BlockSpec((tmx, row_w), lambda i, ea, eb, used: (i, 0)),
            pl.BlockSpec((1, D, F), wa),
            pl.BlockSpec((1, D, F), wa),
            pl.BlockSpec((1, F, D), wa),
            pl.BlockSpec((1, D, F), wb),
            pl.BlockSpec((1, D, F), wb),
            pl.BlockSpec((1, F, D), wb),
        ],
        out_specs=pl.BlockSpec((tmx, D), lambda i, ea, eb, used: (i, 0)),
    )
    return pl.pallas_call(
        _moe_kernel,
        grid_spec=grid_spec,
        out_shape=jax.ShapeDtypeStruct((rows_pad, D), F32),
        compiler_params=_params(("arbitrary",)),
        name="moe_experts",
    )(tile_ea, tile_eb, n_used, rows, w1, w3, w2, w1, w3, w2)


def _combine_kernel(start_ref, cls_ref, rank_ref, x_ref, y_ref, out_ref, buf_ref, sem):
    tc = x_ref.shape[0]

    def issue(r, carry):
        src = start_ref[cls_ref[0, 0, r]] + rank_ref[0, 0, r]
        _row_copy(y_ref, src, buf_ref, r, sem).start()
        return carry

    lax.fori_loop(0, tc, issue, 0)

    def drain(r, carry):
        _row_copy(y_ref, 0, buf_ref, 0, sem).wait()
        return carry

    lax.fori_loop(0, tc, drain, 0)
    out_ref[...] = x_ref[...] + buf_ref[...]


def _combine_call(row_start, cls3, rank3, xn, y, *, tc):
    T, D = xn.shape
    grid_spec = pltpu.PrefetchScalarGridSpec(
        num_scalar_prefetch=1,
        grid=(T // tc,),
        in_specs=[
            pl.BlockSpec((1, 1, tc), lambda i, s: (i, 0, 0), memory_space=pltpu.SMEM),
            pl.BlockSpec((1, 1, tc), lambda i, s: (i, 0, 0), memory_space=pltpu.SMEM),
            pl.BlockSpec((tc, D), lambda i, s: (i, 0)),
            pl.BlockSpec(memory_space=pl.ANY),
        ],
        out_specs=pl.BlockSpec((tc, D), lambda i, s: (i, 0)),
        scratch_shapes=[pltpu.VMEM((tc, D), F32), pltpu.SemaphoreType.DMA(())],
    )
    return pl.pallas_call(
        _combine_kernel,
        grid_spec=grid_spec,
        out_shape=jax.ShapeDtypeStruct((T, D), F32),
        compiler_params=_params(("arbitrary",)),
        name="moe_combine",
    )(row_start, cls3, rank3, xn, y)


def _rope_tables(seq):
    half = ROPE_AXIS_DIM // 2
    freqs = ROPE_THETA ** (-(jnp.arange(half, dtype=F32) * 2.0 / ROPE_AXIS_DIM))
    t = jnp.arange(seq)
    ang_row = (t // GRID_W).astype(F32)[:, None] * freqs[None, :]
    ang_col = (t % GRID_W).astype(F32)[:, None] * freqs[None, :]

    def axis_tables(ang):
        c, s = jnp.cos(ang), jnp.sin(ang)
        return jnp.concatenate([c, c], axis=-1), jnp.concatenate([-s, s], axis=-1)

    cr, sr = axis_tables(ang_row)
    cc, sc = axis_tables(ang_col)
    cos = jnp.concatenate([cr, cc], axis=-1)
    sin = jnp.concatenate([sr, sc], axis=-1)
    reps = LANES // HEAD_DIM
    return jnp.tile(cos, (1, reps)), jnp.tile(sin, (1, reps))


def _t5_bucket_table():
    import math
    rel = jnp.arange(3 * Q_BLOCK)[None, :] - Q_BLOCK - jnp.arange(Q_BLOCK)[:, None]
    nb = N_REL_BUCKETS // 2
    max_exact = nb // 2
    ret = jnp.where(rel > 0, nb, 0)
    n = jnp.abs(rel)
    large = max_exact + (jnp.log(jnp.maximum(n, 1).astype(F32) / max_exact)
                         / math.log(REL_MAX_DIST / max_exact) * (nb - max_exact)).astype(I32)
    large = jnp.minimum(large, nb - 1)
    return (ret + jnp.where(n < max_exact, n, large)).astype(I32)


def _tile_tables(counts, tmx, n_tiles):
    tiles_per_class = (counts + tmx - 1) // tmx
    tile_end = jnp.cumsum(tiles_per_class)
    row_start = ((tile_end - tiles_per_class) * tmx).astype(I32)
    n_used = tile_end[-1].astype(I32)
    tile_id = jnp.minimum(jnp.arange(n_tiles, dtype=I32), jnp.maximum(n_used - 1, 0))
    tile_cls = jnp.minimum(jnp.searchsorted(tile_end, tile_id, side="right"), N_CLASSES - 1).astype(I32)
    group = tile_cls // PAIRS_PER_GROUP
    pair = tile_cls % PAIRS_PER_GROUP
    tile_ea = group * EXPERTS_PER_GROUP + jnp.asarray(PAIR_LO, I32)[pair]
    tile_eb = group * EXPERTS_PER_GROUP + jnp.asarray(PAIR_HI, I32)[pair]
    return row_start, tile_ea.astype(I32), tile_eb.astype(I32), n_used.reshape(1)


def _tile(n, pref):
    t = min(n, pref)
    assert n % t == 0, (n, pref)
    return t


def kernel(x, ln_mix, w_qkv, q_norm, k_norm, w_o, rel_bias, sinks, ln_ffn, w_group, w_expert, w1, w3, w2):
    B, S, D = x.shape
    depth = w_qkv.shape[0]
    T = B * S
    assert S % GRID_W == 0 and S % Q_BLOCK == 0 and D == N_HEADS * HEAD_DIM

    tm = _tile(S, 512)
    tq_flash = _tile(S, 256)
    tk_flash = _tile(S, 512)
    tq_win = _tile(S, 1024)
    ts = _tile(T, 256)
    tmx = _tile(T, 256)
    n_tiles = T // tmx + N_CLASSES
    rows_pad = n_tiles * tmx
    row_w = D // 2 + ROW_WORDS_PAD

    cos, sin = _rope_tables(S)
    bias = _bias_call(rel_bias.astype(F32), _t5_bucket_table())

    w_qkv_b = w_qkv.astype(BF16)
    w_o_b = w_o.astype(BF16)
    w1_b, w3_b, w2_b = w1.astype(BF16), w3.astype(BF16), w2.astype(BF16)
    reps = MXU_DIM // HEAD_DIM
    router = jnp.concatenate(
        [w_group, w_expert,
         jnp.zeros((depth, D, LANES - N_GROUPS - N_EXPERTS), F32)], axis=-1).astype(F32)

    x2 = x.reshape(T, D).astype(F32)
    for i in range(depth):
        q, k, v = _qkv_call(
            x2, ln_mix[i].reshape(1, D).astype(F32), w_qkv_b[i],
            jnp.tile(q_norm[i].astype(F32), reps).reshape(1, MXU_DIM),
            jnp.tile(k_norm[i].astype(F32), reps).reshape(1, MXU_DIM),
            cos, sin, batch=B, seq=S, rope=(i % N_MIXERS == 0), tm=tm)
        if i % N_MIXERS == 0:
            o = _flash_call(q, k, v, tq=tq_flash, tk=tk_flash)
        else:
            o = _window_call(sinks[i // N_MIXERS].astype(F32), q, k, v, bias, tq=tq_win)
        xn, rows, cls, rank, counts = _post_call(
            x2, o.reshape(T, D), w_o_b[i], ln_ffn[i].reshape(1, D).astype(F32), router[i], tm=tm)

        row_start, tile_ea, tile_eb, n_used = _tile_tables(
            counts[0, :N_CLASSES].astype(I32), tmx, n_tiles)
        cls3 = cls[:, 0].reshape(T // ts, 1, ts)
        rank3 = rank[:, 0].reshape(T // ts, 1, ts)
        sorted_rows = _dispatch_call(row_start, cls3, rank3, rows,
                                     jnp.zeros((rows_pad, row_w), U32), ts=ts)
        y = _moe_call(tile_ea, tile_eb, n_used, sorted_rows, w1_b[i], w3_b[i], w2_b[i], tmx=tmx)
        x2 = _combine_call(row_start, cls3, rank3, xn, y, tc=ts)
    return x2.reshape(B, S, D).astype(x.dtype)
```

```python
import functools

import jax
import jax.numpy as jnp
from jax import lax
from jax.experimental import pallas as pl
from jax.experimental.pallas import tpu as pltpu

N_HEADS = 16
N_KV_HEADS = 4
HEAD_DIM = 64
KV_GROUP = N_HEADS // N_KV_HEADS
SCALE = HEAD_DIM ** -0.5
GRID_W = 64
ROPE_THETA = 10000.0
ROPE_AXIS_DIM = HEAD_DIM // 2
Q_BLOCK = 128
WINDOW = 128
N_MIXERS = 2
N_REL_BUCKETS = 32
REL_MAX_DIST = 128
N_GROUPS = 4
EXPERTS_PER_GROUP = 4
N_EXPERTS = N_GROUPS * EXPERTS_PER_GROUP
NORM_EPS = 1e-6
NEG_INF = -1e30
LOG2_E = 1.4426950408889634
V_ROWS = HEAD_DIM + 16

PAIRS_PER_GROUP = 6
N_CLASSES = N_GROUPS * PAIRS_PER_GROUP
PAIR_LO = (0, 0, 0, 1, 1, 2)
PAIR_HI = (1, 2, 3, 2, 3, 3)

LANES = 128
MXU_DIM = 256
VMEM_LIMIT = 48 * 1024 * 1024

ROW_WORDS_PAD = LANES

F32 = jnp.float32
BF16 = jnp.bfloat16
U32 = jnp.uint32
I32 = jnp.int32


def _params(sem, flags=None):
    return pltpu.CompilerParams(dimension_semantics=sem, vmem_limit_bytes=VMEM_LIMIT, flags=flags)


def _segment_sum_matrix():
    r = lax.broadcasted_iota(I32, (MXU_DIM, MXU_DIM), 0) // HEAD_DIM
    c = lax.broadcasted_iota(I32, (MXU_DIM, MXU_DIM), 1) // HEAD_DIM
    return (r == c).astype(BF16)


def _head_rmsnorm(t, gain, seg):
    outs = []
    for c in range(t.shape[1] // MXU_DIM):
        tc = t[:, c * MXU_DIM:(c + 1) * MXU_DIM]
        sq = tc * tc
        hi = sq.astype(BF16)
        lo = (sq - hi.astype(F32)).astype(BF16)
        ss = (jnp.dot(hi, seg, preferred_element_type=F32)
              + jnp.dot(lo, seg, preferred_element_type=F32))
        outs.append(tc * lax.rsqrt(ss * (1.0 / HEAD_DIM) + NORM_EPS) * gain)
    return outs


def _rope(chunks, cos, sin):
    lane = lax.broadcasted_iota(I32, cos.shape, 1)
    upper = (lane & (ROPE_AXIS_DIM // 2)) != 0
    outs = []
    for tc in chunks:
        halves = []
        for j in range(MXU_DIM // LANES):
            xc = tc[:, j * LANES:(j + 1) * LANES]
            partner = jnp.where(upper,
                                pltpu.roll(xc, ROPE_AXIS_DIM // 2, 1),
                                pltpu.roll(xc, LANES - ROPE_AXIS_DIM // 2, 1))
            halves.append(xc * cos + partner * sin)
        outs.append(jnp.concatenate(halves, axis=-1))
    return outs


def _qkv_kernel(x_ref, g_ref, w_ref, qg_ref, kg_ref, cos_ref, sin_ref,
                q_ref, k_ref, v_ref, *, rope, transposed):
    x = x_ref[...]
    ms = jnp.mean(x * x, axis=-1, keepdims=True)
    h = (x * lax.rsqrt(ms + NORM_EPS) * g_ref[...]).astype(BF16)
    qkv = jnp.dot(h, w_ref[...], preferred_element_type=F32)
    nq = N_HEADS * HEAD_DIM
    nk = N_KV_HEADS * HEAD_DIM
    seg = _segment_sum_matrix()
    qs = _head_rmsnorm(qkv[:, :nq], qg_ref[...], seg)
    ks = _head_rmsnorm(qkv[:, nq:nq + nk], kg_ref[...], seg)
    if rope:
        cos = cos_ref[...]
        sin = sin_ref[...]
        qs = _rope(qs, cos, sin)
        ks = _rope(ks, cos, sin)
    heads_per_chunk = MXU_DIM // HEAD_DIM
    v = qkv[:, nq + nk:]
    for c, kc in enumerate(ks):
        for j in range(heads_per_chunk):
            k_ref[0, c * heads_per_chunk + j] = kc[:, j * HEAD_DIM:(j + 1) * HEAD_DIM].astype(BF16)
    if transposed:
        for c, qc in enumerate(qs):
            qt = (qc * (SCALE * LOG2_E)).T
            for j in range(heads_per_chunk):
                q_ref[0, c * heads_per_chunk + j] = qt[j * HEAD_DIM:(j + 1) * HEAD_DIM].astype(BF16)
        vt = v.T
        ones = jnp.ones((V_ROWS - HEAD_DIM, vt.shape[1]), F32)
        for j in range(N_KV_HEADS):
            v_ref[0, j] = jnp.concatenate([vt[j * HEAD_DIM:(j + 1) * HEAD_DIM], ones], axis=0).astype(BF16)
    else:
        for c, qc in enumerate(qs):
            qc = qc * SCALE
            for j in range(heads_per_chunk):
                q_ref[0, c * heads_per_chunk + j] = qc[:, j * HEAD_DIM:(j + 1) * HEAD_DIM].astype(BF16)
        for j in range(N_KV_HEADS):
            v_ref[0, j] = v[:, j * HEAD_DIM:(j + 1) * HEAD_DIM].astype(BF16)


def _qkv_call(x2, gain, w, qg, kg, cos, sin, *, batch, seq, rope, transposed, tm):
    T, D = x2.shape
    spt = seq // tm
    qkv_dim = w.shape[1]
    kern = functools.partial(_qkv_kernel, rope=rope, transposed=transposed)
    head_map = lambda i: (i // spt, 0, i % spt, 0)
    feat_map = lambda i: (i // spt, 0, 0, i % spt)
    if transposed:
        q_spec = pl.BlockSpec((1, N_HEADS, HEAD_DIM, tm), feat_map)
        v_spec = pl.BlockSpec((1, N_KV_HEADS, V_ROWS, tm), feat_map)
        q_shape = (batch, N_HEADS, HEAD_DIM, seq)
        v_shape = (batch, N_KV_HEADS, V_ROWS, seq)
    else:
        q_spec = pl.BlockSpec((1, N_HEADS, tm, HEAD_DIM), head_map)
        v_spec = pl.BlockSpec((1, N_KV_HEADS, tm, HEAD_DIM), head_map)
        q_shape = (batch, N_HEADS, seq, HEAD_DIM)
        v_shape = (batch, N_KV_HEADS, seq, HEAD_DIM)
    return pl.pallas_call(
        kern,
        grid=(T // tm,),
        in_specs=[
            pl.BlockSpec((tm, D), lambda i: (i, 0)),
            pl.BlockSpec((1, D), lambda i: (0, 0)),
            pl.BlockSpec((D, qkv_dim), lambda i: (0, 0)),
            pl.BlockSpec((1, MXU_DIM), lambda i: (0, 0)),
            pl.BlockSpec((1, MXU_DIM), lambda i: (0, 0)),
            pl.BlockSpec((tm, LANES), lambda i: (i % spt, 0)),
            pl.BlockSpec((tm, LANES), lambda i: (i % spt, 0)),
        ],
        out_specs=[
            q_spec,
            pl.BlockSpec((1, N_KV_HEADS, tm, HEAD_DIM), head_map),
            v_spec,
        ],
        out_shape=[
            jax.ShapeDtypeStruct(q_shape, BF16),
            jax.ShapeDtypeStruct((batch, N_KV_HEADS, seq, HEAD_DIM), BF16),
            jax.ShapeDtypeStruct(v_shape, BF16),
        ],
        compiler_params=_params(("parallel",)),
        name="qkv_proj",
    )(x2, gain, w, qg, kg, cos, sin)


def _merge_heads(o, rows):
    return jnp.concatenate([o[g * rows:(g + 1) * rows] for g in range(KV_GROUP)], axis=-1)


FLASH_FLAGS = None


FLASH_RING = 2


def _flash_kernel(qt_ref, k_ref, vt_ref, o_ref, acc_ref, *s_refs, tk):
    tq = qt_ref.shape[3]
    seq = k_ref.shape[2]
    cols = KV_GROUP * tq
    n_chunks = seq // tk
    key_tiles = tk // MXU_DIM

    def head_scores(chunk, g, dst_ref):
        mx = None
        for kt in range(key_tiles):
            off = pl.multiple_of(chunk * tk + kt * MXU_DIM, MXU_DIM)
            s = jnp.dot(k_ref[0, 0, pl.ds(off, MXU_DIM), :], qt_ref[0, g],
                        preferred_element_type=F32)
            dst_ref[kt * MXU_DIM:(kt + 1) * MXU_DIM, g * tq:(g + 1) * tq] = s
            t = jnp.max(s, axis=0, keepdims=True)
            mx = t if mx is None else jnp.maximum(mx, t)
        return mx

    def step(chunk, cur_ref, cur_max, nxt_ref, m):
        nxt_chunk = jnp.minimum(chunk + 1, n_chunks - 1)
        nxt_max, m_out = [], []
        for g in range(KV_GROUP):
            nxt_max.append(head_scores(nxt_chunk, g, nxt_ref))
            m_new = jnp.maximum(m[g], cur_max[g])
            pv = None
            for kt in range(key_tiles):
                off = pl.multiple_of(chunk * tk + kt * MXU_DIM, MXU_DIM)
                s = cur_ref[kt * MXU_DIM:(kt + 1) * MXU_DIM, g * tq:(g + 1) * tq]
                p = jnp.exp2((s - m_new).astype(BF16))
                d = jnp.dot(vt_ref[0, 0, :, pl.ds(off, MXU_DIM)], p, preferred_element_type=F32)
                pv = d if pv is None else pv + d
            acc_ref[:, g * tq:(g + 1) * tq] = (
                jnp.exp2(m[g] - m_new) * acc_ref[:, g * tq:(g + 1) * tq] + pv)
            m_out.append(m_new)
        return tuple(nxt_max), tuple(m_out)

    ring = len(s_refs)

    def body(j, carry):
        mx, m = carry
        for u in range(ring):
            mx, m = step(ring * j + u, s_refs[u], mx, s_refs[(u + 1) % ring], m)
        return mx, m

    acc_ref[...] = jnp.zeros_like(acc_ref)
    init = (tuple(head_scores(0, g, s_refs[0]) for g in range(KV_GROUP)),
            tuple(jnp.full((1, tq), NEG_INF, F32) for _ in range(KV_GROUP)))
    lax.fori_loop(0, n_chunks // ring, body, init)
    acc = acc_ref[...]
    o = acc[:HEAD_DIM] / acc[HEAD_DIM:HEAD_DIM + 1]
    pairs = []
    for g in range(0, KV_GROUP, 2):
        two_heads = jnp.concatenate([o[:, g * tq:(g + 1) * tq], o[:, (g + 1) * tq:(g + 2) * tq]], axis=0)
        pairs.append(two_heads.T)
    o_ref[0] = jnp.concatenate(pairs, axis=-1).astype(BF16)


def _flash_call(q, k, v, *, tq, tk):
    B, _, _, S = q.shape
    kern = functools.partial(_flash_kernel, tk=tk)
    return pl.pallas_call(
        kern,
        grid=(B, N_KV_HEADS, S // tq),
        in_specs=[
            pl.BlockSpec((1, KV_GROUP, HEAD_DIM, tq), lambda b, h, i: (b, h, 0, i)),
            pl.BlockSpec((1, 1, S, HEAD_DIM), lambda b, h, i: (b, h, 0, 0)),
            pl.BlockSpec((1, 1, V_ROWS, S), lambda b, h, i: (b, h, 0, 0)),
        ],
        out_specs=pl.BlockSpec((1, tq, KV_GROUP * HEAD_DIM), lambda b, h, i: (b, i, h)),
        out_shape=jax.ShapeDtypeStruct((B, S, N_HEADS * HEAD_DIM), BF16),
        scratch_shapes=([pltpu.VMEM((V_ROWS, KV_GROUP * tq), F32)]
                        + [pltpu.VMEM((tk, KV_GROUP * tq), F32)] * FLASH_RING),
        compiler_params=_params(("parallel", "parallel", "parallel"), FLASH_FLAGS),
        name="flash_global",
    )(q, k, v)


def _bias_kernel(rb_ref, bucket_ref, out_ref):
    h = pl.program_id(0)
    b = bucket_ref[...]
    acc = jnp.zeros(b.shape, F32)
    for j in range(N_REL_BUCKETS):
        acc = jnp.where(b == j, rb_ref[h, j], acc)
    out_ref[0] = acc


def _bias_call(rel_bias, bucket):
    return pl.pallas_call(
        _bias_kernel,
        grid=(N_HEADS,),
        in_specs=[
            pl.BlockSpec(memory_space=pltpu.SMEM),
            pl.BlockSpec(bucket.shape, lambda h: (0, 0)),
        ],
        out_specs=pl.BlockSpec((1,) + bucket.shape, lambda h: (h, 0, 0)),
        out_shape=jax.ShapeDtypeStruct((N_HEADS,) + bucket.shape, F32),
        compiler_params=_params(("parallel",)),
        name="rel_bias_table",
    )(rel_bias, bucket)


def _window_kernel(sink_ref, q_ref, k_ref, v_ref, bias_ref, o_ref, *, blocks_per_tile):
    hk = pl.program_id(1)
    tile = pl.program_id(2)
    nb = k_ref.shape[2] // Q_BLOCK
    rows = KV_GROUP * Q_BLOCK
    band_w = 3 * Q_BLOCK

    qpos = lax.broadcasted_iota(I32, (rows, band_w), 0) % Q_BLOCK
    col = lax.broadcasted_iota(I32, (rows, band_w), 1)
    in_band = jnp.abs(col - Q_BLOCK - qpos) <= WINDOW
    bias = bias_ref[...].reshape(rows, band_w)

    head = lax.broadcasted_iota(I32, (rows, 1), 0) // Q_BLOCK
    sink = jnp.zeros((rows, 1), F32)
    for g in range(KV_GROUP):
        sink = jnp.where(head == g, sink_ref[hk * KV_GROUP + g], sink)

    def body(j, carry):
        n = tile * blocks_per_tile + j
        off = pl.multiple_of(j * Q_BLOCK, Q_BLOCK)
        qb = q_ref[0, :, pl.ds(off, Q_BLOCK), :].reshape(rows, HEAD_DIM)
        starts = [pl.multiple_of(jnp.maximum(n - 1, 0) * Q_BLOCK, Q_BLOCK),
                  pl.multiple_of(n * Q_BLOCK, Q_BLOCK),
                  pl.multiple_of(jnp.minimum(n + 1, nb - 1) * Q_BLOCK, Q_BLOCK)]
        kcat = jnp.concatenate([k_ref[0, 0, pl.ds(s, Q_BLOCK), :] for s in starts], axis=0)
        vcat = jnp.concatenate([v_ref[0, 0, pl.ds(s, Q_BLOCK), :] for s in starts], axis=0)
        s = lax.dot_general(qb, kcat, (((1,), (1,)), ((), ())), preferred_element_type=F32)
        valid = (in_band
                 & ((col >= Q_BLOCK) | (n > 0))
                 & ((col < 2 * Q_BLOCK) | (n < nb - 1)))
        s = jnp.where(valid, s + bias, NEG_INF)
        m = jnp.maximum(jnp.max(s, axis=-1, keepdims=True), sink)
        p = jnp.exp(s - m)
        denom = jnp.sum(p, axis=-1, keepdims=True) + jnp.exp(sink - m)
        o = jnp.dot(p.astype(BF16), vcat, preferred_element_type=F32) / denom
        o_ref[0, pl.ds(off, Q_BLOCK), :] = _merge_heads(o, Q_BLOCK).astype(BF16)
        return carry

    lax.fori_loop(0, blocks_per_tile, body, 0)


def _window_call(sink, q, k, v, bias, *, tq):
    B, _, S, _ = q.shape
    kern = functools.partial(_window_kernel, blocks_per_tile=tq // Q_BLOCK)
    return pl.pallas_call(
        kern,
        grid=(B, N_KV_HEADS, S // tq),
        in_specs=[
            pl.BlockSpec(memory_space=pltpu.SMEM),
            pl.BlockSpec((1, KV_GROUP, tq, HEAD_DIM), lambda b, h, i: (b, h, i, 0)),
            pl.BlockSpec((1, 1, S, HEAD_DIM), lambda b, h, i: (b, h, 0, 0)),
            pl.BlockSpec((1, 1, S, HEAD_DIM), lambda b, h, i: (b, h, 0, 0)),
            pl.BlockSpec((KV_GROUP, Q_BLOCK, 3 * Q_BLOCK), lambda b, h, i: (h, 0, 0)),
        ],
        out_specs=pl.BlockSpec((1, tq, KV_GROUP * HEAD_DIM), lambda b, h, i: (b, i, h)),
        out_shape=jax.ShapeDtypeStruct((B, S, N_HEADS * HEAD_DIM), BF16),
        compiler_params=_params(("parallel", "parallel", "parallel")),
        name="window_attn",
    )(sink, q, k, v, bias)


def _f32_bits(x):
    return lax.bitcast_convert_type(x, U32)


def _pack_bf16_pair(lo_half, hi_half):
    lo = lax.shift_right_logical(_f32_bits(lo_half.astype(BF16).astype(F32)), jnp.uint32(16))
    hi = _f32_bits(hi_half.astype(BF16).astype(F32)) & jnp.uint32(0xFFFF0000)
    return lo | hi


def _unpack_bf16_pair(words):
    lo = lax.bitcast_convert_type(lax.shift_left(words, jnp.uint32(16)), F32)
    hi = lax.bitcast_convert_type(words & jnp.uint32(0xFFFF0000), F32)
    return lo, hi


def _column_to_lanes(col):
    tm = col.shape[0]
    wide = jnp.broadcast_to(col, (tm, LANES))
    return jnp.concatenate(
        [wide[b * LANES:(b + 1) * LANES].T[0:1] for b in range(tm // LANES)], axis=0)


def _post_kernel(x_ref, o_ref, wo_ref, g_ref, wr_ref,
                 xn_ref, row_ref, cls_ref, rank_ref, cnt_ref, run_ref):
    step = pl.program_id(0)
    tm = x_ref.shape[0]
    half = x_ref.shape[1] // 2

    @pl.when(step == 0)
    def _():
        run_ref[...] = jnp.zeros_like(run_ref)

    xn = x_ref[...] + jnp.dot(o_ref[...], wo_ref[...], preferred_element_type=F32)
    xn_ref[...] = xn
    ms = jnp.mean(xn * xn, axis=-1, keepdims=True)
    t = xn * lax.rsqrt(ms + NORM_EPS) * g_ref[...]
    logits = jnp.dot(t, wr_ref[...], preferred_element_type=F32,
                     precision=lax.Precision.HIGHEST)

    lane = lax.broadcasted_iota(I32, (tm, LANES), 1)
    lane_f = lane.astype(F32)
    big = jnp.float32(LANES)

    def first_argmax(vals):
        top = jnp.max(vals, axis=-1, keepdims=True)
        idx = jnp.min(jnp.where(vals == top, lane_f, big), axis=-1, keepdims=True)
        return top, idx

    is_group = lane < N_GROUPS
    g_top, g_idx = first_argmax(jnp.where(is_group, logits, -jnp.inf))
    g_sum = jnp.sum(jnp.where(is_group, jnp.exp(logits - g_top), 0.0), axis=-1, keepdims=True)
    g_prob = 1.0 / g_sum

    base = N_GROUPS + EXPERTS_PER_GROUP * g_idx
    in_group = (lane_f >= base) & (lane_f < base + EXPERTS_PER_GROUP)
    e_logits = jnp.where(in_group, logits, -jnp.inf)
    e1, i1 = first_argmax(e_logits)
    e2, i2 = first_argmax(jnp.where(lane_f == i1, -jnp.inf, e_logits))
    r = jnp.exp(e2 - e1)
    w_first = (1.0 / (1.0 + r)) * g_prob
    w_second = (r / (1.0 + r)) * g_prob

    j1 = i1 - base
    j2 = i2 - base
    first_is_lo = j1 < j2
    a = jnp.minimum(j1, j2)
    b = jnp.maximum(j1, j2)
    pair = a * 3.0 - a * (a - 1.0) * 0.5 + (b - a - 1.0)
    cls_f = g_idx * PAIRS_PER_GROUP + pair
    w_lo = jnp.where(first_is_lo, w_first, w_second)
    w_hi = jnp.where(first_is_lo, w_second, w_first)

    onehot = lane_f == cls_f
    before = (lax.broadcasted_iota(I32, (tm, tm), 0) > lax.broadcasted_iota(I32, (tm, tm), 1))
    earlier = jnp.dot(before.astype(BF16), onehot.astype(BF16), preferred_element_type=F32)
    rank = jnp.sum(jnp.where(onehot, earlier + run_ref[...], 0.0), axis=-1, keepdims=True)
    run_ref[...] += jnp.sum(onehot.astype(F32), axis=0, keepdims=True)
    cnt_ref[...] = run_ref[...]

    cls_ref[0] = _column_to_lanes(cls_f).astype(I32)
    rank_ref[0] = _column_to_lanes(rank).astype(I32)

    row_ref[:, :half] = _pack_bf16_pair(t[:, :half], t[:, half:])
    weights = jnp.where(lane == 0, w_lo, jnp.where(lane == 1, w_hi, 0.0))
    row_ref[:, half:] = _f32_bits(weights)


def _post_call(x2, o2, wo, gain, wr, *, tm):
    T, D = x2.shape
    row_w = D // 2 + ROW_WORDS_PAD
    return pl.pallas_call(
        _post_kernel,
        grid=(T // tm,),
        in_specs=[
            pl.BlockSpec((tm, D), lambda i: (i, 0)),
            pl.BlockSpec((tm, D), lambda i: (i, 0)),
            pl.BlockSpec((D, D), lambda i: (0, 0)),
            pl.BlockSpec((1, D), lambda i: (0, 0)),
            pl.BlockSpec((D, LANES), lambda i: (0, 0)),
        ],
        out_specs=[
            pl.BlockSpec((tm, D), lambda i: (i, 0)),
            pl.BlockSpec((tm, row_w), lambda i: (i, 0)),
            pl.BlockSpec((1, tm // LANES, LANES), lambda i: (i, 0, 0)),
            pl.BlockSpec((1, tm // LANES, LANES), lambda i: (i, 0, 0)),
            pl.BlockSpec((1, LANES), lambda i: (0, 0)),
        ],
        out_shape=[
            jax.ShapeDtypeStruct((T, D), F32),
            jax.ShapeDtypeStruct((T, row_w), U32),
            jax.ShapeDtypeStruct((T // tm, tm // LANES, LANES), I32),
            jax.ShapeDtypeStruct((T // tm, tm // LANES, LANES), I32),
            jax.ShapeDtypeStruct((1, LANES), F32),
        ],
        scratch_shapes=[pltpu.VMEM((1, LANES), F32)],
        compiler_params=_params(("arbitrary",)),
        name="post_attn_router",
    )(x2, o2, wo, gain, wr)


def _pos_kernel(start_ref, cls_ref, rank_ref, pos_ref):
    cls = cls_ref[...]
    base = jnp.zeros(cls.shape, I32)
    for c in range(N_CLASSES):
        base = jnp.where(cls == c, start_ref[c], base)
    pos_ref[...] = base + rank_ref[...]


def _pos_call(row_start, cls, rank):
    return pl.pallas_call(
        _pos_kernel,
        in_specs=[
            pl.BlockSpec(memory_space=pltpu.SMEM),
            pl.BlockSpec(memory_space=pltpu.VMEM),
            pl.BlockSpec(memory_space=pltpu.VMEM),
        ],
        out_specs=pl.BlockSpec(memory_space=pltpu.VMEM),
        out_shape=jax.ShapeDtypeStruct(cls.shape, I32),
        name="moe_positions",
    )(row_start, cls, rank)


ROW_DMA_UNROLL = 8


def _row_copy(src_ref, src_row, dst_ref, dst_row, sem):
    return pltpu.make_async_copy(src_ref.at[pl.ds(src_row, 1)], dst_ref.at[pl.ds(dst_row, 1)], sem)


def _issue_row_copies(n_rows, make_copy):
    def group(g, carry):
        for u in range(ROW_DMA_UNROLL):
            make_copy(g * ROW_DMA_UNROLL + u).start(priority=u % 2)
        return carry

    lax.fori_loop(0, n_rows // ROW_DMA_UNROLL, group, 0)


def _dispatch_kernel(pos_ref, row_ref, init_ref, out_ref, sem):
    del init_ref
    ts = row_ref.shape[0]
    _issue_row_copies(ts, lambda r: _row_copy(row_ref, r, out_ref, pos_ref[0, 0, r], sem))
    pltpu.make_async_copy(row_ref, out_ref.at[pl.ds(0, ts)], sem).wait()


def _dispatch_call(pos3, rows, init, *, ts):
    T, row_w = rows.shape
    return pl.pallas_call(
        _dispatch_kernel,
        grid=(T // ts,),
        in_specs=[
            pl.BlockSpec((1, 1, ts), lambda i: (i, 0, 0), memory_space=pltpu.SMEM),
            pl.BlockSpec((ts, row_w), lambda i: (i, 0)),
            pl.BlockSpec(memory_space=pl.ANY),
        ],
        out_specs=pl.BlockSpec(memory_space=pl.ANY),
        out_shape=jax.ShapeDtypeStruct(init.shape, init.dtype),
        scratch_shapes=[pltpu.SemaphoreType.DMA(())],
        input_output_aliases={2: 0},
        compiler_params=_params(("arbitrary",)),
        name="moe_dispatch",
    )(pos3, rows, init)


def _moe_kernel(ea_ref, eb_ref, used_ref, row_ref, w1a_ref, w3a_ref, w2a_ref,
                w1b_ref, w3b_ref, w2b_ref, y_ref):
    del ea_ref, eb_ref
    i = pl.program_id(0)
    half = row_ref.shape[1] - ROW_WORDS_PAD

    @pl.when(i < used_ref[0])
    def _():
        lo, hi = _unpack_bf16_pair(row_ref[:, :half])
        x = jnp.concatenate([lo, hi], axis=-1).astype(BF16)
        wts = lax.bitcast_convert_type(row_ref[:, half:], F32)
        w_lo = wts[:, 0:1]
        w_hi = wts[:, 1:2]

        def expert(w1_ref, w3_ref, w2_ref):
            z = jnp.dot(x, w1_ref[0], preferred_element_type=F32)
            u = jnp.dot(x, w3_ref[0], preferred_element_type=F32)
            h = (z * (1.0 / (1.0 + jnp.exp(-z)))) * u
            return jnp.dot(h.astype(BF16), w2_ref[0], preferred_element_type=F32)

        y_ref[...] = (w_lo * expert(w1a_ref, w3a_ref, w2a_ref)
                      + w_hi * expert(w1b_ref, w3b_ref, w2b_ref))

    @pl.when(i >= used_ref[0])
    def _():
        y_ref[...] = jnp.zeros_like(y_ref)


def _moe_call(tile_ea, tile_eb, n_used, rows, w1, w3, w2, *, tmx):
    rows_pad, row_w = rows.shape
    _, D, F = w1.shape
    wa = lambda i, ea, eb, used: (ea[i], 0, 0)
    wb = lambda i, ea, eb, used: (eb[i], 0, 0)
    grid_spec = pltpu.PrefetchScalarGridSpec(
        num_scalar_prefetch=3,
        grid=(rows_pad // tmx,),
        in_specs=[
            pl.BlockSpec((tmx, row_w), lambda i, ea, eb, used: (i, 0)),
            pl.BlockSpec((1, D, F), wa),
            pl.BlockSpec((1, D, F), wa),
            pl.BlockSpec((1, F, D), wa),
            pl.BlockSpec((1, D, F), wb),
            pl.BlockSpec((1, D, F), wb),
            pl.BlockSpec((1, F, D), wb),
        ],
        out_specs=pl.BlockSpec((tmx, D), lambda i, ea, eb, used: (i, 0)),
    )
    return pl.pallas_call(
        _moe_kernel,
        grid_spec=grid_spec,
        out_shape=jax.ShapeDtypeStruct((rows_pad, D), F32),
        compiler_params=_params(("arbitrary",)),
        name="moe_experts",
    )(tile_ea, tile_eb, n_used, rows, w1, w3, w2, w1, w3, w2)


def _combine_kernel(pos_ref, pos_next_ref, x_ref, y_ref, out_ref, buf_ref, sem):
    i = pl.program_id(0)
    n = pl.num_programs(0)
    tc = x_ref.shape[0]
    slot = i % 2

    def gather(p_ref, s):
        _issue_row_copies(
            tc, lambda r: _row_copy(y_ref, p_ref[0, 0, r], buf_ref.at[s], r, sem.at[s]))

    @pl.when(i == 0)
    def _():
        gather(pos_ref, 0)

    @pl.when(i + 1 < n)
    def _():
        gather(pos_next_ref, 1 - slot)

    pltpu.make_async_copy(y_ref.at[pl.ds(0, tc)], buf_ref.at[slot], sem.at[slot]).wait()
    out_ref[...] = x_ref[...] + buf_ref[slot]


def _combine_call(pos3, xn, y, *, tc):
    T, D = xn.shape
    n = T // tc
    return pl.pallas_call(
        _combine_kernel,
        grid=(n,),
        in_specs=[
            pl.BlockSpec((1, 1, tc), lambda i: (i, 0, 0), memory_space=pltpu.SMEM),
            pl.BlockSpec((1, 1, tc), lambda i: (jnp.minimum(i + 1, n - 1), 0, 0), memory_space=pltpu.SMEM),
            pl.BlockSpec((tc, D), lambda i: (i, 0)),
            pl.BlockSpec(memory_space=pl.ANY),
        ],
        out_specs=pl.BlockSpec((tc, D), lambda i: (i, 0)),
        out_shape=jax.ShapeDtypeStruct((T, D), F32),
        scratch_shapes=[pltpu.VMEM((2, tc, D), F32), pltpu.SemaphoreType.DMA((2,))],
        compiler_params=_params(("arbitrary",)),
        name="moe_combine",
    )(pos3, pos3, xn, y)


def _rope_tables(seq):
    half = ROPE_AXIS_DIM // 2
    freqs = ROPE_THETA ** (-(jnp.arange(half, dtype=F32) * 2.0 / ROPE_AXIS_DIM))
    t = jnp.arange(seq)
    ang_row = (t // GRID_W).astype(F32)[:, None] * freqs[None, :]
    ang_col = (t % GRID_W).astype(F32)[:, None] * freqs[None, :]

    def axis_tables(ang):
        c, s = jnp.cos(ang), jnp.sin(ang)
        return jnp.concatenate([c, c], axis=-1), jnp.concatenate([-s, s], axis=-1)

    cr, sr = axis_tables(ang_row)
    cc, sc = axis_tables(ang_col)
    cos = jnp.concatenate([cr, cc], axis=-1)
    sin = jnp.concatenate([sr, sc], axis=-1)
    reps = LANES // HEAD_DIM
    return jnp.tile(cos, (1, reps)), jnp.tile(sin, (1, reps))


def _t5_bucket_table():
    import math
    rel = jnp.arange(3 * Q_BLOCK)[None, :] - Q_BLOCK - jnp.arange(Q_BLOCK)[:, None]
    nb = N_REL_BUCKETS // 2
    max_exact = nb // 2
    ret = jnp.where(rel > 0, nb, 0)
    n = jnp.abs(rel)
    large = max_exact + (jnp.log(jnp.maximum(n, 1).astype(F32) / max_exact)
                         / math.log(REL_MAX_DIST / max_exact) * (nb - max_exact)).astype(I32)
    large = jnp.minimum(large, nb - 1)
    return (ret + jnp.where(n < max_exact, n, large)).astype(I32)


def _tile_tables(counts, tmx, n_tiles):
    tiles_per_class = (counts + tmx - 1) // tmx
    tile_end = jnp.cumsum(tiles_per_class)
    row_start = ((tile_end - tiles_per_class) * tmx).astype(I32)
    n_used = tile_end[-1].astype(I32)
    tile_id = jnp.minimum(jnp.arange(n_tiles, dtype=I32), jnp.maximum(n_used - 1, 0))
    tile_cls = jnp.minimum(jnp.sum(tile_id[:, None] >= tile_end[None, :], axis=1), N_CLASSES - 1).astype(I32)
    group = tile_cls // PAIRS_PER_GROUP
    pair = tile_cls % PAIRS_PER_GROUP
    tile_ea = group * EXPERTS_PER_GROUP + jnp.asarray(PAIR_LO, I32)[pair]
    tile_eb = group * EXPERTS_PER_GROUP + jnp.asarray(PAIR_HI, I32)[pair]
    return row_start, tile_ea.astype(I32), tile_eb.astype(I32), n_used.reshape(1)


def _tile(n, pref):
    t = min(n, pref)
    assert n % t == 0, (n, pref)
    return t


def kernel(x, ln_mix, w_qkv, q_norm, k_norm, w_o, rel_bias, sinks, ln_ffn, w_group, w_expert, w1, w3, w2):
    B, S, D = x.shape
    depth = w_qkv.shape[0]
    T = B * S
    assert S % GRID_W == 0 and S % Q_BLOCK == 0 and D == N_HEADS * HEAD_DIM

    tm = _tile(S, 512)
    tq_flash = _tile(S, 256)
    tk_flash = _tile(S, 512)
    assert (S // tk_flash) % FLASH_RING == 0
    tq_win = _tile(S, 1024)
    ts = _tile(T, 256)
    tmx = _tile(T, 256)
    n_tiles = T // tmx + N_CLASSES
    rows_pad = n_tiles * tmx
    row_w = D // 2 + ROW_WORDS_PAD

    cos, sin = _rope_tables(S)
    bias = _bias_call(rel_bias.astype(F32), _t5_bucket_table())

    w_qkv_b = w_qkv.astype(BF16)
    w_o_b = w_o.astype(BF16)
    w1_b, w3_b, w2_b = w1.astype(BF16), w3.astype(BF16), w2.astype(BF16)
    reps = MXU_DIM // HEAD_DIM
    router = jnp.concatenate(
        [w_group, w_expert,
         jnp.zeros((depth, D, LANES - N_GROUPS - N_EXPERTS), F32)], axis=-1).astype(F32)

    x2 = x.reshape(T, D).astype(F32)
    sorted_rows = jnp.zeros((rows_pad, row_w), U32)
    for i in range(depth):
        q, k, v = _qkv_call(
            x2, ln_mix[i].reshape(1, D).astype(F32), w_qkv_b[i],
            jnp.tile(q_norm[i].astype(F32), reps).reshape(1, MXU_DIM),
            jnp.tile(k_norm[i].astype(F32), reps).reshape(1, MXU_DIM),
            cos, sin, batch=B, seq=S, rope=(i % N_MIXERS == 0), transposed=(i % N_MIXERS == 0), tm=tm)
        if i % N_MIXERS == 0:
            o = _flash_call(q, k, v, tq=tq_flash, tk=tk_flash)
        else:
            o = _window_call(sinks[i // N_MIXERS].astype(F32), q, k, v, bias, tq=tq_win)
        xn, rows, cls, rank, counts = _post_call(
            x2, o.reshape(T, D), w_o_b[i], ln_ffn[i].reshape(1, D).astype(F32), router[i], tm=tm)

        row_start, tile_ea, tile_eb, n_used = _tile_tables(
            counts[0, :N_CLASSES].astype(I32), tmx, n_tiles)
        pos = _pos_call(row_start, cls.reshape(T // LANES, LANES), rank.reshape(T // LANES, LANES))
        pos3 = pos.reshape(T // ts, 1, ts)
        sorted_rows = _dispatch_call(pos3, rows, sorted_rows, ts=ts)
        y = _moe_call(tile_ea, tile_eb, n_used, sorted_rows, w1_b[i], w3_b[i], w2_b[i], tmx=tmx)
        x2 = _combine_call(pos3, xn, y, tc=ts)
    return x2.reshape(B, S, D).astype(x.dtype)
```

```python
import functools

import jax
import jax.numpy as jnp
from jax import lax
from jax.experimental import pallas as pl
from jax.experimental.pallas import tpu as pltpu

N_HEADS = 16
N_KV_HEADS = 4
HEAD_DIM = 64
KV_GROUP = N_HEADS // N_KV_HEADS
SCALE = HEAD_DIM ** -0.5
GRID_W = 64
ROPE_THETA = 10000.0
ROPE_AXIS_DIM = HEAD_DIM // 2
Q_BLOCK = 128
WINDOW = 128
N_MIXERS = 2
N_REL_BUCKETS = 32
REL_MAX_DIST = 128
N_GROUPS = 4
EXPERTS_PER_GROUP = 4
N_EXPERTS = N_GROUPS * EXPERTS_PER_GROUP
NORM_EPS = 1e-6
NEG_INF = -1e30
LOG2_E = 1.4426950408889634
V_ROWS = HEAD_DIM + 16

PAIRS_PER_GROUP = 6
N_CLASSES = N_GROUPS * PAIRS_PER_GROUP
PAIR_LO = (0, 0, 0, 1, 1, 2)
PAIR_HI = (1, 2, 3, 2, 3, 3)

LANES = 128
MXU_DIM = 256
VMEM_LIMIT = 48 * 1024 * 1024

ROW_WORDS_PAD = LANES

F32 = jnp.float32
BF16 = jnp.bfloat16
U32 = jnp.uint32
I32 = jnp.int32


def _params(sem, flags=None):
    return pltpu.CompilerParams(dimension_semantics=sem, vmem_limit_bytes=VMEM_LIMIT, flags=flags)


def _segment_sum_matrix():
    r = lax.broadcasted_iota(I32, (MXU_DIM, MXU_DIM), 0) // HEAD_DIM
    c = lax.broadcasted_iota(I32, (MXU_DIM, MXU_DIM), 1) // HEAD_DIM
    return (r == c).astype(BF16)


def _head_rmsnorm(t, gain, seg):
    outs = []
    for c in range(t.shape[1] // MXU_DIM):
        tc = t[:, c * MXU_DIM:(c + 1) * MXU_DIM]
        sq = tc * tc
        hi = sq.astype(BF16)
        lo = (sq - hi.astype(F32)).astype(BF16)
        ss = (jnp.dot(hi, seg, preferred_element_type=F32)
              + jnp.dot(lo, seg, preferred_element_type=F32))
        outs.append(tc * lax.rsqrt(ss * (1.0 / HEAD_DIM) + NORM_EPS) * gain)
    return outs


def _rope(chunks, cos, sin):
    lane = lax.broadcasted_iota(I32, cos.shape, 1)
    upper = (lane & (ROPE_AXIS_DIM // 2)) != 0
    outs = []
    for tc in chunks:
        halves = []
        for j in range(MXU_DIM // LANES):
            xc = tc[:, j * LANES:(j + 1) * LANES]
            partner = jnp.where(upper,
                                pltpu.roll(xc, ROPE_AXIS_DIM // 2, 1),
                                pltpu.roll(xc, LANES - ROPE_AXIS_DIM // 2, 1))
            halves.append(xc * cos + partner * sin)
        outs.append(jnp.concatenate(halves, axis=-1))
    return outs


def _qkv_kernel(x_ref, g_ref, w_ref, qg_ref, kg_ref, cos_ref, sin_ref,
                q_ref, k_ref, v_ref, *, rope, transposed):
    x = x_ref[...]
    ms = jnp.mean(x * x, axis=-1, keepdims=True)
    h = (x * lax.rsqrt(ms + NORM_EPS) * g_ref[...]).astype(BF16)
    qkv = jnp.dot(h, w_ref[...], preferred_element_type=F32)
    nq = N_HEADS * HEAD_DIM
    nk = N_KV_HEADS * HEAD_DIM
    seg = _segment_sum_matrix()
    qs = _head_rmsnorm(qkv[:, :nq], qg_ref[...], seg)
    ks = _head_rmsnorm(qkv[:, nq:nq + nk], kg_ref[...], seg)
    if rope:
        cos = cos_ref[...]
        sin = sin_ref[...]
        qs = _rope(qs, cos, sin)
        ks = _rope(ks, cos, sin)
    heads_per_chunk = MXU_DIM // HEAD_DIM
    v = qkv[:, nq + nk:]
    for c, kc in enumerate(ks):
        for j in range(heads_per_chunk):
            k_ref[0, c * heads_per_chunk + j] = kc[:, j * HEAD_DIM:(j + 1) * HEAD_DIM].astype(BF16)
    if transposed:
        for c, qc in enumerate(qs):
            qt = (qc * (SCALE * LOG2_E)).T
            for j in range(heads_per_chunk):
                q_ref[0, c * heads_per_chunk + j] = qt[j * HEAD_DIM:(j + 1) * HEAD_DIM].astype(BF16)
        vt = v.T
        ones = jnp.ones((V_ROWS - HEAD_DIM, vt.shape[1]), F32)
        for j in range(N_KV_HEADS):
            v_ref[0, j] = jnp.concatenate([vt[j * HEAD_DIM:(j + 1) * HEAD_DIM], ones], axis=0).astype(BF16)
    else:
        for c, qc in enumerate(qs):
            qc = qc * SCALE
            for j in range(heads_per_chunk):
                q_ref[0, c * heads_per_chunk + j] = qc[:, j * HEAD_DIM:(j + 1) * HEAD_DIM].astype(BF16)
        ones = jnp.ones((v.shape[0], LANES - HEAD_DIM), F32)
        for j in range(N_KV_HEADS):
            v_ref[0, j] = jnp.concatenate(
                [v[:, j * HEAD_DIM:(j + 1) * HEAD_DIM], ones], axis=-1).astype(BF16)


def _qkv_call(x2, gain, w, qg, kg, cos, sin, *, batch, seq, rope, transposed, tm):
    T, D = x2.shape
    spt = seq // tm
    qkv_dim = w.shape[1]
    kern = functools.partial(_qkv_kernel, rope=rope, transposed=transposed)
    head_map = lambda i: (i // spt, 0, i % spt, 0)
    feat_map = lambda i: (i // spt, 0, 0, i % spt)
    if transposed:
        q_spec = pl.BlockSpec((1, N_HEADS, HEAD_DIM, tm), feat_map)
        v_spec = pl.BlockSpec((1, N_KV_HEADS, V_ROWS, tm), feat_map)
        q_shape = (batch, N_HEADS, HEAD_DIM, seq)
        v_shape = (batch, N_KV_HEADS, V_ROWS, seq)
    else:
        q_spec = pl.BlockSpec((1, N_HEADS, tm, HEAD_DIM), head_map)
        v_spec = pl.BlockSpec((1, N_KV_HEADS, tm, LANES), head_map)
        q_shape = (batch, N_HEADS, seq, HEAD_DIM)
        v_shape = (batch, N_KV_HEADS, seq, LANES)
    return pl.pallas_call(
        kern,
        grid=(T // tm,),
        in_specs=[
            pl.BlockSpec((tm, D), lambda i: (i, 0)),
            pl.BlockSpec((1, D), lambda i: (0, 0)),
            pl.BlockSpec((D, qkv_dim), lambda i: (0, 0)),
            pl.BlockSpec((1, MXU_DIM), lambda i: (0, 0)),
            pl.BlockSpec((1, MXU_DIM), lambda i: (0, 0)),
            pl.BlockSpec((tm, LANES), lambda i: (i % spt, 0)),
            pl.BlockSpec((tm, LANES), lambda i: (i % spt, 0)),
        ],
        out_specs=[
            q_spec,
            pl.BlockSpec((1, N_KV_HEADS, tm, HEAD_DIM), head_map),
            v_spec,
        ],
        out_shape=[
            jax.ShapeDtypeStruct(q_shape, BF16),
            jax.ShapeDtypeStruct((batch, N_KV_HEADS, seq, HEAD_DIM), BF16),
            jax.ShapeDtypeStruct(v_shape, BF16),
        ],
        compiler_params=_params(("parallel",)),
        name="qkv_proj",
    )(x2, gain, w, qg, kg, cos, sin)


def _merge_heads(o, rows):
    return jnp.concatenate([o[g * rows:(g + 1) * rows] for g in range(KV_GROUP)], axis=-1)


FLASH_FLAGS = None


FLASH_RING = 4


def _flash_kernel(qt_ref, k_ref, vt_ref, o_ref, acc_ref, *s_refs, tk):
    ring = len(s_refs)
    tq = qt_ref.shape[3]
    seq = k_ref.shape[2]
    n_chunks = seq // tk
    key_tiles = tk // MXU_DIM
    heads = range(KV_GROUP)

    def tile(g, kt):
        return (slice(kt * MXU_DIM, (kt + 1) * MXU_DIM), slice(g * tq, (g + 1) * tq))

    def qk_head(chunk, g, dst_ref):
        mx = None
        for kt in range(key_tiles):
            off = pl.multiple_of(chunk * tk + kt * MXU_DIM, MXU_DIM)
            s = jnp.dot(k_ref[0, 0, pl.ds(off, MXU_DIM), :], qt_ref[0, g],
                        preferred_element_type=F32)
            dst_ref[tile(g, kt)] = s
            t = jnp.max(s, axis=0, keepdims=True)
            mx = t if mx is None else jnp.maximum(mx, t)
        return mx

    def attend_head(chunk, g, src_ref, mx, m):
        m_new = jnp.maximum(m, mx)
        pv = None
        for kt in range(key_tiles):
            off = pl.multiple_of(chunk * tk + kt * MXU_DIM, MXU_DIM)
            p = jnp.exp2(src_ref[tile(g, kt)] - m_new).astype(BF16)
            d = jnp.dot(vt_ref[0, 0, :, pl.ds(off, MXU_DIM)], p, preferred_element_type=F32)
            pv = d if pv is None else pv + d
        cols = slice(g * tq, (g + 1) * tq)
        acc_ref[:, cols] = jnp.exp2(m - m_new) * acc_ref[:, cols] + pv
        return m_new

    def step(chunk, u, mx, m, last=False):
        mx_next, m_out = [], []
        for g in heads:
            if not last:
                mx_next.append(qk_head(chunk + 1, g, s_refs[(u + 1) % ring]))
            m_out.append(attend_head(chunk, g, s_refs[u], mx[g], m[g]))
        return tuple(mx_next), tuple(m_out)

    def body(j, carry):
        mx, m = carry
        for u in range(ring):
            mx, m = step(ring * j + u, u, mx, m)
        return mx, m

    acc_ref[...] = jnp.zeros_like(acc_ref)
    m = tuple(jnp.full((1, tq), NEG_INF, F32) for _ in heads)
    mx = tuple(qk_head(0, g, s_refs[0]) for g in heads)
    loops = (n_chunks - 1) // ring
    mx, m = lax.fori_loop(0, loops, body, (mx, m))
    for c in range(loops * ring, n_chunks - 1):
        mx, m = step(c, c % ring, mx, m)
    step(n_chunks - 1, (n_chunks - 1) % ring, mx, m, last=True)
    acc = acc_ref[...]
    o = acc[:HEAD_DIM] / acc[HEAD_DIM:HEAD_DIM + 1]
    pairs = []
    for g in range(0, KV_GROUP, 2):
        two_heads = jnp.concatenate([o[:, g * tq:(g + 1) * tq], o[:, (g + 1) * tq:(g + 2) * tq]], axis=0)
        pairs.append(two_heads.T)
    o_ref[0] = jnp.concatenate(pairs, axis=-1).astype(BF16)


def _flash_call(q, k, v, *, tq, tk):
    B, _, _, S = q.shape
    kern = functools.partial(_flash_kernel, tk=tk)
    return pl.pallas_call(
        kern,
        grid=(B, N_KV_HEADS, S // tq),
        in_specs=[
            pl.BlockSpec((1, KV_GROUP, HEAD_DIM, tq), lambda b, h, i: (b, h, 0, i)),
            pl.BlockSpec((1, 1, S, HEAD_DIM), lambda b, h, i: (b, h, 0, 0)),
            pl.BlockSpec((1, 1, V_ROWS, S), lambda b, h, i: (b, h, 0, 0)),
        ],
        out_specs=pl.BlockSpec((1, tq, KV_GROUP * HEAD_DIM), lambda b, h, i: (b, i, h)),
        out_shape=jax.ShapeDtypeStruct((B, S, N_HEADS * HEAD_DIM), BF16),
        scratch_shapes=([pltpu.VMEM((V_ROWS, KV_GROUP * tq), F32)]
                        + [pltpu.VMEM((tk, KV_GROUP * tq), F32)] * FLASH_RING),
        compiler_params=_params(("parallel", "parallel", "parallel"), FLASH_FLAGS),
        name="flash_global",
    )(q, k, v)


EDGE_VARIANTS = 4
WINDOW_UNROLL = 8


def _bias_kernel(rb_ref, bucket_ref, out_ref):
    h = pl.program_id(0)
    b = bucket_ref[...]
    bias = jnp.zeros(b.shape, F32)
    for j in range(N_REL_BUCKETS):
        bias = jnp.where(b == j, rb_ref[h, j], bias)
    qpos = lax.broadcasted_iota(I32, b.shape, 0)
    col = lax.broadcasted_iota(I32, b.shape, 1)
    in_band = jnp.abs(col - Q_BLOCK - qpos) <= WINDOW
    for e in range(EDGE_VARIANTS):
        valid = in_band
        if e & 1:
            valid = valid & (col >= Q_BLOCK)
        if e & 2:
            valid = valid & (col < 2 * Q_BLOCK)
        out_ref[e, 0] = jnp.where(valid, bias, NEG_INF)


def _bias_call(rel_bias, bucket):
    return pl.pallas_call(
        _bias_kernel,
        grid=(N_HEADS,),
        in_specs=[
            pl.BlockSpec(memory_space=pltpu.SMEM),
            pl.BlockSpec(bucket.shape, lambda h: (0, 0)),
        ],
        out_specs=pl.BlockSpec((EDGE_VARIANTS, 1) + bucket.shape, lambda h: (0, h, 0, 0)),
        out_shape=jax.ShapeDtypeStruct((EDGE_VARIANTS, N_HEADS) + bucket.shape, F32),
        compiler_params=_params(("parallel",)),
        name="rel_bias_table",
    )(rel_bias, bucket)


def _window_kernel(sink_ref, q_ref, k_ref, v_ref, bias_ref, o_ref, *, blocks_per_tile):
    hk = pl.program_id(1)
    tile = pl.program_id(2)
    nb = k_ref.shape[2] // Q_BLOCK
    rows = KV_GROUP * Q_BLOCK
    band_w = 3 * Q_BLOCK

    head = lax.broadcasted_iota(I32, (rows, 1), 0) // Q_BLOCK
    sink = jnp.zeros((rows, 1), F32)
    for g in range(KV_GROUP):
        sink = jnp.where(head == g, sink_ref[hk * KV_GROUP + g], sink)

    def body(j, carry):
        n = tile * blocks_per_tile + j
        off = pl.multiple_of(j * Q_BLOCK, Q_BLOCK)
        qb = q_ref[0, :, pl.ds(off, Q_BLOCK), :].reshape(rows, HEAD_DIM)
        starts = [pl.multiple_of(jnp.maximum(n - 1, 0) * Q_BLOCK, Q_BLOCK),
                  pl.multiple_of(n * Q_BLOCK, Q_BLOCK),
                  pl.multiple_of(jnp.minimum(n + 1, nb - 1) * Q_BLOCK, Q_BLOCK)]
        kcat = jnp.concatenate([k_ref[0, 0, pl.ds(s, Q_BLOCK), :] for s in starts], axis=0)
        vcat = jnp.concatenate([v_ref[0, 0, pl.ds(s, Q_BLOCK), :] for s in starts], axis=0)
        edge = (n == 0).astype(I32) + 2 * (n == nb - 1).astype(I32)
        s = (lax.dot_general(qb, kcat, (((1,), (1,)), ((), ())), preferred_element_type=F32)
             + bias_ref[edge].reshape(rows, band_w))
        m = jnp.maximum(jnp.max(s, axis=-1, keepdims=True), sink)
        p = jnp.exp(s - m).astype(BF16)
        ov = jnp.dot(p, vcat, preferred_element_type=F32)
        denom = ov[:, HEAD_DIM:HEAD_DIM + 1] + jnp.exp(sink - m)
        o = ov[:, :HEAD_DIM] / denom
        o_ref[0, pl.ds(off, Q_BLOCK), :] = _merge_heads(o, Q_BLOCK).astype(BF16)
        return carry

    lax.fori_loop(0, blocks_per_tile, body, 0, unroll=WINDOW_UNROLL)


def _window_call(sink, q, k, v, bias, *, tq):
    B, _, S, _ = q.shape
    kern = functools.partial(_window_kernel, blocks_per_tile=tq // Q_BLOCK)
    return pl.pallas_call(
        kern,
        grid=(B, N_KV_HEADS, S // tq),
        in_specs=[
            pl.BlockSpec(memory_space=pltpu.SMEM),
            pl.BlockSpec((1, KV_GROUP, tq, HEAD_DIM), lambda b, h, i: (b, h, i, 0)),
            pl.BlockSpec((1, 1, S, HEAD_DIM), lambda b, h, i: (b, h, 0, 0)),
            pl.BlockSpec((1, 1, S, LANES), lambda b, h, i: (b, h, 0, 0)),
            pl.BlockSpec((EDGE_VARIANTS, KV_GROUP, Q_BLOCK, 3 * Q_BLOCK), lambda b, h, i: (0, h, 0, 0)),
        ],
        out_specs=pl.BlockSpec((1, tq, KV_GROUP * HEAD_DIM), lambda b, h, i: (b, i, h)),
        out_shape=jax.ShapeDtypeStruct((B, S, N_HEADS * HEAD_DIM), BF16),
        compiler_params=_params(("parallel", "parallel", "parallel")),
        name="window_attn",
    )(sink, q, k, v, bias)


def _f32_bits(x):
    return lax.bitcast_convert_type(x, U32)


def _pack_bf16_pair(lo_half, hi_half):
    lo = lax.shift_right_logical(_f32_bits(lo_half.astype(BF16).astype(F32)), jnp.uint32(16))
    hi = _f32_bits(hi_half.astype(BF16).astype(F32)) & jnp.uint32(0xFFFF0000)
    return lo | hi


def _unpack_bf16_pair(words):
    lo = lax.bitcast_convert_type(lax.shift_left(words, jnp.uint32(16)), F32)
    hi = lax.bitcast_convert_type(words & jnp.uint32(0xFFFF0000), F32)
    return lo, hi


def _column_to_lanes(col):
    tm = col.shape[0]
    wide = jnp.broadcast_to(col, (tm, LANES))
    return jnp.concatenate(
        [wide[b * LANES:(b + 1) * LANES].T[0:1] for b in range(tm // LANES)], axis=0)


def _post_kernel(x_ref, o_ref, wo_ref, g_ref, wr_ref,
                 xn_ref, row_ref, cls_ref, rank_ref, cnt_ref, run_ref):
    step = pl.program_id(0)
    tm = x_ref.shape[0]
    half = x_ref.shape[1] // 2

    @pl.when(step == 0)
    def _():
        run_ref[...] = jnp.zeros_like(run_ref)

    xn = x_ref[...] + jnp.dot(o_ref[...], wo_ref[...], preferred_element_type=F32)
    xn_ref[...] = xn
    ms = jnp.mean(xn * xn, axis=-1, keepdims=True)
    t = xn * lax.rsqrt(ms + NORM_EPS) * g_ref[...]
    t_hi = t.astype(BF16)
    t_lo = (t - t_hi.astype(F32)).astype(BF16)
    hi_both = jnp.dot(t_hi, wr_ref[...], preferred_element_type=F32)
    lo_hi = jnp.dot(t_lo, wr_ref[:, :LANES], preferred_element_type=F32)
    logits = hi_both[:, :LANES] + (hi_both[:, LANES:] + lo_hi)

    lane = lax.broadcasted_iota(I32, (tm, LANES), 1)
    lane_f = lane.astype(F32)
    big = jnp.float32(LANES)

    def first_argmax(vals):
        top = jnp.max(vals, axis=-1, keepdims=True)
        idx = jnp.min(jnp.where(vals == top, lane_f, big), axis=-1, keepdims=True)
        return top, idx

    is_group = lane < N_GROUPS
    g_top, g_idx = first_argmax(jnp.where(is_group, logits, -jnp.inf))
    g_sum = jnp.sum(jnp.where(is_group, jnp.exp(logits - g_top), 0.0), axis=-1, keepdims=True)
    g_prob = 1.0 / g_sum

    base = N_GROUPS + EXPERTS_PER_GROUP * g_idx
    in_group = (lane_f >= base) & (lane_f < base + EXPERTS_PER_GROUP)
    e_logits = jnp.where(in_group, logits, -jnp.inf)
    e1, i1 = first_argmax(e_logits)
    e2, i2 = first_argmax(jnp.where(lane_f == i1, -jnp.inf, e_logits))
    r = jnp.exp(e2 - e1)
    w_first = (1.0 / (1.0 + r)) * g_prob
    w_second = (r / (1.0 + r)) * g_prob

    j1 = i1 - base
    j2 = i2 - base
    first_is_lo = j1 < j2
    a = jnp.minimum(j1, j2)
    b = jnp.maximum(j1, j2)
    pair = a * 3.0 - a * (a - 1.0) * 0.5 + (b - a - 1.0)
    cls_f = g_idx * PAIRS_PER_GROUP + pair
    w_lo = jnp.where(first_is_lo, w_first, w_second)
    w_hi = jnp.where(first_is_lo, w_second, w_first)

    onehot = lane_f == cls_f
    before = (lax.broadcasted_iota(I32, (tm, tm), 0) > lax.broadcasted_iota(I32, (tm, tm), 1))
    earlier = jnp.dot(before.astype(BF16), onehot.astype(BF16), preferred_element_type=F32)
    rank = jnp.sum(jnp.where(onehot, earlier + run_ref[...], 0.0), axis=-1, keepdims=True)
    run_ref[...] += jnp.sum(onehot.astype(F32), axis=0, keepdims=True)
    cnt_ref[...] = run_ref[...]

    cls_ref[0] = _column_to_lanes(cls_f).astype(I32)
    rank_ref[0] = _column_to_lanes(rank).astype(I32)

    row_ref[:, :half] = _pack_bf16_pair(t[:, :half], t[:, half:])
    weights = jnp.where(lane == 0, w_lo, jnp.where(lane == 1, w_hi, 0.0))
    row_ref[:, half:] = _f32_bits(weights)


def _post_call(x2, o2, wo, gain, wr, *, tm):
    T, D = x2.shape
    row_w = D // 2 + ROW_WORDS_PAD
    return pl.pallas_call(
        _post_kernel,
        grid=(T // tm,),
        in_specs=[
            pl.BlockSpec((tm, D), lambda i: (i, 0)),
            pl.BlockSpec((tm, D), lambda i: (i, 0)),
            pl.BlockSpec((D, D), lambda i: (0, 0)),
            pl.BlockSpec((1, D), lambda i: (0, 0)),
            pl.BlockSpec((D, 2 * LANES), lambda i: (0, 0)),
        ],
        out_specs=[
            pl.BlockSpec((tm, D), lambda i: (i, 0)),
            pl.BlockSpec((tm, row_w), lambda i: (i, 0)),
            pl.BlockSpec((1, tm // LANES, LANES), lambda i: (i, 0, 0)),
            pl.BlockSpec((1, tm // LANES, LANES), lambda i: (i, 0, 0)),
            pl.BlockSpec((1, LANES), lambda i: (0, 0)),
        ],
        out_shape=[
            jax.ShapeDtypeStruct((T, D), F32),
            jax.ShapeDtypeStruct((T, row_w), U32),
            jax.ShapeDtypeStruct((T // tm, tm // LANES, LANES), I32),
            jax.ShapeDtypeStruct((T // tm, tm // LANES, LANES), I32),
            jax.ShapeDtypeStruct((1, LANES), F32),
        ],
        scratch_shapes=[pltpu.VMEM((1, LANES), F32)],
        compiler_params=_params(("arbitrary",)),
        name="post_attn_router",
    )(x2, o2, wo, gain, wr)


def _pos_kernel(start_ref, cls_ref, rank_ref, pos_ref):
    cls = cls_ref[...]
    base = jnp.zeros(cls.shape, I32)
    for c in range(N_CLASSES):
        base = jnp.where(cls == c, start_ref[c], base)
    pos_ref[...] = base + rank_ref[...]


def _pos_call(row_start, cls, rank):
    return pl.pallas_call(
        _pos_kernel,
        in_specs=[
            pl.BlockSpec(memory_space=pltpu.SMEM),
            pl.BlockSpec(memory_space=pltpu.VMEM),
            pl.BlockSpec(memory_space=pltpu.VMEM),
        ],
        out_specs=pl.BlockSpec(memory_space=pltpu.VMEM),
        out_shape=jax.ShapeDtypeStruct(cls.shape, I32),
        name="moe_positions",
    )(row_start, cls, rank)


ROW_DMA_UNROLL = 8


def _row_copy(src_ref, src_row, dst_ref, dst_row, sem):
    return pltpu.make_async_copy(src_ref.at[pl.ds(src_row, 1)], dst_ref.at[pl.ds(dst_row, 1)], sem)


def _issue_row_copies(n_rows, make_copy):
    def group(g, carry):
        for u in range(ROW_DMA_UNROLL):
            make_copy(g * ROW_DMA_UNROLL + u).start(priority=u % 2)
        return carry

    lax.fori_loop(0, n_rows // ROW_DMA_UNROLL, group, 0)


def _dispatch_kernel(pos_ref, row_ref, init_ref, out_ref, sem):
    del init_ref
    ts = row_ref.shape[0]
    _issue_row_copies(ts, lambda r: _row_copy(row_ref, r, out_ref, pos_ref[0, 0, r], sem))
    pltpu.make_async_copy(row_ref, out_ref.at[pl.ds(0, ts)], sem).wait()


def _dispatch_call(pos3, rows, init, *, ts):
    T, row_w = rows.shape
    return pl.pallas_call(
        _dispatch_kernel,
        grid=(T // ts,),
        in_specs=[
            pl.BlockSpec((1, 1, ts), lambda i: (i, 0, 0), memory_space=pltpu.SMEM),
            pl.BlockSpec((ts, row_w), lambda i: (i, 0)),
            pl.BlockSpec(memory_space=pl.ANY),
        ],
        out_specs=pl.BlockSpec(memory_space=pl.ANY),
        out_shape=jax.ShapeDtypeStruct(init.shape, init.dtype),
        scratch_shapes=[pltpu.SemaphoreType.DMA(())],
        input_output_aliases={2: 0},
        compiler_params=_params(("arbitrary",)),
        name="moe_dispatch",
    )(pos3, rows, init)


def _moe_kernel(ea_ref, eb_ref, used_ref, row_ref, w1a_ref, w3a_ref, w2a_ref,
                w1b_ref, w3b_ref, w2b_ref, y_ref):
    del ea_ref, eb_ref
    i = pl.program_id(0)
    half = row_ref.shape[1] - ROW_WORDS_PAD

    @pl.when(i < used_ref[0])
    def _():
        lo, hi = _unpack_bf16_pair(row_ref[:, :half])
        x = jnp.concatenate([lo, hi], axis=-1).astype(BF16)
        wts = lax.bitcast_convert_type(row_ref[:, half:], F32)
        w_lo = wts[:, 0:1]
        w_hi = wts[:, 1:2]

        def expert(w1_ref, w3_ref, w2_ref):
            z = jnp.dot(x, w1_ref[0], preferred_element_type=F32)
            u = jnp.dot(x, w3_ref[0], preferred_element_type=F32)
            h = (z * (1.0 / (1.0 + jnp.exp(-z)))) * u
            return jnp.dot(h.astype(BF16), w2_ref[0], preferred_element_type=F32)

        y_ref[...] = (w_lo * expert(w1a_ref, w3a_ref, w2a_ref)
                      + w_hi * expert(w1b_ref, w3b_ref, w2b_ref))

    @pl.when(i >= used_ref[0])
    def _():
        y_ref[...] = jnp.zeros_like(y_ref)


def _moe_call(tile_ea, tile_eb, n_used, rows, w1, w3, w2, *, tmx):
    rows_pad, row_w = rows.shape
    _, D, F = w1.shape
    wa = lambda i, ea, eb, used: (ea[i], 0, 0)
    wb = lambda i, ea, eb, used: (eb[i], 0, 0)
    grid_spec = pltpu.PrefetchScalarGridSpec(
        num_scalar_prefetch=3,
        grid=(rows_pad // tmx,),
        in_specs=[
            pl.BlockSpec((tmx, row_w), lambda i, ea, eb, used: (i, 0)),
            pl.BlockSpec((1, D, F), wa),
            pl.BlockSpec((1, D, F), wa),
            pl.BlockSpec((1, F, D), wa),
            pl.BlockSpec((1, D, F), wb),
            pl.BlockSpec((1, D, F), wb),
            pl.BlockSpec((1, F, D), wb),
        ],
        out_specs=pl.BlockSpec((tmx, D), lambda i, ea, eb, used: (i, 0)),
    )
    return pl.pallas_call(
        _moe_kernel,
        grid_spec=grid_spec,
        out_shape=jax.ShapeDtypeStruct((rows_pad, D), F32),
        compiler_params=_params(("arbitrary",)),
        name="moe_experts",
    )(tile_ea, tile_eb, n_used, rows, w1, w3, w2, w1, w3, w2)


def _combine_kernel(pos_ref, pos_next_ref, x_ref, y_ref, out_ref, buf_ref, sem):
    i = pl.program_id(0)
    n = pl.num_programs(0)
    tc = x_ref.shape[0]
    slot = i % 2

    def gather(p_ref, s):
        _issue_row_copies(
            tc, lambda r: _row_copy(y_ref, p_ref[0, 0, r], buf_ref.at[s], r, sem.at[s]))

    @pl.when(i == 0)
    def _():
        gather(pos_ref, 0)

    @pl.when(i + 1 < n)
    def _():
        gather(pos_next_ref, 1 - slot)

    pltpu.make_async_copy(y_ref.at[pl.ds(0, tc)], buf_ref.at[slot], sem.at[slot]).wait()
    out_ref[...] = x_ref[...] + buf_ref[slot]


def _combine_call(pos3, xn, y, *, tc):
    T, D = xn.shape
    n = T // tc
    return pl.pallas_call(
        _combine_kernel,
        grid=(n,),
        in_specs=[
            pl.BlockSpec((1, 1, tc), lambda i: (i, 0, 0), memory_space=pltpu.SMEM),
            pl.BlockSpec((1, 1, tc), lambda i: (jnp.minimum(i + 1, n - 1), 0, 0), memory_space=pltpu.SMEM),
            pl.BlockSpec((tc, D), lambda i: (i, 0)),
            pl.BlockSpec(memory_space=pl.ANY),
        ],
        out_specs=pl.BlockSpec((tc, D), lambda i: (i, 0)),
        out_shape=jax.ShapeDtypeStruct((T, D), F32),
        scratch_shapes=[pltpu.VMEM((2, tc, D), F32), pltpu.SemaphoreType.DMA((2,))],
        compiler_params=_params(("arbitrary",)),
        name="moe_combine",
    )(pos3, pos3, xn, y)


def _rope_tables(seq):
    half = ROPE_AXIS_DIM // 2
    freqs = ROPE_THETA ** (-(jnp.arange(half, dtype=F32) * 2.0 / ROPE_AXIS_DIM))
    t = jnp.arange(seq)
    ang_row = (t // GRID_W).astype(F32)[:, None] * freqs[None, :]
    ang_col = (t % GRID_W).astype(F32)[:, None] * freqs[None, :]

    def axis_tables(ang):
        c, s = jnp.cos(ang), jnp.sin(ang)
        return jnp.concatenate([c, c], axis=-1), jnp.concatenate([-s, s], axis=-1)

    cr, sr = axis_tables(ang_row)
    cc, sc = axis_tables(ang_col)
    cos = jnp.concatenate([cr, cc], axis=-1)
    sin = jnp.concatenate([sr, sc], axis=-1)
    reps = LANES // HEAD_DIM
    return jnp.tile(cos, (1, reps)), jnp.tile(sin, (1, reps))


def _t5_bucket_table():
    import math
    rel = jnp.arange(3 * Q_BLOCK)[None, :] - Q_BLOCK - jnp.arange(Q_BLOCK)[:, None]
    nb = N_REL_BUCKETS // 2
    max_exact = nb // 2
    ret = jnp.where(rel > 0, nb, 0)
    n = jnp.abs(rel)
    large = max_exact + (jnp.log(jnp.maximum(n, 1).astype(F32) / max_exact)
                         / math.log(REL_MAX_DIST / max_exact) * (nb - max_exact)).astype(I32)
    large = jnp.minimum(large, nb - 1)
    return (ret + jnp.where(n < max_exact, n, large)).astype(I32)


def _tile_tables(counts, tmx, n_tiles):
    tiles_per_class = (counts + tmx - 1) // tmx
    tile_end = jnp.cumsum(tiles_per_class)
    row_start = ((tile_end - tiles_per_class) * tmx).astype(I32)
    n_used = tile_end[-1].astype(I32)
    tile_id = jnp.minimum(jnp.arange(n_tiles, dtype=I32), jnp.maximum(n_used - 1, 0))
    tile_cls = jnp.minimum(jnp.sum(tile_id[:, None] >= tile_end[None, :], axis=1), N_CLASSES - 1).astype(I32)
    group = tile_cls // PAIRS_PER_GROUP
    pair = tile_cls % PAIRS_PER_GROUP
    tile_ea = group * EXPERTS_PER_GROUP + jnp.asarray(PAIR_LO, I32)[pair]
    tile_eb = group * EXPERTS_PER_GROUP + jnp.asarray(PAIR_HI, I32)[pair]
    return row_start, tile_ea.astype(I32), tile_eb.astype(I32), n_used.reshape(1)


def _tile(n, pref):
    t = min(n, pref)
    assert n % t == 0, (n, pref)
    return t


def kernel(x, ln_mix, w_qkv, q_norm, k_norm, w_o, rel_bias, sinks, ln_ffn, w_group, w_expert, w1, w3, w2):
    B, S, D = x.shape
    depth = w_qkv.shape[0]
    T = B * S
    assert S % GRID_W == 0 and S % Q_BLOCK == 0 and D == N_HEADS * HEAD_DIM

    tm = _tile(S, 512)
    tq_flash = _tile(S, 256)
    tk_flash = _tile(S, 512)
    tq_win = _tile(S, 1024)
    ts = _tile(T, 256)
    tmx = _tile(T, 256)
    n_tiles = T // tmx + N_CLASSES
    rows_pad = n_tiles * tmx
    row_w = D // 2 + ROW_WORDS_PAD

    cos, sin = _rope_tables(S)
    bias = _bias_call(rel_bias.astype(F32), _t5_bucket_table())

    w_qkv_b = w_qkv.astype(BF16)
    w_o_b = w_o.astype(BF16)
    w1_b, w3_b, w2_b = w1.astype(BF16), w3.astype(BF16), w2.astype(BF16)
    reps = MXU_DIM // HEAD_DIM
    router_f32 = jnp.concatenate(
        [w_group, w_expert,
         jnp.zeros((depth, D, LANES - N_GROUPS - N_EXPERTS), F32)], axis=-1).astype(F32)
    router_hi = router_f32.astype(BF16)
    router_lo = (router_f32 - router_hi.astype(F32)).astype(BF16)
    router = jnp.concatenate([router_hi, router_lo], axis=-1)

    x2 = x.reshape(T, D).astype(F32)
    sorted_rows = jnp.zeros((rows_pad, row_w), U32)
    for i in range(depth):
        q, k, v = _qkv_call(
            x2, ln_mix[i].reshape(1, D).astype(F32), w_qkv_b[i],
            jnp.tile(q_norm[i].astype(F32), reps).reshape(1, MXU_DIM),
            jnp.tile(k_norm[i].astype(F32), reps).reshape(1, MXU_DIM),
            cos, sin, batch=B, seq=S, rope=(i % N_MIXERS == 0), transposed=(i % N_MIXERS == 0), tm=tm)
        if i % N_MIXERS == 0:
            o = _flash_call(q, k, v, tq=tq_flash, tk=tk_flash)
        else:
            o = _window_call(sinks[i // N_MIXERS].astype(F32), q, k, v, bias, tq=tq_win)
        xn, rows, cls, rank, counts = _post_call(
            x2, o.reshape(T, D), w_o_b[i], ln_ffn[i].reshape(1, D).astype(F32), router[i], tm=tm)

        row_start, tile_ea, tile_eb, n_used = _tile_tables(
            counts[0, :N_CLASSES].astype(I32), tmx, n_tiles)
        pos = _pos_call(row_start, cls.reshape(T // LANES, LANES), rank.reshape(T // LANES, LANES))
        pos3 = pos.reshape(T // ts, 1, ts)
        sorted_rows = _dispatch_call(pos3, rows, sorted_rows, ts=ts)
        y = _moe_call(tile_ea, tile_eb, n_used, sorted_rows, w1_b[i], w3_b[i], w2_b[i], tmx=tmx)
        x2 = _combine_call(pos3, xn, y, tc=ts)
    return x2.reshape(B, S, D).astype(x.dtype)
```

```python
import functools

import jax
import jax.numpy as jnp
from jax import lax
from jax.experimental import pallas as pl
from jax.experimental.pallas import tpu as pltpu

N_HEADS = 16
N_KV_HEADS = 4
HEAD_DIM = 64
KV_GROUP = N_HEADS // N_KV_HEADS
SCALE = HEAD_DIM ** -0.5
GRID_W = 64
ROPE_THETA = 10000.0
ROPE_AXIS_DIM = HEAD_DIM // 2
Q_BLOCK = 128
WINDOW = 128
N_MIXERS = 2
N_REL_BUCKETS = 32
REL_MAX_DIST = 128
N_GROUPS = 4
EXPERTS_PER_GROUP = 4
N_EXPERTS = N_GROUPS * EXPERTS_PER_GROUP
NORM_EPS = 1e-6
NEG_INF = -1e30
LOG2_E = 1.4426950408889634
V_ROWS = HEAD_DIM + 16

PAIRS_PER_GROUP = 6
N_CLASSES = N_GROUPS * PAIRS_PER_GROUP
PAIR_LO = (0, 0, 0, 1, 1, 2)
PAIR_HI = (1, 2, 3, 2, 3, 3)

LANES = 128
SUBLANES = 8
MXU_DIM = 256
VMEM_LIMIT = 48 * 1024 * 1024

F32 = jnp.float32
BF16 = jnp.bfloat16
U32 = jnp.uint32
I32 = jnp.int32


def _params(sem, flags=None):
    return pltpu.CompilerParams(dimension_semantics=sem, vmem_limit_bytes=VMEM_LIMIT, flags=flags)


def _segment_sum_matrix():
    r = lax.broadcasted_iota(I32, (MXU_DIM, MXU_DIM), 0) // HEAD_DIM
    c = lax.broadcasted_iota(I32, (MXU_DIM, MXU_DIM), 1) // HEAD_DIM
    return (r == c).astype(BF16)


def _head_rmsnorm(t, gain, seg):
    outs = []
    for c in range(t.shape[1] // MXU_DIM):
        tc = t[:, c * MXU_DIM:(c + 1) * MXU_DIM]
        sq = tc * tc
        hi = sq.astype(BF16)
        lo = (sq - hi.astype(F32)).astype(BF16)
        ss = (jnp.dot(hi, seg, preferred_element_type=F32)
              + jnp.dot(lo, seg, preferred_element_type=F32))
        outs.append(tc * lax.rsqrt(ss * (1.0 / HEAD_DIM) + NORM_EPS) * gain)
    return outs


def _rope(chunks, cos, sin):
    lane = lax.broadcasted_iota(I32, cos.shape, 1)
    upper = (lane & (ROPE_AXIS_DIM // 2)) != 0
    outs = []
    for tc in chunks:
        halves = []
        for j in range(MXU_DIM // LANES):
            xc = tc[:, j * LANES:(j + 1) * LANES]
            partner = jnp.where(upper,
                                pltpu.roll(xc, ROPE_AXIS_DIM // 2, 1),
                                pltpu.roll(xc, LANES - ROPE_AXIS_DIM // 2, 1))
            halves.append(xc * cos + partner * sin)
        outs.append(jnp.concatenate(halves, axis=-1))
    return outs


def _qkv_kernel(x_ref, g_ref, w_ref, qg_ref, kg_ref, cos_ref, sin_ref,
                q_ref, k_ref, v_ref, *, rope, transposed):
    x = x_ref[...]
    ms = jnp.mean(x * x, axis=-1, keepdims=True)
    h = (x * lax.rsqrt(ms + NORM_EPS) * g_ref[...]).astype(BF16)
    qkv = jnp.dot(h, w_ref[...], preferred_element_type=F32)
    nq = N_HEADS * HEAD_DIM
    nk = N_KV_HEADS * HEAD_DIM
    seg = _segment_sum_matrix()
    qs = _head_rmsnorm(qkv[:, :nq], qg_ref[...], seg)
    ks = _head_rmsnorm(qkv[:, nq:nq + nk], kg_ref[...], seg)
    if rope:
        cos = cos_ref[...]
        sin = sin_ref[...]
        qs = _rope(qs, cos, sin)
        ks = _rope(ks, cos, sin)
    heads_per_chunk = MXU_DIM // HEAD_DIM
    v = qkv[:, nq + nk:]
    for c, kc in enumerate(ks):
        for j in range(heads_per_chunk):
            k_ref[0, c * heads_per_chunk + j] = kc[:, j * HEAD_DIM:(j + 1) * HEAD_DIM].astype(BF16)
    if transposed:
        for c, qc in enumerate(qs):
            qt = (qc * (SCALE * LOG2_E)).T
            for j in range(heads_per_chunk):
                q_ref[0, c * heads_per_chunk + j] = qt[j * HEAD_DIM:(j + 1) * HEAD_DIM].astype(BF16)
        vt = v.T
        ones = jnp.ones((V_ROWS - HEAD_DIM, vt.shape[1]), F32)
        for j in range(N_KV_HEADS):
            v_ref[0, j] = jnp.concatenate([vt[j * HEAD_DIM:(j + 1) * HEAD_DIM], ones], axis=0).astype(BF16)
    else:
        for c, qc in enumerate(qs):
            qc = qc * SCALE
            for j in range(heads_per_chunk):
                q_ref[0, c * heads_per_chunk + j] = qc[:, j * HEAD_DIM:(j + 1) * HEAD_DIM].astype(BF16)
        ones = jnp.ones((v.shape[0], LANES - HEAD_DIM), F32)
        for j in range(N_KV_HEADS):
            v_ref[0, j] = jnp.concatenate(
                [v[:, j * HEAD_DIM:(j + 1) * HEAD_DIM], ones], axis=-1).astype(BF16)


def _qkv_call(x2, gain, w, qg, kg, cos, sin, *, batch, seq, rope, transposed, tm):
    T, D = x2.shape
    spt = seq // tm
    qkv_dim = w.shape[1]
    kern = functools.partial(_qkv_kernel, rope=rope, transposed=transposed)
    head_map = lambda i: (i // spt, 0, i % spt, 0)
    feat_map = lambda i: (i // spt, 0, 0, i % spt)
    if transposed:
        q_spec = pl.BlockSpec((1, N_HEADS, HEAD_DIM, tm), feat_map)
        v_spec = pl.BlockSpec((1, N_KV_HEADS, V_ROWS, tm), feat_map)
        q_shape = (batch, N_HEADS, HEAD_DIM, seq)
        v_shape = (batch, N_KV_HEADS, V_ROWS, seq)
    else:
        q_spec = pl.BlockSpec((1, N_HEADS, tm, HEAD_DIM), head_map)
        v_spec = pl.BlockSpec((1, N_KV_HEADS, tm, LANES), head_map)
        q_shape = (batch, N_HEADS, seq, HEAD_DIM)
        v_shape = (batch, N_KV_HEADS, seq, LANES)
    return pl.pallas_call(
        kern,
        grid=(T // tm,),
        in_specs=[
            pl.BlockSpec((tm, D), lambda i: (i, 0)),
            pl.BlockSpec((1, D), lambda i: (0, 0)),
            pl.BlockSpec((D, qkv_dim), lambda i: (0, 0)),
            pl.BlockSpec((1, MXU_DIM), lambda i: (0, 0)),
            pl.BlockSpec((1, MXU_DIM), lambda i: (0, 0)),
            pl.BlockSpec((tm, LANES), lambda i: (i % spt, 0)),
            pl.BlockSpec((tm, LANES), lambda i: (i % spt, 0)),
        ],
        out_specs=[
            q_spec,
            pl.BlockSpec((1, N_KV_HEADS, tm, HEAD_DIM), head_map),
            v_spec,
        ],
        out_shape=[
            jax.ShapeDtypeStruct(q_shape, BF16),
            jax.ShapeDtypeStruct((batch, N_KV_HEADS, seq, HEAD_DIM), BF16),
            jax.ShapeDtypeStruct(v_shape, BF16),
        ],
        compiler_params=_params(("parallel",)),
        name="qkv_proj",
    )(x2, gain, w, qg, kg, cos, sin)


FLASH_RING = 4


def _flash_kernel(qt_ref, k_ref, vt_ref, o_ref, acc_ref, *s_refs, tk):
    ring = len(s_refs)
    tq = qt_ref.shape[3]
    seq = k_ref.shape[2]
    n_chunks = seq // tk
    key_tiles = tk // MXU_DIM
    heads = range(KV_GROUP)

    def tile(g, kt):
        return (slice(kt * MXU_DIM, (kt + 1) * MXU_DIM), slice(g * tq, (g + 1) * tq))

    def qk_head(chunk, g, dst_ref):
        mx = None
        for kt in range(key_tiles):
            off = pl.multiple_of(chunk * tk + kt * MXU_DIM, MXU_DIM)
            s = jnp.dot(k_ref[0, 0, pl.ds(off, MXU_DIM), :], qt_ref[0, g],
                        preferred_element_type=F32)
            dst_ref[tile(g, kt)] = s
            t = jnp.max(s, axis=0, keepdims=True)
            mx = t if mx is None else jnp.maximum(mx, t)
        return mx

    def attend_head(chunk, g, src_ref, mx, m):
        m_new = jnp.maximum(m, mx)
        pv = None
        for kt in range(key_tiles):
            off = pl.multiple_of(chunk * tk + kt * MXU_DIM, MXU_DIM)
            p = jnp.exp2(src_ref[tile(g, kt)] - m_new).astype(BF16)
            d = jnp.dot(vt_ref[0, 0, :, pl.ds(off, MXU_DIM)], p, preferred_element_type=F32)
            pv = d if pv is None else pv + d
        cols = slice(g * tq, (g + 1) * tq)
        acc_ref[:, cols] = jnp.exp2(m - m_new) * acc_ref[:, cols] + pv
        return m_new

    def step(chunk, u, mx, m, last=False):
        mx_next, m_out = [], []
        for g in heads:
            if not last:
                mx_next.append(qk_head(chunk + 1, g, s_refs[(u + 1) % ring]))
            m_out.append(attend_head(chunk, g, s_refs[u], mx[g], m[g]))
        return tuple(mx_next), tuple(m_out)

    def body(j, carry):
        mx, m = carry
        for u in range(ring):
            mx, m = step(ring * j + u, u, mx, m)
        return mx, m

    acc_ref[...] = jnp.zeros_like(acc_ref)
    m = tuple(jnp.full((1, tq), NEG_INF, F32) for _ in heads)
    mx = tuple(qk_head(0, g, s_refs[0]) for g in heads)
    loops = (n_chunks - 1) // ring
    mx, m = lax.fori_loop(0, loops, body, (mx, m))
    for c in range(loops * ring, n_chunks - 1):
        mx, m = step(c, c % ring, mx, m)
    step(n_chunks - 1, (n_chunks - 1) % ring, mx, m, last=True)
    acc = acc_ref[...]
    o = acc[:HEAD_DIM] / acc[HEAD_DIM:HEAD_DIM + 1]
    pairs = []
    for g in range(0, KV_GROUP, 2):
        two_heads = jnp.concatenate([o[:, g * tq:(g + 1) * tq], o[:, (g + 1) * tq:(g + 2) * tq]], axis=0)
        pairs.append(two_heads.T)
    o_ref[0] = jnp.concatenate(pairs, axis=-1).astype(BF16)


def _flash_call(q, k, v, *, tq, tk):
    B, _, _, S = q.shape
    kern = functools.partial(_flash_kernel, tk=tk)
    return pl.pallas_call(
        kern,
        grid=(B, N_KV_HEADS, S // tq),
        in_specs=[
            pl.BlockSpec((1, KV_GROUP, HEAD_DIM, tq), lambda b, h, i: (b, h, 0, i)),
            pl.BlockSpec((1, 1, S, HEAD_DIM), lambda b, h, i: (b, h, 0, 0)),
            pl.BlockSpec((1, 1, V_ROWS, S), lambda b, h, i: (b, h, 0, 0)),
        ],
        out_specs=pl.BlockSpec((1, tq, KV_GROUP * HEAD_DIM), lambda b, h, i: (b, i, h)),
        out_shape=jax.ShapeDtypeStruct((B, S, N_HEADS * HEAD_DIM), BF16),
        scratch_shapes=([pltpu.VMEM((V_ROWS, KV_GROUP * tq), F32)]
                        + [pltpu.VMEM((tk, KV_GROUP * tq), F32)] * FLASH_RING),
        compiler_params=_params(("parallel", "parallel", "parallel")),
        name="flash_global",
    )(q, k, v)


EDGE_VARIANTS = 4
WINDOW_UNROLL = 8


def _merge_heads(o, rows):
    return jnp.concatenate([o[g * rows:(g + 1) * rows] for g in range(KV_GROUP)], axis=-1)


def _bias_kernel(rb_ref, bucket_ref, out_ref):
    h = pl.program_id(0)
    b = bucket_ref[...]
    bias = jnp.zeros(b.shape, F32)
    for j in range(N_REL_BUCKETS):
        bias = jnp.where(b == j, rb_ref[h, j], bias)
    qpos = lax.broadcasted_iota(I32, b.shape, 0)
    col = lax.broadcasted_iota(I32, b.shape, 1)
    in_band = jnp.abs(col - Q_BLOCK - qpos) <= WINDOW
    for e in range(EDGE_VARIANTS):
        valid = in_band
        if e & 1:
            valid = valid & (col >= Q_BLOCK)
        if e & 2:
            valid = valid & (col < 2 * Q_BLOCK)
        out_ref[e, 0] = jnp.where(valid, bias, NEG_INF)


def _bias_call(rel_bias, bucket):
    return pl.pallas_call(
        _bias_kernel,
        grid=(N_HEADS,),
        in_specs=[
            pl.BlockSpec(memory_space=pltpu.SMEM),
            pl.BlockSpec(bucket.shape, lambda h: (0, 0)),
        ],
        out_specs=pl.BlockSpec((EDGE_VARIANTS, 1) + bucket.shape, lambda h: (0, h, 0, 0)),
        out_shape=jax.ShapeDtypeStruct((EDGE_VARIANTS, N_HEADS) + bucket.shape, F32),
        compiler_params=_params(("parallel",)),
        name="rel_bias_table",
    )(rel_bias, bucket)


def _window_kernel(sink_ref, q_ref, k_ref, v_ref, bias_ref, o_ref, *, blocks_per_tile):
    hk = pl.program_id(1)
    tile = pl.program_id(2)
    nb = k_ref.shape[2] // Q_BLOCK
    rows = KV_GROUP * Q_BLOCK
    band_w = 3 * Q_BLOCK

    head = lax.broadcasted_iota(I32, (rows, 1), 0) // Q_BLOCK
    sink = jnp.zeros((rows, 1), F32)
    for g in range(KV_GROUP):
        sink = jnp.where(head == g, sink_ref[hk * KV_GROUP + g], sink)

    def body(j, carry):
        n = tile * blocks_per_tile + j
        off = pl.multiple_of(j * Q_BLOCK, Q_BLOCK)
        qb = q_ref[0, :, pl.ds(off, Q_BLOCK), :].reshape(rows, HEAD_DIM)
        starts = [pl.multiple_of(jnp.maximum(n - 1, 0) * Q_BLOCK, Q_BLOCK),
                  pl.multiple_of(n * Q_BLOCK, Q_BLOCK),
                  pl.multiple_of(jnp.minimum(n + 1, nb - 1) * Q_BLOCK, Q_BLOCK)]
        kcat = jnp.concatenate([k_ref[0, 0, pl.ds(s, Q_BLOCK), :] for s in starts], axis=0)
        vcat = jnp.concatenate([v_ref[0, 0, pl.ds(s, Q_BLOCK), :] for s in starts], axis=0)
        edge = (n == 0).astype(I32) + 2 * (n == nb - 1).astype(I32)
        s = (lax.dot_general(qb, kcat, (((1,), (1,)), ((), ())), preferred_element_type=F32)
             + bias_ref[edge].reshape(rows, band_w))
        m = jnp.maximum(jnp.max(s, axis=-1, keepdims=True), sink)
        p = jnp.exp(s - m).astype(BF16)
        ov = jnp.dot(p, vcat, preferred_element_type=F32)
        denom = ov[:, HEAD_DIM:HEAD_DIM + 1] + jnp.exp(sink - m)
        o = ov[:, :HEAD_DIM] / denom
        o_ref[0, pl.ds(off, Q_BLOCK), :] = _merge_heads(o, Q_BLOCK).astype(BF16)
        return carry

    lax.fori_loop(0, blocks_per_tile, body, 0, unroll=WINDOW_UNROLL)


def _window_call(sink, q, k, v, bias, *, tq):
    B, _, S, _ = q.shape
    kern = functools.partial(_window_kernel, blocks_per_tile=tq // Q_BLOCK)
    return pl.pallas_call(
        kern,
        grid=(B, N_KV_HEADS, S // tq),
        in_specs=[
            pl.BlockSpec(memory_space=pltpu.SMEM),
            pl.BlockSpec((1, KV_GROUP, tq, HEAD_DIM), lambda b, h, i: (b, h, i, 0)),
            pl.BlockSpec((1, 1, S, HEAD_DIM), lambda b, h, i: (b, h, 0, 0)),
            pl.BlockSpec((1, 1, S, LANES), lambda b, h, i: (b, h, 0, 0)),
            pl.BlockSpec((EDGE_VARIANTS, KV_GROUP, Q_BLOCK, 3 * Q_BLOCK), lambda b, h, i: (0, h, 0, 0)),
        ],
        out_specs=pl.BlockSpec((1, tq, KV_GROUP * HEAD_DIM), lambda b, h, i: (b, i, h)),
        out_shape=jax.ShapeDtypeStruct((B, S, N_HEADS * HEAD_DIM), BF16),
        compiler_params=_params(("parallel", "parallel", "parallel")),
        name="window_attn",
    )(sink, q, k, v, bias)


ROW_DMA_UNROLL = 8


def _subl(c, n_rows):
    return pl.ds(c, n_rows, stride=SUBLANES)


def _row_tile(ref, row):
    return ref.at[pl.ds(pl.multiple_of(row * SUBLANES, SUBLANES), SUBLANES)]


def _issue_row_copies(n_rows, make_copy):
    def group(g, carry):
        for u in range(ROW_DMA_UNROLL):
            make_copy(g * ROW_DMA_UNROLL + u).start(priority=u % 2)
        return carry

    lax.fori_loop(0, n_rows // ROW_DMA_UNROLL, group, 0)


def _f32_bits(x):
    return lax.bitcast_convert_type(x, U32)


def _pack_bf16_pair(lo_half, hi_half):
    lo = lax.shift_right_logical(_f32_bits(lo_half.astype(BF16).astype(F32)), jnp.uint32(16))
    hi = _f32_bits(hi_half.astype(BF16).astype(F32)) & jnp.uint32(0xFFFF0000)
    return lo | hi


def _unpack_bf16_pair(words):
    lo = lax.bitcast_convert_type(lax.shift_left(words, jnp.uint32(16)), F32)
    hi = lax.bitcast_convert_type(words & jnp.uint32(0xFFFF0000), F32)
    return lo, hi


def _column_to_lanes(col):
    tm = col.shape[0]
    wide = jnp.broadcast_to(col, (tm, LANES))
    return jnp.concatenate(
        [wide[b * LANES:(b + 1) * LANES].T[0:1] for b in range(tm // LANES)], axis=0)


def _post_kernel(x_ref, o_ref, wo_ref, g_ref, wr_ref,
                 xn_ref, row_ref, cls_ref, rank_ref, cnt_ref, run_ref):
    step = pl.program_id(0)
    tm = x_ref.shape[0]
    half = x_ref.shape[1] // 2

    @pl.when(step == 0)
    def _():
        run_ref[...] = jnp.zeros_like(run_ref)

    xn = x_ref[...] + jnp.dot(o_ref[...], wo_ref[...], preferred_element_type=F32)
    xn_ref[...] = xn
    ms = jnp.mean(xn * xn, axis=-1, keepdims=True)
    t = xn * lax.rsqrt(ms + NORM_EPS) * g_ref[...]
    t_hi = t.astype(BF16)
    t_lo = (t - t_hi.astype(F32)).astype(BF16)
    hi_both = jnp.dot(t_hi, wr_ref[...], preferred_element_type=F32)
    lo_hi = jnp.dot(t_lo, wr_ref[:, :LANES], preferred_element_type=F32)
    logits = hi_both[:, :LANES] + (hi_both[:, LANES:] + lo_hi)

    lane = lax.broadcasted_iota(I32, (tm, LANES), 1)
    lane_f = lane.astype(F32)
    big = jnp.float32(LANES)

    def first_argmax(vals):
        top = jnp.max(vals, axis=-1, keepdims=True)
        idx = jnp.min(jnp.where(vals == top, lane_f, big), axis=-1, keepdims=True)
        return top, idx

    is_group = lane < N_GROUPS
    g_top, g_idx = first_argmax(jnp.where(is_group, logits, -jnp.inf))
    g_sum = jnp.sum(jnp.where(is_group, jnp.exp(logits - g_top), 0.0), axis=-1, keepdims=True)
    g_prob = 1.0 / g_sum

    base = N_GROUPS + EXPERTS_PER_GROUP * g_idx
    in_group = (lane_f >= base) & (lane_f < base + EXPERTS_PER_GROUP)
    e_logits = jnp.where(in_group, logits, -jnp.inf)
    e1, i1 = first_argmax(e_logits)
    e2, i2 = first_argmax(jnp.where(lane_f == i1, -jnp.inf, e_logits))
    r = jnp.exp(e2 - e1)
    w_first = (1.0 / (1.0 + r)) * g_prob
    w_second = (r / (1.0 + r)) * g_prob

    j1 = i1 - base
    j2 = i2 - base
    first_is_lo = j1 < j2
    a = jnp.minimum(j1, j2)
    b = jnp.maximum(j1, j2)
    pair = a * 3.0 - a * (a - 1.0) * 0.5 + (b - a - 1.0)
    cls_f = g_idx * PAIRS_PER_GROUP + pair
    w_lo = jnp.where(first_is_lo, w_first, w_second)
    w_hi = jnp.where(first_is_lo, w_second, w_first)

    onehot = lane_f == cls_f
    before = (lax.broadcasted_iota(I32, (tm, tm), 0) > lax.broadcasted_iota(I32, (tm, tm), 1))
    earlier = jnp.dot(before.astype(BF16), onehot.astype(BF16), preferred_element_type=F32)
    rank = jnp.sum(jnp.where(onehot, earlier + run_ref[...], 0.0), axis=-1, keepdims=True)
    run_ref[...] += jnp.sum(onehot.astype(F32), axis=0, keepdims=True)
    cnt_ref[...] = run_ref[...]

    cls_ref[0] = _column_to_lanes(cls_f).astype(I32)
    rank_ref[0] = _column_to_lanes(rank).astype(I32)

    words = _pack_bf16_pair(t[:, :half], t[:, half:])
    x_subl = half // LANES
    for c in range(x_subl):
        row_ref[_subl(c, tm), :] = words[:, c * LANES:(c + 1) * LANES]
    weights = jnp.where(lane == 0, w_lo, jnp.where(lane == 1, w_hi, 0.0))
    row_ref[_subl(x_subl, tm), :] = _f32_bits(weights)
    for c in range(x_subl + 1, SUBLANES):
        row_ref[_subl(c, tm), :] = jnp.zeros((tm, LANES), U32)


def _post_call(x2, o2, wo, gain, wr, *, tm):
    T, D = x2.shape
    assert D // 2 + LANES <= SUBLANES * LANES
    return pl.pallas_call(
        _post_kernel,
        grid=(T // tm,),
        in_specs=[
            pl.BlockSpec((tm, D), lambda i: (i, 0)),
            pl.BlockSpec((tm, D), lambda i: (i, 0)),
            pl.BlockSpec((D, D), lambda i: (0, 0)),
            pl.BlockSpec((1, D), lambda i: (0, 0)),
            pl.BlockSpec((D, 2 * LANES), lambda i: (0, 0)),
        ],
        out_specs=[
            pl.BlockSpec((tm, D), lambda i: (i, 0)),
            pl.BlockSpec((tm * SUBLANES, LANES), lambda i: (i, 0)),
            pl.BlockSpec((1, tm // LANES, LANES), lambda i: (i, 0, 0)),
            pl.BlockSpec((1, tm // LANES, LANES), lambda i: (i, 0, 0)),
            pl.BlockSpec((1, LANES), lambda i: (0, 0)),
        ],
        out_shape=[
            jax.ShapeDtypeStruct((T, D), F32),
            jax.ShapeDtypeStruct((T * SUBLANES, LANES), U32),
            jax.ShapeDtypeStruct((T // tm, tm // LANES, LANES), I32),
            jax.ShapeDtypeStruct((T // tm, tm // LANES, LANES), I32),
            jax.ShapeDtypeStruct((1, LANES), F32),
        ],
        scratch_shapes=[pltpu.VMEM((1, LANES), F32)],
        compiler_params=_params(("arbitrary",)),
        name="post_attn_router",
    )(x2, o2, wo, gain, wr)


def _pos_kernel(start_ref, cls_ref, rank_ref, pos_ref):
    cls = cls_ref[...]
    base = jnp.zeros(cls.shape, I32)
    for c in range(N_CLASSES):
        base = jnp.where(cls == c, start_ref[c], base)
    pos_ref[...] = base + rank_ref[...]


def _pos_call(row_start, cls, rank):
    return pl.pallas_call(
        _pos_kernel,
        in_specs=[
            pl.BlockSpec(memory_space=pltpu.SMEM),
            pl.BlockSpec(memory_space=pltpu.VMEM),
            pl.BlockSpec(memory_space=pltpu.VMEM),
        ],
        out_specs=pl.BlockSpec(memory_space=pltpu.VMEM),
        out_shape=jax.ShapeDtypeStruct(cls.shape, I32),
        name="moe_positions",
    )(row_start, cls, rank)


def _dispatch_kernel(pos_ref, row_ref, init_ref, out_ref, sem):
    del init_ref
    ts = row_ref.shape[0] // SUBLANES
    _issue_row_copies(
        ts, lambda r: pltpu.make_async_copy(
            _row_tile(row_ref, r), _row_tile(out_ref, pos_ref[0, 0, r]), sem))
    pltpu.make_async_copy(row_ref, out_ref.at[pl.ds(0, ts * SUBLANES)], sem).wait()


def _dispatch_call(pos3, rows, init, *, ts):
    T = rows.shape[0] // SUBLANES
    return pl.pallas_call(
        _dispatch_kernel,
        grid=(T // ts,),
        in_specs=[
            pl.BlockSpec((1, 1, ts), lambda i: (i, 0, 0), memory_space=pltpu.SMEM),
            pl.BlockSpec((ts * SUBLANES, LANES), lambda i: (i, 0)),
            pl.BlockSpec(memory_space=pl.ANY),
        ],
        out_specs=pl.BlockSpec(memory_space=pl.ANY),
        out_shape=jax.ShapeDtypeStruct(init.shape, init.dtype),
        scratch_shapes=[pltpu.SemaphoreType.DMA(())],
        input_output_aliases={2: 0},
        compiler_params=_params(("arbitrary",)),
        name="moe_dispatch",
    )(pos3, rows, init)


def _moe_kernel(ea_ref, eb_ref, used_ref, row_ref, w1a_ref, w3a_ref, w2a_ref,
                w1b_ref, w3b_ref, w2b_ref, y_ref):
    del ea_ref, eb_ref
    i = pl.program_id(0)
    x_subl = w1a_ref.shape[2] // (2 * LANES)

    @pl.when(i < used_ref[0])
    def _():
        tmx = y_ref.shape[0] // SUBLANES
        words = jnp.concatenate([row_ref[_subl(c, tmx), :] for c in range(x_subl)], axis=-1)
        lo, hi = _unpack_bf16_pair(words)
        x = jnp.concatenate([lo, hi], axis=-1).astype(BF16)
        wts = lax.bitcast_convert_type(row_ref[_subl(x_subl, tmx), :], F32)
        w_lo = wts[:, 0:1]
        w_hi = wts[:, 1:2]

        def expert(w1_ref, w3_ref, w2_ref):
            z = jnp.dot(x, w1_ref[0, 0], preferred_element_type=F32)
            u = jnp.dot(x, w3_ref[0, 0], preferred_element_type=F32)
            h = (z * (1.0 / (1.0 + jnp.exp(-z)))) * u
            return jnp.dot(h.astype(BF16), w2_ref[0, 0], preferred_element_type=F32)

        y = (w_lo * expert(w1a_ref, w3a_ref, w2a_ref)
             + w_hi * expert(w1b_ref, w3b_ref, w2b_ref))
        for c in range(SUBLANES):
            y_ref[_subl(c, tmx), :] = y[:, c * LANES:(c + 1) * LANES]

    @pl.when(i >= used_ref[0])
    def _():
        y_ref[...] = jnp.zeros_like(y_ref)


def _moe_call(tile_ea, tile_eb, n_used, rows, w1, w3, w2, *, layer, tmx):
    rows_pad = rows.shape[0] // SUBLANES
    _, _, D, F = w1.shape
    assert D == SUBLANES * LANES
    wa = lambda i, ea, eb, used: (layer, ea[i], 0, 0)
    wb = lambda i, ea, eb, used: (layer, eb[i], 0, 0)
    grid_spec = pltpu.PrefetchScalarGridSpec(
        num_scalar_prefetch=3,
        grid=(rows_pad // tmx,),
        in_specs=[
            pl.BlockSpec((tmx * SUBLANES, LANES), lambda i, ea, eb, used: (i, 0)),
            pl.BlockSpec((1, 1, D, F), wa),
            pl.BlockSpec((1, 1, D, F), wa),
            pl.BlockSpec((1, 1, F, D), wa),
            pl.BlockSpec((1, 1, D, F), wb),
            pl.BlockSpec((1, 1, D, F), wb),
            pl.BlockSpec((1, 1, F, D), wb),
        ],
        out_specs=pl.BlockSpec((tmx * SUBLANES, LANES), lambda i, ea, eb, used: (i, 0)),
    )
    return pl.pallas_call(
        _moe_kernel,
        grid_spec=grid_spec,
        out_shape=jax.ShapeDtypeStruct((rows_pad * SUBLANES, LANES), F32),
        compiler_params=_params(("arbitrary",)),
        name="moe_experts",
    )(tile_ea, tile_eb, n_used, rows, w1, w3, w2, w1, w3, w2)


def _combine_kernel(pos_ref, pos_next_ref, x_ref, y_ref, out_ref, buf_ref, sem):
    i = pl.program_id(0)
    n = pl.num_programs(0)
    tc = x_ref.shape[0]
    slot = i % 2

    def gather(p_ref, s):
        _issue_row_copies(
            tc, lambda r: pltpu.make_async_copy(
                _row_tile(y_ref, p_ref[0, 0, r]), _row_tile(buf_ref.at[s], r), sem.at[s]))

    @pl.when(i == 0)
    def _():
        gather(pos_ref, 0)

    @pl.when(i + 1 < n)
    def _():
        gather(pos_next_ref, 1 - slot)

    pltpu.make_async_copy(y_ref.at[pl.ds(0, tc * SUBLANES)], buf_ref.at[slot], sem.at[slot]).wait()
    for c in range(SUBLANES):
        cols = slice(c * LANES, (c + 1) * LANES)
        out_ref[:, cols] = x_ref[:, cols] + buf_ref[slot, _subl(c, tc), :]


def _combine_call(pos3, xn, y, *, tc):
    T, D = xn.shape
    n = T // tc
    return pl.pallas_call(
        _combine_kernel,
        grid=(n,),
        in_specs=[
            pl.BlockSpec((1, 1, tc), lambda i: (i, 0, 0), memory_space=pltpu.SMEM),
            pl.BlockSpec((1, 1, tc), lambda i: (jnp.minimum(i + 1, n - 1), 0, 0), memory_space=pltpu.SMEM),
            pl.BlockSpec((tc, D), lambda i: (i, 0)),
            pl.BlockSpec(memory_space=pl.ANY),
        ],
        out_specs=pl.BlockSpec((tc, D), lambda i: (i, 0)),
        out_shape=jax.ShapeDtypeStruct((T, D), F32),
        scratch_shapes=[pltpu.VMEM((2, tc * SUBLANES, LANES), F32),
                        pltpu.SemaphoreType.DMA((2,))],
        compiler_params=_params(("arbitrary",)),
        name="moe_combine",
    )(pos3, pos3, xn, y)


def _rope_tables(seq):
    half = ROPE_AXIS_DIM // 2
    freqs = ROPE_THETA ** (-(jnp.arange(half, dtype=F32) * 2.0 / ROPE_AXIS_DIM))
    t = jnp.arange(seq)
    ang_row = (t // GRID_W).astype(F32)[:, None] * freqs[None, :]
    ang_col = (t % GRID_W).astype(F32)[:, None] * freqs[None, :]

    def axis_tables(ang):
        c, s = jnp.cos(ang), jnp.sin(ang)
        return jnp.concatenate([c, c], axis=-1), jnp.concatenate([-s, s], axis=-1)

    cr, sr = axis_tables(ang_row)
    cc, sc = axis_tables(ang_col)
    cos = jnp.concatenate([cr, cc], axis=-1)
    sin = jnp.concatenate([sr, sc], axis=-1)
    reps = LANES // HEAD_DIM
    return jnp.tile(cos, (1, reps)), jnp.tile(sin, (1, reps))


def _t5_bucket_table():
    import math
    rel = jnp.arange(3 * Q_BLOCK)[None, :] - Q_BLOCK - jnp.arange(Q_BLOCK)[:, None]
    nb = N_REL_BUCKETS // 2
    max_exact = nb // 2
    ret = jnp.where(rel > 0, nb, 0)
    n = jnp.abs(rel)
    large = max_exact + (jnp.log(jnp.maximum(n, 1).astype(F32) / max_exact)
                         / math.log(REL_MAX_DIST / max_exact) * (nb - max_exact)).astype(I32)
    large = jnp.minimum(large, nb - 1)
    return (ret + jnp.where(n < max_exact, n, large)).astype(I32)


def _tile_tables(counts, tmx, n_tiles):
    tiles_per_class = (counts + tmx - 1) // tmx
    tile_end = jnp.cumsum(tiles_per_class)
    row_start = ((tile_end - tiles_per_class) * tmx).astype(I32)
    n_used = tile_end[-1].astype(I32)
    tile_id = jnp.minimum(jnp.arange(n_tiles, dtype=I32), jnp.maximum(n_used - 1, 0))
    tile_cls = jnp.minimum(jnp.sum(tile_id[:, None] >= tile_end[None, :], axis=1), N_CLASSES - 1).astype(I32)
    group = tile_cls // PAIRS_PER_GROUP
    pair = tile_cls % PAIRS_PER_GROUP
    tile_ea = group * EXPERTS_PER_GROUP + jnp.asarray(PAIR_LO, I32)[pair]
    tile_eb = group * EXPERTS_PER_GROUP + jnp.asarray(PAIR_HI, I32)[pair]
    return row_start, tile_ea.astype(I32), tile_eb.astype(I32), n_used.reshape(1)


def _tile(n, pref):
    t = min(n, pref)
    assert n % t == 0, (n, pref)
    return t


def kernel(x, ln_mix, w_qkv, q_norm, k_norm, w_o, rel_bias, sinks, ln_ffn, w_group, w_expert, w1, w3, w2):
    B, S, D = x.shape
    depth = w_qkv.shape[0]
    T = B * S
    assert S % GRID_W == 0 and S % Q_BLOCK == 0 and D == N_HEADS * HEAD_DIM

    tm = _tile(S, 512)
    tq_flash = _tile(S, MXU_DIM)
    tk_flash = _tile(S, 512)
    tq_win = _tile(S, 1024)
    ts = _tile(T, 256)
    tmx = _tile(T, 256)
    n_tiles = T // tmx + N_CLASSES
    rows_pad = n_tiles * tmx

    cos, sin = _rope_tables(S)
    bias = _bias_call(rel_bias.astype(F32), _t5_bucket_table())

    w_qkv_b = w_qkv.astype(BF16)
    w_o_b = w_o.astype(BF16)
    w1_b, w3_b, w2_b = w1.astype(BF16), w3.astype(BF16), w2.astype(BF16)
    reps = MXU_DIM // HEAD_DIM
    router_f32 = jnp.concatenate(
        [w_group, w_expert,
         jnp.zeros((depth, D, LANES - N_GROUPS - N_EXPERTS), F32)], axis=-1).astype(F32)
    router_hi = router_f32.astype(BF16)
    router_lo = (router_f32 - router_hi.astype(F32)).astype(BF16)
    router = jnp.concatenate([router_hi, router_lo], axis=-1)

    x2 = x.reshape(T, D).astype(F32)
    sorted_rows = jnp.zeros((rows_pad * SUBLANES, LANES), U32)
    for i in range(depth):
        q, k, v = _qkv_call(
            x2, ln_mix[i].reshape(1, D).astype(F32), w_qkv_b[i],
            jnp.tile(q_norm[i].astype(F32), reps).reshape(1, MXU_DIM),
            jnp.tile(k_norm[i].astype(F32), reps).reshape(1, MXU_DIM),
            cos, sin, batch=B, seq=S, rope=(i % N_MIXERS == 0), transposed=(i % N_MIXERS == 0), tm=tm)
        if i % N_MIXERS == 0:
            o = _flash_call(q, k, v, tq=tq_flash, tk=tk_flash)
        else:
            o = _window_call(sinks[i // N_MIXERS].astype(F32), q, k, v, bias, tq=tq_win)
        xn, rows, cls, rank, counts = _post_call(
            x2, o.reshape(T, D), w_o_b[i], ln_ffn[i].reshape(1, D).astype(F32), router[i], tm=tm)

        row_start, tile_ea, tile_eb, n_used = _tile_tables(
            counts[0, :N_CLASSES].astype(I32), tmx, n_tiles)
        pos = _pos_call(row_start, cls.reshape(T // LANES, LANES), rank.reshape(T // LANES, LANES))
        pos3 = pos.reshape(T // ts, 1, ts)
        sorted_rows = _dispatch_call(pos3, rows, sorted_rows, ts=ts)
        y = _moe_call(tile_ea, tile_eb, n_used, sorted_rows, w1_b, w3_b, w2_b, layer=i, tmx=tmx)
        x2 = _combine_call(pos3, xn, y, tc=ts)
    return x2.reshape(B, S, D).astype(x.dtype)
```

```python
import functools

import jax
import jax.numpy as jnp
from jax import lax
from jax.experimental import pallas as pl
from jax.experimental.pallas import tpu as pltpu

N_HEADS = 16
N_KV_HEADS = 4
HEAD_DIM = 64
KV_GROUP = N_HEADS // N_KV_HEADS
SCALE = HEAD_DIM ** -0.5
GRID_W = 64
ROPE_THETA = 10000.0
ROPE_AXIS_DIM = HEAD_DIM // 2
Q_BLOCK = 128
WINDOW = 128
N_MIXERS = 2
N_REL_BUCKETS = 32
REL_MAX_DIST = 128
N_GROUPS = 4
EXPERTS_PER_GROUP = 4
N_EXPERTS = N_GROUPS * EXPERTS_PER_GROUP
NORM_EPS = 1e-6
NEG_INF = -1e30
LOG2_E = 1.4426950408889634
V_ROWS = HEAD_DIM + 16

PAIRS_PER_GROUP = 6
N_CLASSES = N_GROUPS * PAIRS_PER_GROUP
PAIR_LO = (0, 0, 0, 1, 1, 2)
PAIR_HI = (1, 2, 3, 2, 3, 3)

LANES = 128
SUBLANES = 8
MXU_DIM = 256
VMEM_LIMIT = 48 * 1024 * 1024

F32 = jnp.float32
BF16 = jnp.bfloat16
U32 = jnp.uint32
I32 = jnp.int32


def _params(sem, flags=None):
    return pltpu.CompilerParams(dimension_semantics=sem, vmem_limit_bytes=VMEM_LIMIT, flags=flags)


def _segment_sum_matrix():
    r = lax.broadcasted_iota(I32, (MXU_DIM, MXU_DIM), 0) // HEAD_DIM
    c = lax.broadcasted_iota(I32, (MXU_DIM, MXU_DIM), 1) // HEAD_DIM
    return (r == c).astype(BF16)


def _head_rmsnorm(t, gain, seg):
    outs = []
    for c in range(t.shape[1] // MXU_DIM):
        tc = t[:, c * MXU_DIM:(c + 1) * MXU_DIM]
        sq = tc * tc
        hi = sq.astype(BF16)
        lo = (sq - hi.astype(F32)).astype(BF16)
        ss = (jnp.dot(hi, seg, preferred_element_type=F32)
              + jnp.dot(lo, seg, preferred_element_type=F32))
        outs.append(tc * lax.rsqrt(ss * (1.0 / HEAD_DIM) + NORM_EPS) * gain)
    return outs


def _rope(chunks, cos, sin):
    lane = lax.broadcasted_iota(I32, cos.shape, 1)
    upper = (lane & (ROPE_AXIS_DIM // 2)) != 0
    outs = []
    for tc in chunks:
        halves = []
        for j in range(MXU_DIM // LANES):
            xc = tc[:, j * LANES:(j + 1) * LANES]
            partner = jnp.where(upper,
                                pltpu.roll(xc, ROPE_AXIS_DIM // 2, 1),
                                pltpu.roll(xc, LANES - ROPE_AXIS_DIM // 2, 1))
            halves.append(xc * cos + partner * sin)
        outs.append(jnp.concatenate(halves, axis=-1))
    return outs


def _qkv_kernel(x_ref, g_ref, w_ref, qg_ref, kg_ref, cos_ref, sin_ref,
                q_ref, k_ref, v_ref, *, rope, transposed):
    x = x_ref[...]
    ms = jnp.mean(x * x, axis=-1, keepdims=True)
    h = (x * lax.rsqrt(ms + NORM_EPS) * g_ref[...]).astype(BF16)
    qkv = jnp.dot(h, w_ref[...], preferred_element_type=F32)
    nq = N_HEADS * HEAD_DIM
    nk = N_KV_HEADS * HEAD_DIM
    seg = _segment_sum_matrix()
    qs = _head_rmsnorm(qkv[:, :nq], qg_ref[...], seg)
    ks = _head_rmsnorm(qkv[:, nq:nq + nk], kg_ref[...], seg)
    if rope:
        cos = cos_ref[...]
        sin = sin_ref[...]
        qs = _rope(qs, cos, sin)
        ks = _rope(ks, cos, sin)
    heads_per_chunk = MXU_DIM // HEAD_DIM
    v = qkv[:, nq + nk:]
    for c, kc in enumerate(ks):
        for j in range(heads_per_chunk):
            k_ref[0, c * heads_per_chunk + j] = kc[:, j * HEAD_DIM:(j + 1) * HEAD_DIM].astype(BF16)
    if transposed:
        for c, qc in enumerate(qs):
            qt = (qc * (SCALE * LOG2_E)).T
            for j in range(heads_per_chunk):
                q_ref[0, c * heads_per_chunk + j] = qt[j * HEAD_DIM:(j + 1) * HEAD_DIM].astype(BF16)
        vt = v.T
        ones = jnp.ones((V_ROWS - HEAD_DIM, vt.shape[1]), F32)
        for j in range(N_KV_HEADS):
            v_ref[0, j] = jnp.concatenate([vt[j * HEAD_DIM:(j + 1) * HEAD_DIM], ones], axis=0).astype(BF16)
    else:
        for c, qc in enumerate(qs):
            qc = qc * SCALE
            for j in range(heads_per_chunk):
                q_ref[0, c * heads_per_chunk + j] = qc[:, j * HEAD_DIM:(j + 1) * HEAD_DIM].astype(BF16)
        ones = jnp.ones((v.shape[0], LANES - HEAD_DIM), F32)
        for j in range(N_KV_HEADS):
            v_ref[0, j] = jnp.concatenate(
                [v[:, j * HEAD_DIM:(j + 1) * HEAD_DIM], ones], axis=-1).astype(BF16)


def _qkv_call(x2, gain, w, qg, kg, cos, sin, *, batch, seq, rope, transposed, tm):
    T, D = x2.shape
    spt = seq // tm
    qkv_dim = w.shape[1]
    kern = functools.partial(_qkv_kernel, rope=rope, transposed=transposed)
    head_map = lambda i: (i // spt, 0, i % spt, 0)
    feat_map = lambda i: (i // spt, 0, 0, i % spt)
    if transposed:
        q_spec = pl.BlockSpec((1, N_HEADS, HEAD_DIM, tm), feat_map)
        v_spec = pl.BlockSpec((1, N_KV_HEADS, V_ROWS, tm), feat_map)
        q_shape = (batch, N_HEADS, HEAD_DIM, seq)
        v_shape = (batch, N_KV_HEADS, V_ROWS, seq)
    else:
        q_spec = pl.BlockSpec((1, N_HEADS, tm, HEAD_DIM), head_map)
        v_spec = pl.BlockSpec((1, N_KV_HEADS, tm, LANES), head_map)
        q_shape = (batch, N_HEADS, seq, HEAD_DIM)
        v_shape = (batch, N_KV_HEADS, seq, LANES)
    return pl.pallas_call(
        kern,
        grid=(T // tm,),
        in_specs=[
            pl.BlockSpec((tm, D), lambda i: (i, 0)),
            pl.BlockSpec((1, D), lambda i: (0, 0)),
            pl.BlockSpec((D, qkv_dim), lambda i: (0, 0)),
            pl.BlockSpec((1, MXU_DIM), lambda i: (0, 0)),
            pl.BlockSpec((1, MXU_DIM), lambda i: (0, 0)),
            pl.BlockSpec((tm, LANES), lambda i: (i % spt, 0)),
            pl.BlockSpec((tm, LANES), lambda i: (i % spt, 0)),
        ],
        out_specs=[
            q_spec,
            pl.BlockSpec((1, N_KV_HEADS, tm, HEAD_DIM), head_map),
            v_spec,
        ],
        out_shape=[
            jax.ShapeDtypeStruct(q_shape, BF16),
            jax.ShapeDtypeStruct((batch, N_KV_HEADS, seq, HEAD_DIM), BF16),
            jax.ShapeDtypeStruct(v_shape, BF16),
        ],
        compiler_params=_params(("parallel",)),
        name="qkv_proj",
    )(x2, gain, w, qg, kg, cos, sin)


FLASH_RING = 2


def _flash_kernel(qt_ref, k_ref, vt_ref, o_ref, acc_ref, *s_refs, tk):
    ring = len(s_refs)
    tq = qt_ref.shape[3]
    seq = k_ref.shape[2]
    n_chunks = seq // tk
    key_tiles = tk // MXU_DIM
    units = [(g, g * tq + h * MXU_DIM, h * MXU_DIM) for g in range(KV_GROUP) for h in range(tq // MXU_DIM)]

    def tile(c0, kt):
        return (slice(kt * MXU_DIM, (kt + 1) * MXU_DIM), slice(c0, c0 + MXU_DIM))

    def qk_unit(chunk, unit, dst_ref):
        g, c0, q0 = unit
        mx = None
        for kt in range(key_tiles):
            off = pl.multiple_of(chunk * tk + kt * MXU_DIM, MXU_DIM)
            s = jnp.dot(k_ref[0, 0, pl.ds(off, MXU_DIM), :], qt_ref[0, g, :, q0:q0 + MXU_DIM],
                        preferred_element_type=F32)
            dst_ref[tile(c0, kt)] = s
            t = jnp.max(s, axis=0, keepdims=True)
            mx = t if mx is None else jnp.maximum(mx, t)
        return mx

    def attend_unit(chunk, unit, src_ref, mx, m):
        _, c0, _ = unit
        m_new = jnp.maximum(m, mx)
        pv = None
        for kt in range(key_tiles):
            off = pl.multiple_of(chunk * tk + kt * MXU_DIM, MXU_DIM)
            p = jnp.exp2(src_ref[tile(c0, kt)] - m_new).astype(BF16)
            d = jnp.dot(vt_ref[0, 0, :, pl.ds(off, MXU_DIM)], p, preferred_element_type=F32)
            pv = d if pv is None else pv + d
        cols = slice(c0, c0 + MXU_DIM)
        acc_ref[:, cols] = jnp.exp2(m - m_new) * acc_ref[:, cols] + pv
        return m_new

    def step(chunk, u, mx, m, last=False):
        mx_next, m_out = [], []
        for i, unit in enumerate(units):
            if not last:
                mx_next.append(qk_unit(chunk + 1, unit, s_refs[(u + 1) % ring]))
            m_out.append(attend_unit(chunk, unit, s_refs[u], mx[i], m[i]))
        return tuple(mx_next), tuple(m_out)

    def body(j, carry):
        mx, m = carry
        for u in range(ring):
            mx, m = step(ring * j + u, u, mx, m)
        return mx, m

    acc_ref[...] = jnp.zeros_like(acc_ref)
    m = tuple(jnp.full((1, MXU_DIM), NEG_INF, F32) for _ in units)
    mx = tuple(qk_unit(0, unit, s_refs[0]) for unit in units)
    loops = (n_chunks - 1) // ring
    mx, m = lax.fori_loop(0, loops, body, (mx, m))
    for c in range(loops * ring, n_chunks - 1):
        mx, m = step(c, c % ring, mx, m)
    step(n_chunks - 1, (n_chunks - 1) % ring, mx, m, last=True)
    acc = acc_ref[...]
    o = acc[:HEAD_DIM] / acc[HEAD_DIM:HEAD_DIM + 1]
    pairs = []
    for g in range(0, KV_GROUP, 2):
        two_heads = jnp.concatenate([o[:, g * tq:(g + 1) * tq], o[:, (g + 1) * tq:(g + 2) * tq]], axis=0)
        pairs.append(two_heads.T)
    o_ref[0] = jnp.concatenate(pairs, axis=-1).astype(BF16)


def _flash_call(q, k, v, *, tq, tk):
    B, _, _, S = q.shape
    kern = functools.partial(_flash_kernel, tk=tk)
    return pl.pallas_call(
        kern,
        grid=(B, N_KV_HEADS, S // tq),
        in_specs=[
            pl.BlockSpec((1, KV_GROUP, HEAD_DIM, tq), lambda b, h, i: (b, h, 0, i)),
            pl.BlockSpec((1, 1, S, HEAD_DIM), lambda b, h, i: (b, h, 0, 0)),
            pl.BlockSpec((1, 1, V_ROWS, S), lambda b, h, i: (b, h, 0, 0)),
        ],
        out_specs=pl.BlockSpec((1, tq, KV_GROUP * HEAD_DIM), lambda b, h, i: (b, i, h)),
        out_shape=jax.ShapeDtypeStruct((B, S, N_HEADS * HEAD_DIM), BF16),
        scratch_shapes=([pltpu.VMEM((V_ROWS, KV_GROUP * tq), F32)]
                        + [pltpu.VMEM((tk, KV_GROUP * tq), F32)] * FLASH_RING),
        compiler_params=_params(("parallel", "parallel", "parallel")),
        name="flash_global",
    )(q, k, v)


EDGE_VARIANTS = 4
WINDOW_UNROLL = 8


def _merge_heads(o, rows):
    return jnp.concatenate([o[g * rows:(g + 1) * rows] for g in range(KV_GROUP)], axis=-1)


def _bias_kernel(rb_ref, bucket_ref, out_ref):
    h = pl.program_id(0)
    b = bucket_ref[...]
    bias = jnp.zeros(b.shape, F32)
    for j in range(N_REL_BUCKETS):
        bias = jnp.where(b == j, rb_ref[h, j], bias)
    qpos = lax.broadcasted_iota(I32, b.shape, 0)
    col = lax.broadcasted_iota(I32, b.shape, 1)
    in_band = jnp.abs(col - Q_BLOCK - qpos) <= WINDOW
    for e in range(EDGE_VARIANTS):
        valid = in_band
        if e & 1:
            valid = valid & (col >= Q_BLOCK)
        if e & 2:
            valid = valid & (col < 2 * Q_BLOCK)
        out_ref[e, 0] = jnp.where(valid, bias, NEG_INF)


def _bias_call(rel_bias, bucket):
    return pl.pallas_call(
        _bias_kernel,
        grid=(N_HEADS,),
        in_specs=[
            pl.BlockSpec(memory_space=pltpu.SMEM),
            pl.BlockSpec(bucket.shape, lambda h: (0, 0)),
        ],
        out_specs=pl.BlockSpec((EDGE_VARIANTS, 1) + bucket.shape, lambda h: (0, h, 0, 0)),
        out_shape=jax.ShapeDtypeStruct((EDGE_VARIANTS, N_HEADS) + bucket.shape, F32),
        compiler_params=_params(("parallel",)),
        name="rel_bias_table",
    )(rel_bias, bucket)


def _window_kernel(sink_ref, q_ref, k_ref, v_ref, bias_ref, o_ref, *, blocks_per_tile):
    hk = pl.program_id(1)
    tile = pl.program_id(2)
    nb = k_ref.shape[2] // Q_BLOCK
    rows = KV_GROUP * Q_BLOCK
    band_w = 3 * Q_BLOCK

    head = lax.broadcasted_iota(I32, (rows, 1), 0) // Q_BLOCK
    sink = jnp.zeros((rows, 1), F32)
    for g in range(KV_GROUP):
        sink = jnp.where(head == g, sink_ref[hk * KV_GROUP + g], sink)

    def body(j, carry):
        n = tile * blocks_per_tile + j
        off = pl.multiple_of(j * Q_BLOCK, Q_BLOCK)
        qb = q_ref[0, :, pl.ds(off, Q_BLOCK), :].reshape(rows, HEAD_DIM)
        starts = [pl.multiple_of(jnp.maximum(n - 1, 0) * Q_BLOCK, Q_BLOCK),
                  pl.multiple_of(n * Q_BLOCK, Q_BLOCK),
                  pl.multiple_of(jnp.minimum(n + 1, nb - 1) * Q_BLOCK, Q_BLOCK)]
        kcat = jnp.concatenate([k_ref[0, 0, pl.ds(s, Q_BLOCK), :] for s in starts], axis=0)
        vcat = jnp.concatenate([v_ref[0, 0, pl.ds(s, Q_BLOCK), :] for s in starts], axis=0)
        edge = (n == 0).astype(I32) + 2 * (n == nb - 1).astype(I32)
        s = (lax.dot_general(qb, kcat, (((1,), (1,)), ((), ())), preferred_element_type=F32)
             + bias_ref[edge].reshape(rows, band_w))
        m = jnp.maximum(jnp.max(s, axis=-1, keepdims=True), sink)
        p = jnp.exp(s - m).astype(BF16)
        ov = jnp.dot(p, vcat, preferred_element_type=F32)
        denom = ov[:, HEAD_DIM:HEAD_DIM + 1] + jnp.exp(sink - m)
        o = ov[:, :HEAD_DIM] / denom
        o_ref[0, pl.ds(off, Q_BLOCK), :] = _merge_heads(o, Q_BLOCK).astype(BF16)
        return carry

    lax.fori_loop(0, blocks_per_tile, body, 0, unroll=WINDOW_UNROLL)


def _window_call(sink, q, k, v, bias, *, tq):
    B, _, S, _ = q.shape
    kern = functools.partial(_window_kernel, blocks_per_tile=tq // Q_BLOCK)
    return pl.pallas_call(
        kern,
        grid=(B, N_KV_HEADS, S // tq),
        in_specs=[
            pl.BlockSpec(memory_space=pltpu.SMEM),
            pl.BlockSpec((1, KV_GROUP, tq, HEAD_DIM), lambda b, h, i: (b, h, i, 0)),
            pl.BlockSpec((1, 1, S, HEAD_DIM), lambda b, h, i: (b, h, 0, 0)),
            pl.BlockSpec((1, 1, S, LANES), lambda b, h, i: (b, h, 0, 0)),
            pl.BlockSpec((EDGE_VARIANTS, KV_GROUP, Q_BLOCK, 3 * Q_BLOCK), lambda b, h, i: (0, h, 0, 0)),
        ],
        out_specs=pl.BlockSpec((1, tq, KV_GROUP * HEAD_DIM), lambda b, h, i: (b, i, h)),
        out_shape=jax.ShapeDtypeStruct((B, S, N_HEADS * HEAD_DIM), BF16),
        compiler_params=_params(("parallel", "parallel", "parallel")),
        name="window_attn",
    )(sink, q, k, v, bias)


ROW_DMA_UNROLL = 8


def _subl(c, n_rows):
    return pl.ds(c, n_rows, stride=SUBLANES)


def _row_tile(ref, row):
    return ref.at[pl.ds(pl.multiple_of(row * SUBLANES, SUBLANES), SUBLANES)]


def _issue_row_copies(n_rows, make_copy):
    def group(g, carry):
        for u in range(ROW_DMA_UNROLL):
            make_copy(g * ROW_DMA_UNROLL + u).start(priority=u % 2)
        return carry

    lax.fori_loop(0, n_rows // ROW_DMA_UNROLL, group, 0)


def _f32_bits(x):
    return lax.bitcast_convert_type(x, U32)


def _pack_bf16_pair(lo_half, hi_half):
    lo = lax.shift_right_logical(_f32_bits(lo_half.astype(BF16).astype(F32)), jnp.uint32(16))
    hi = _f32_bits(hi_half.astype(BF16).astype(F32)) & jnp.uint32(0xFFFF0000)
    return lo | hi


def _unpack_bf16_pair(words):
    lo = lax.bitcast_convert_type(lax.shift_left(words, jnp.uint32(16)), F32)
    hi = lax.bitcast_convert_type(words & jnp.uint32(0xFFFF0000), F32)
    return lo, hi


def _column_to_lanes(col):
    tm = col.shape[0]
    wide = jnp.broadcast_to(col, (tm, LANES))
    return jnp.concatenate(
        [wide[b * LANES:(b + 1) * LANES].T[0:1] for b in range(tm // LANES)], axis=0)


def _post_kernel(x_ref, o_ref, wo_ref, g_ref, wr_ref,
                 xn_ref, row_ref, cls_ref, rank_ref, cnt_ref, run_ref):
    step = pl.program_id(0)
    tm = x_ref.shape[0]
    half = x_ref.shape[1] // 2

    @pl.when(step == 0)
    def _():
        run_ref[...] = jnp.zeros_like(run_ref)

    xn = x_ref[...] + jnp.dot(o_ref[...], wo_ref[...], preferred_element_type=F32)
    xn_ref[...] = xn
    ms = jnp.mean(xn * xn, axis=-1, keepdims=True)
    t = xn * lax.rsqrt(ms + NORM_EPS) * g_ref[...]
    t_hi = t.astype(BF16)
    t_lo = (t - t_hi.astype(F32)).astype(BF16)
    hi_both = jnp.dot(t_hi, wr_ref[...], preferred_element_type=F32)
    lo_hi = jnp.dot(t_lo, wr_ref[:, :LANES], preferred_element_type=F32)
    logits = hi_both[:, :LANES] + (hi_both[:, LANES:] + lo_hi)

    lane = lax.broadcasted_iota(I32, (tm, LANES), 1)
    lane_f = lane.astype(F32)
    big = jnp.float32(LANES)

    def first_argmax(vals):
        top = jnp.max(vals, axis=-1, keepdims=True)
        idx = jnp.min(jnp.where(vals == top, lane_f, big), axis=-1, keepdims=True)
        return top, idx

    is_group = lane < N_GROUPS
    g_top, g_idx = first_argmax(jnp.where(is_group, logits, -jnp.inf))
    g_sum = jnp.sum(jnp.where(is_group, jnp.exp(logits - g_top), 0.0), axis=-1, keepdims=True)
    g_prob = 1.0 / g_sum

    base = N_GROUPS + EXPERTS_PER_GROUP * g_idx
    in_group = (lane_f >= base) & (lane_f < base + EXPERTS_PER_GROUP)
    e_logits = jnp.where(in_group, logits, -jnp.inf)
    e1, i1 = first_argmax(e_logits)
    e2, i2 = first_argmax(jnp.where(lane_f == i1, -jnp.inf, e_logits))
    r = jnp.exp(e2 - e1)
    w_first = (1.0 / (1.0 + r)) * g_prob
    w_second = (r / (1.0 + r)) * g_prob

    j1 = i1 - base
    j2 = i2 - base
    first_is_lo = j1 < j2
    a = jnp.minimum(j1, j2)
    b = jnp.maximum(j1, j2)
    pair = a * 3.0 - a * (a - 1.0) * 0.5 + (b - a - 1.0)
    cls_f = g_idx * PAIRS_PER_GROUP + pair
    w_lo = jnp.where(first_is_lo, w_first, w_second)
    w_hi = jnp.where(first_is_lo, w_second, w_first)

    onehot = lane_f == cls_f
    before = (lax.broadcasted_iota(I32, (tm, tm), 0) > lax.broadcasted_iota(I32, (tm, tm), 1))
    earlier = jnp.dot(before.astype(BF16), onehot.astype(BF16), preferred_element_type=F32)
    rank = jnp.sum(jnp.where(onehot, earlier + run_ref[...], 0.0), axis=-1, keepdims=True)
    run_ref[...] += jnp.sum(onehot.astype(F32), axis=0, keepdims=True)
    cnt_ref[...] = run_ref[...]

    cls_ref[0] = _column_to_lanes(cls_f).astype(I32)
    rank_ref[0] = _column_to_lanes(rank).astype(I32)

    words = _pack_bf16_pair(t[:, :half], t[:, half:])
    x_subl = half // LANES
    for c in range(x_subl):
        row_ref[_subl(c, tm), :] = words[:, c * LANES:(c + 1) * LANES]
    weights = jnp.where(lane == 0, w_lo, jnp.where(lane == 1, w_hi, 0.0))
    row_ref[_subl(x_subl, tm), :] = _f32_bits(weights)
    for c in range(x_subl + 1, SUBLANES):
        row_ref[_subl(c, tm), :] = jnp.zeros((tm, LANES), U32)


def _post_call(x2, o2, wo, gain, wr, *, tm):
    T, D = x2.shape
    assert D // 2 + LANES <= SUBLANES * LANES
    return pl.pallas_call(
        _post_kernel,
        grid=(T // tm,),
        in_specs=[
            pl.BlockSpec((tm, D), lambda i: (i, 0)),
            pl.BlockSpec((tm, D), lambda i: (i, 0)),
            pl.BlockSpec((D, D), lambda i: (0, 0)),
            pl.BlockSpec((1, D), lambda i: (0, 0)),
            pl.BlockSpec((D, 2 * LANES), lambda i: (0, 0)),
        ],
        out_specs=[
            pl.BlockSpec((tm, D), lambda i: (i, 0)),
            pl.BlockSpec((tm * SUBLANES, LANES), lambda i: (i, 0)),
            pl.BlockSpec((1, tm // LANES, LANES), lambda i: (i, 0, 0)),
            pl.BlockSpec((1, tm // LANES, LANES), lambda i: (i, 0, 0)),
            pl.BlockSpec((1, LANES), lambda i: (0, 0)),
        ],
        out_shape=[
            jax.ShapeDtypeStruct((T, D), F32),
            jax.ShapeDtypeStruct((T * SUBLANES, LANES), U32),
            jax.ShapeDtypeStruct((T // tm, tm // LANES, LANES), I32),
            jax.ShapeDtypeStruct((T // tm, tm // LANES, LANES), I32),
            jax.ShapeDtypeStruct((1, LANES), F32),
        ],
        scratch_shapes=[pltpu.VMEM((1, LANES), F32)],
        compiler_params=_params(("arbitrary",)),
        name="post_attn_router",
    )(x2, o2, wo, gain, wr)


def _pos_kernel(start_ref, cls_ref, rank_ref, pos_ref):
    cls = cls_ref[...]
    base = jnp.zeros(cls.shape, I32)
    for c in range(N_CLASSES):
        base = jnp.where(cls == c, start_ref[c], base)
    pos_ref[...] = base + rank_ref[...]


def _pos_call(row_start, cls, rank):
    return pl.pallas_call(
        _pos_kernel,
        in_specs=[
            pl.BlockSpec(memory_space=pltpu.SMEM),
            pl.BlockSpec(memory_space=pltpu.VMEM),
            pl.BlockSpec(memory_space=pltpu.VMEM),
        ],
        out_specs=pl.BlockSpec(memory_space=pltpu.VMEM),
        out_shape=jax.ShapeDtypeStruct(cls.shape, I32),
        name="moe_positions",
    )(row_start, cls, rank)


def _dispatch_kernel(pos_ref, row_ref, init_ref, out_ref, sem):
    del init_ref
    ts = row_ref.shape[0] // SUBLANES
    _issue_row_copies(
        ts, lambda r: pltpu.make_async_copy(
            _row_tile(row_ref, r), _row_tile(out_ref, pos_ref[0, 0, r]), sem))
    pltpu.make_async_copy(row_ref, out_ref.at[pl.ds(0, ts * SUBLANES)], sem).wait()


def _dispatch_call(pos3, rows, init, *, ts):
    T = rows.shape[0] // SUBLANES
    return pl.pallas_call(
        _dispatch_kernel,
        grid=(T // ts,),
        in_specs=[
            pl.BlockSpec((1, 1, ts), lambda i: (i, 0, 0), memory_space=pltpu.SMEM),
            pl.BlockSpec((ts * SUBLANES, LANES), lambda i: (i, 0)),
            pl.BlockSpec(memory_space=pl.ANY),
        ],
        out_specs=pl.BlockSpec(memory_space=pl.ANY),
        out_shape=jax.ShapeDtypeStruct(init.shape, init.dtype),
        scratch_shapes=[pltpu.SemaphoreType.DMA(())],
        input_output_aliases={2: 0},
        compiler_params=_params(("arbitrary",)),
        name="moe_dispatch",
    )(pos3, rows, init)


def _moe_kernel(ea_ref, eb_ref, used_ref, row_ref, w1a_ref, w3a_ref, w2a_ref,
                w1b_ref, w3b_ref, w2b_ref, y_ref):
    del ea_ref, eb_ref
    i = pl.program_id(0)
    x_subl = w1a_ref.shape[2] // (2 * LANES)

    @pl.when(i < used_ref[0])
    def _():
        tmx = y_ref.shape[0] // SUBLANES
        words = jnp.concatenate([row_ref[_subl(c, tmx), :] for c in range(x_subl)], axis=-1)
        lo, hi = _unpack_bf16_pair(words)
        x = jnp.concatenate([lo, hi], axis=-1).astype(BF16)
        wts = lax.bitcast_convert_type(row_ref[_subl(x_subl, tmx), :], F32)
        w_lo = wts[:, 0:1]
        w_hi = wts[:, 1:2]

        def expert(w1_ref, w3_ref, w2_ref):
            z = jnp.dot(x, w1_ref[0, 0], preferred_element_type=F32)
            u = jnp.dot(x, w3_ref[0, 0], preferred_element_type=F32)
            h = (z * (1.0 / (1.0 + jnp.exp(-z)))) * u
            return jnp.dot(h.astype(BF16), w2_ref[0, 0], preferred_element_type=F32)

        y = (w_lo * expert(w1a_ref, w3a_ref, w2a_ref)
             + w_hi * expert(w1b_ref, w3b_ref, w2b_ref))
        for c in range(SUBLANES):
            y_ref[_subl(c, tmx), :] = y[:, c * LANES:(c + 1) * LANES]

    @pl.when(i >= used_ref[0])
    def _():
        y_ref[...] = jnp.zeros_like(y_ref)


def _moe_call(tile_ea, tile_eb, n_used, rows, w1, w3, w2, *, layer, tmx):
    rows_pad = rows.shape[0] // SUBLANES
    _, _, D, F = w1.shape
    assert D == SUBLANES * LANES
    wa = lambda i, ea, eb, used: (layer, ea[i], 0, 0)
    wb = lambda i, ea, eb, used: (layer, eb[i], 0, 0)
    grid_spec = pltpu.PrefetchScalarGridSpec(
        num_scalar_prefetch=3,
        grid=(rows_pad // tmx,),
        in_specs=[
            pl.BlockSpec((tmx * SUBLANES, LANES), lambda i, ea, eb, used: (i, 0)),
            pl.BlockSpec((1, 1, D, F), wa),
            pl.BlockSpec((1, 1, D, F), wa),
            pl.BlockSpec((1, 1, F, D), wa),
            pl.BlockSpec((1, 1, D, F), wb),
            pl.BlockSpec((1, 1, D, F), wb),
            pl.BlockSpec((1, 1, F, D), wb),
        ],
        out_specs=pl.BlockSpec((tmx * SUBLANES, LANES), lambda i, ea, eb, used: (i, 0)),
    )
    return pl.pallas_call(
        _moe_kernel,
        grid_spec=grid_spec,
        out_shape=jax.ShapeDtypeStruct((rows_pad * SUBLANES, LANES), F32),
        compiler_params=_params(("arbitrary",)),
        name="moe_experts",
    )(tile_ea, tile_eb, n_used, rows, w1, w3, w2, w1, w3, w2)


def _combine_kernel(pos_ref, pos_next_ref, x_ref, y_ref, out_ref, buf_ref, sem):
    i = pl.program_id(0)
    n = pl.num_programs(0)
    tc = x_ref.shape[0]
    slot = i % 2

    def gather(p_ref, s):
        _issue_row_copies(
            tc, lambda r: pltpu.make_async_copy(
                _row_tile(y_ref, p_ref[0, 0, r]), _row_tile(buf_ref.at[s], r), sem.at[s]))

    @pl.when(i == 0)
    def _():
        gather(pos_ref, 0)

    @pl.when(i + 1 < n)
    def _():
        gather(pos_next_ref, 1 - slot)

    pltpu.make_async_copy(y_ref.at[pl.ds(0, tc * SUBLANES)], buf_ref.at[slot], sem.at[slot]).wait()
    for c in range(SUBLANES):
        cols = slice(c * LANES, (c + 1) * LANES)
        out_ref[:, cols] = x_ref[:, cols] + buf_ref[slot, _subl(c, tc), :]


def _combine_call(pos3, xn, y, *, tc):
    T, D = xn.shape
    n = T // tc
    return pl.pallas_call(
        _combine_kernel,
        grid=(n,),
        in_specs=[
            pl.BlockSpec((1, 1, tc), lambda i: (i, 0, 0), memory_space=pltpu.SMEM),
            pl.BlockSpec((1, 1, tc), lambda i: (jnp.minimum(i + 1, n - 1), 0, 0), memory_space=pltpu.SMEM),
            pl.BlockSpec((tc, D), lambda i: (i, 0)),
            pl.BlockSpec(memory_space=pl.ANY),
        ],
        out_specs=pl.BlockSpec((tc, D), lambda i: (i, 0)),
        out_shape=jax.ShapeDtypeStruct((T, D), F32),
        scratch_shapes=[pltpu.VMEM((2, tc * SUBLANES, LANES), F32),
                        pltpu.SemaphoreType.DMA((2,))],
        compiler_params=_params(("arbitrary",)),
        name="moe_combine",
    )(pos3, pos3, xn, y)


def _rope_tables(seq):
    half = ROPE_AXIS_DIM // 2
    freqs = ROPE_THETA ** (-(jnp.arange(half, dtype=F32) * 2.0 / ROPE_AXIS_DIM))
    t = jnp.arange(seq)
    ang_row = (t // GRID_W).astype(F32)[:, None] * freqs[None, :]
    ang_col = (t % GRID_W).astype(F32)[:, None] * freqs[None, :]

    def axis_tables(ang):
        c, s = jnp.cos(ang), jnp.sin(ang)
        return jnp.concatenate([c, c], axis=-1), jnp.concatenate([-s, s], axis=-1)

    cr, sr = axis_tables(ang_row)
    cc, sc = axis_tables(ang_col)
    cos = jnp.concatenate([cr, cc], axis=-1)
    sin = jnp.concatenate([sr, sc], axis=-1)
    reps = LANES // HEAD_DIM
    return jnp.tile(cos, (1, reps)), jnp.tile(sin, (1, reps))


def _t5_bucket_table():
    import math
    rel = jnp.arange(3 * Q_BLOCK)[None, :] - Q_BLOCK - jnp.arange(Q_BLOCK)[:, None]
    nb = N_REL_BUCKETS // 2
    max_exact = nb // 2
    ret = jnp.where(rel > 0, nb, 0)
    n = jnp.abs(rel)
    large = max_exact + (jnp.log(jnp.maximum(n, 1).astype(F32) / max_exact)
                         / math.log(REL_MAX_DIST / max_exact) * (nb - max_exact)).astype(I32)
    large = jnp.minimum(large, nb - 1)
    return (ret + jnp.where(n < max_exact, n, large)).astype(I32)


def _tile_tables(counts, tmx, n_tiles):
    tiles_per_class = (counts + tmx - 1) // tmx
    tile_end = jnp.cumsum(tiles_per_class)
    row_start = ((tile_end - tiles_per_class) * tmx).astype(I32)
    n_used = tile_end[-1].astype(I32)
    tile_id = jnp.minimum(jnp.arange(n_tiles, dtype=I32), jnp.maximum(n_used - 1, 0))
    tile_cls = jnp.minimum(jnp.sum(tile_id[:, None] >= tile_end[None, :], axis=1), N_CLASSES - 1).astype(I32)
    group = tile_cls // PAIRS_PER_GROUP
    pair = tile_cls % PAIRS_PER_GROUP
    tile_ea = group * EXPERTS_PER_GROUP + jnp.asarray(PAIR_LO, I32)[pair]
    tile_eb = group * EXPERTS_PER_GROUP + jnp.asarray(PAIR_HI, I32)[pair]
    return row_start, tile_ea.astype(I32), tile_eb.astype(I32), n_used.reshape(1)


def _tile(n, pref):
    t = min(n, pref)
    assert n % t == 0, (n, pref)
    return t


def kernel(x, ln_mix, w_qkv, q_norm, k_norm, w_o, rel_bias, sinks, ln_ffn, w_group, w_expert, w1, w3, w2):
    B, S, D = x.shape
    depth = w_qkv.shape[0]
    T = B * S
    assert S % GRID_W == 0 and S % Q_BLOCK == 0 and D == N_HEADS * HEAD_DIM

    tm = _tile(S, 512)
    tq_flash = _tile(S, 1024)
    tk_flash = _tile(S, 512)
    tq_win = _tile(S, 1024)
    ts = _tile(T, 256)
    tmx = _tile(T, 256)
    n_tiles = T // tmx + N_CLASSES
    rows_pad = n_tiles * tmx

    cos, sin = _rope_tables(S)
    bias = _bias_call(rel_bias.astype(F32), _t5_bucket_table())

    w_qkv_b = w_qkv.astype(BF16)
    w_o_b = w_o.astype(BF16)
    w1_b, w3_b, w2_b = w1.astype(BF16), w3.astype(BF16), w2.astype(BF16)
    reps = MXU_DIM // HEAD_DIM
    router_f32 = jnp.concatenate(
        [w_group, w_expert,
         jnp.zeros((depth, D, LANES - N_GROUPS - N_EXPERTS), F32)], axis=-1).astype(F32)
    router_hi = router_f32.astype(BF16)
    router_lo = (router_f32 - router_hi.astype(F32)).astype(BF16)
    router = jnp.concatenate([router_hi, router_lo], axis=-1)

    x2 = x.reshape(T, D).astype(F32)
    sorted_rows = jnp.zeros((rows_pad * SUBLANES, LANES), U32)
    for i in range(depth):
        q, k, v = _qkv_call(
            x2, ln_mix[i].reshape(1, D).astype(F32), w_qkv_b[i],
            jnp.tile(q_norm[i].astype(F32), reps).reshape(1, MXU_DIM),
            jnp.tile(k_norm[i].astype(F32), reps).reshape(1, MXU_DIM),
            cos, sin, batch=B, seq=S, rope=(i % N_MIXERS == 0), transposed=(i % N_MIXERS == 0), tm=tm)
        if i % N_MIXERS == 0:
            o = _flash_call(q, k, v, tq=tq_flash, tk=tk_flash)
        else:
            o = _window_call(sinks[i // N_MIXERS].astype(F32), q, k, v, bias, tq=tq_win)
        xn, rows, cls, rank, counts = _post_call(
            x2, o.reshape(T, D), w_o_b[i], ln_ffn[i].reshape(1, D).astype(F32), router[i], tm=tm)

        row_start, tile_ea, tile_eb, n_used = _tile_tables(
            counts[0, :N_CLASSES].astype(I32), tmx, n_tiles)
        pos = _pos_call(row_start, cls.reshape(T // LANES, LANES), rank.reshape(T // LANES, LANES))
        pos3 = pos.reshape(T // ts, 1, ts)
        sorted_rows = _dispatch_call(pos3, rows, sorted_rows, ts=ts)
        y = _moe_call(tile_ea, tile_eb, n_used, sorted_rows, w1_b, w3_b, w2_b, layer=i, tmx=tmx)
        x2 = _combine_call(pos3, xn, y, tc=ts)
    return x2.reshape(B, S, D).astype(x.dtype)
```

```python
import functools

import jax
import jax.numpy as jnp
from jax import lax
from jax.experimental import pallas as pl
from jax.experimental.pallas import tpu as pltpu

N_HEADS = 16
N_KV_HEADS = 4
HEAD_DIM = 64
KV_GROUP = N_HEADS // N_KV_HEADS
SCALE = HEAD_DIM ** -0.5
GRID_W = 64
ROPE_THETA = 10000.0
ROPE_AXIS_DIM = HEAD_DIM // 2
Q_BLOCK = 128
WINDOW = 128
N_MIXERS = 2
N_REL_BUCKETS = 32
REL_MAX_DIST = 128
N_GROUPS = 4
EXPERTS_PER_GROUP = 4
N_EXPERTS = N_GROUPS * EXPERTS_PER_GROUP
NORM_EPS = 1e-6
NEG_INF = -1e30
LOG2_E = 1.4426950408889634
V_ROWS = HEAD_DIM + 16

PAIRS_PER_GROUP = 6
N_CLASSES = N_GROUPS * PAIRS_PER_GROUP
PAIR_LO = (0, 0, 0, 1, 1, 2)
PAIR_HI = (1, 2, 3, 2, 3, 3)

LANES = 128
SUBLANES = 8
MXU_DIM = 256
VMEM_LIMIT = 48 * 1024 * 1024

F32 = jnp.float32
BF16 = jnp.bfloat16
U32 = jnp.uint32
I32 = jnp.int32


def _params(sem, flags=None):
    return pltpu.CompilerParams(dimension_semantics=sem, vmem_limit_bytes=VMEM_LIMIT, flags=flags)


def _segment_sum_matrix():
    r = lax.broadcasted_iota(I32, (MXU_DIM, MXU_DIM), 0) // HEAD_DIM
    c = lax.broadcasted_iota(I32, (MXU_DIM, MXU_DIM), 1) // HEAD_DIM
    return (r == c).astype(BF16)


def _head_rmsnorm(t, gain, seg):
    outs = []
    for c in range(t.shape[1] // MXU_DIM):
        tc = t[:, c * MXU_DIM:(c + 1) * MXU_DIM]
        sq = tc * tc
        hi = sq.astype(BF16)
        lo = (sq - hi.astype(F32)).astype(BF16)
        ss = (jnp.dot(hi, seg, preferred_element_type=F32)
              + jnp.dot(lo, seg, preferred_element_type=F32))
        outs.append(tc * lax.rsqrt(ss * (1.0 / HEAD_DIM) + NORM_EPS) * gain)
    return outs


def _rope(chunks, cos, sin):
    lane = lax.broadcasted_iota(I32, cos.shape, 1)
    upper = (lane & (ROPE_AXIS_DIM // 2)) != 0
    outs = []
    for tc in chunks:
        halves = []
        for j in range(MXU_DIM // LANES):
            xc = tc[:, j * LANES:(j + 1) * LANES]
            partner = jnp.where(upper,
                                pltpu.roll(xc, ROPE_AXIS_DIM // 2, 1),
                                pltpu.roll(xc, LANES - ROPE_AXIS_DIM // 2, 1))
            halves.append(xc * cos + partner * sin)
        outs.append(jnp.concatenate(halves, axis=-1))
    return outs


def _qkv_kernel(x_ref, g_ref, w_ref, qg_ref, kg_ref, cos_ref, sin_ref,
                q_ref, k_ref, v_ref, *, rope):
    x = x_ref[...]
    ms = jnp.mean(x * x, axis=-1, keepdims=True)
    h = (x * lax.rsqrt(ms + NORM_EPS) * g_ref[...]).astype(BF16)
    qkv = jnp.dot(h, w_ref[...], preferred_element_type=F32)
    nq = N_HEADS * HEAD_DIM
    nk = N_KV_HEADS * HEAD_DIM
    seg = _segment_sum_matrix()
    qs = _head_rmsnorm(qkv[:, :nq], qg_ref[...], seg)
    ks = _head_rmsnorm(qkv[:, nq:nq + nk], kg_ref[...], seg)
    if rope:
        cos = cos_ref[...]
        sin = sin_ref[...]
        qs = _rope(qs, cos, sin)
        ks = _rope(ks, cos, sin)
    heads_per_chunk = MXU_DIM // HEAD_DIM
    v = qkv[:, nq + nk:]
    for c, kc in enumerate(ks):
        for j in range(heads_per_chunk):
            k_ref[0, c * heads_per_chunk + j] = kc[:, j * HEAD_DIM:(j + 1) * HEAD_DIM].astype(BF16)
    for c, qc in enumerate(qs):
        qt = (qc * (SCALE * LOG2_E)).T
        for j in range(heads_per_chunk):
            q_ref[0, c * heads_per_chunk + j] = qt[j * HEAD_DIM:(j + 1) * HEAD_DIM].astype(BF16)
    vt = v.T
    ones = jnp.ones((V_ROWS - HEAD_DIM, vt.shape[1]), F32)
    for j in range(N_KV_HEADS):
        v_ref[0, j] = jnp.concatenate([vt[j * HEAD_DIM:(j + 1) * HEAD_DIM], ones], axis=0).astype(BF16)


def _qkv_call(x2, gain, w, qg, kg, cos, sin, *, batch, seq, rope, tm):
    T, D = x2.shape
    spt = seq // tm
    qkv_dim = w.shape[1]
    kern = functools.partial(_qkv_kernel, rope=rope)
    head_map = lambda i: (i // spt, 0, i % spt, 0)
    feat_map = lambda i: (i // spt, 0, 0, i % spt)
    return pl.pallas_call(
        kern,
        grid=(T // tm,),
        in_specs=[
            pl.BlockSpec((tm, D), lambda i: (i, 0)),
            pl.BlockSpec((1, D), lambda i: (0, 0)),
            pl.BlockSpec((D, qkv_dim), lambda i: (0, 0)),
            pl.BlockSpec((1, MXU_DIM), lambda i: (0, 0)),
            pl.BlockSpec((1, MXU_DIM), lambda i: (0, 0)),
            pl.BlockSpec((tm, LANES), lambda i: (i % spt, 0)),
            pl.BlockSpec((tm, LANES), lambda i: (i % spt, 0)),
        ],
        out_specs=[
            pl.BlockSpec((1, N_HEADS, HEAD_DIM, tm), feat_map),
            pl.BlockSpec((1, N_KV_HEADS, tm, HEAD_DIM), head_map),
            pl.BlockSpec((1, N_KV_HEADS, V_ROWS, tm), feat_map),
        ],
        out_shape=[
            jax.ShapeDtypeStruct((batch, N_HEADS, HEAD_DIM, seq), BF16),
            jax.ShapeDtypeStruct((batch, N_KV_HEADS, seq, HEAD_DIM), BF16),
            jax.ShapeDtypeStruct((batch, N_KV_HEADS, V_ROWS, seq), BF16),
        ],
        compiler_params=_params(("parallel",)),
        name="qkv_proj",
    )(x2, gain, w, qg, kg, cos, sin)


FLASH_RING = 2


def _heads_to_rows(o, width):
    pairs = []
    for g in range(0, KV_GROUP, 2):
        two_heads = jnp.concatenate(
            [o[:, g * width:(g + 1) * width], o[:, (g + 1) * width:(g + 2) * width]], axis=0)
        pairs.append(two_heads.T)
    return jnp.concatenate(pairs, axis=-1)


def _flash_kernel(qt_ref, k_ref, vt_ref, o_ref, acc_ref, *s_refs, tk):
    ring = len(s_refs)
    tq = qt_ref.shape[3]
    seq = k_ref.shape[2]
    n_chunks = seq // tk
    key_tiles = tk // MXU_DIM
    units = [(g, g * tq + h * MXU_DIM, h * MXU_DIM) for g in range(KV_GROUP) for h in range(tq // MXU_DIM)]

    def tile(c0, kt):
        return (slice(kt * MXU_DIM, (kt + 1) * MXU_DIM), slice(c0, c0 + MXU_DIM))

    def qk_unit(chunk, unit, dst_ref):
        g, c0, q0 = unit
        mx = None
        for kt in range(key_tiles):
            off = pl.multiple_of(chunk * tk + kt * MXU_DIM, MXU_DIM)
            s = jnp.dot(k_ref[0, 0, pl.ds(off, MXU_DIM), :], qt_ref[0, g, :, q0:q0 + MXU_DIM],
                        preferred_element_type=F32)
            dst_ref[tile(c0, kt)] = s
            t = jnp.max(s, axis=0, keepdims=True)
            mx = t if mx is None else jnp.maximum(mx, t)
        return mx

    def attend_unit(chunk, unit, src_ref, mx, m):
        _, c0, _ = unit
        m_new = jnp.maximum(m, mx)
        pv = None
        for kt in range(key_tiles):
            off = pl.multiple_of(chunk * tk + kt * MXU_DIM, MXU_DIM)
            p = jnp.exp2(src_ref[tile(c0, kt)] - m_new).astype(BF16)
            d = jnp.dot(vt_ref[0, 0, :, pl.ds(off, MXU_DIM)], p, preferred_element_type=F32)
            pv = d if pv is None else pv + d
        cols = slice(c0, c0 + MXU_DIM)
        acc_ref[:, cols] = jnp.exp2(m - m_new) * acc_ref[:, cols] + pv
        return m_new

    def step(chunk, u, mx, m, last=False):
        mx_next, m_out = [], []
        for i, unit in enumerate(units):
            if not last:
                mx_next.append(qk_unit(chunk + 1, unit, s_refs[(u + 1) % ring]))
            m_out.append(attend_unit(chunk, unit, s_refs[u], mx[i], m[i]))
        return tuple(mx_next), tuple(m_out)

    def body(j, carry):
        mx, m = carry
        for u in range(ring):
            mx, m = step(ring * j + u, u, mx, m)
        return mx, m

    acc_ref[...] = jnp.zeros_like(acc_ref)
    m = tuple(jnp.full((1, MXU_DIM), NEG_INF, F32) for _ in units)
    mx = tuple(qk_unit(0, unit, s_refs[0]) for unit in units)
    loops = (n_chunks - 1) // ring
    mx, m = lax.fori_loop(0, loops, body, (mx, m))
    for c in range(loops * ring, n_chunks - 1):
        mx, m = step(c, c % ring, mx, m)
    step(n_chunks - 1, (n_chunks - 1) % ring, mx, m, last=True)
    acc = acc_ref[...]
    o_ref[0] = _heads_to_rows(acc[:HEAD_DIM] / acc[HEAD_DIM:HEAD_DIM + 1], tq).astype(BF16)


def _flash_call(q, k, v, *, tq, tk):
    B, _, _, S = q.shape
    kern = functools.partial(_flash_kernel, tk=tk)
    return pl.pallas_call(
        kern,
        grid=(B, N_KV_HEADS, S // tq),
        in_specs=[
            pl.BlockSpec((1, KV_GROUP, HEAD_DIM, tq), lambda b, h, i: (b, h, 0, i)),
            pl.BlockSpec((1, 1, S, HEAD_DIM), lambda b, h, i: (b, h, 0, 0)),
            pl.BlockSpec((1, 1, V_ROWS, S), lambda b, h, i: (b, h, 0, 0)),
        ],
        out_specs=pl.BlockSpec((1, tq, KV_GROUP * HEAD_DIM), lambda b, h, i: (b, i, h)),
        out_shape=jax.ShapeDtypeStruct((B, S, N_HEADS * HEAD_DIM), BF16),
        scratch_shapes=([pltpu.VMEM((V_ROWS, KV_GROUP * tq), F32)]
                        + [pltpu.VMEM((tk, KV_GROUP * tq), F32)] * FLASH_RING),
        compiler_params=_params(("parallel", "parallel", "parallel")),
        name="flash_global",
    )(q, k, v)


EDGE_VARIANTS = 4


def _bias_kernel(rb_ref, bucket_ref, out_ref):
    h = pl.program_id(0)
    b = bucket_ref[...]
    bias = jnp.zeros(b.shape, F32)
    for j in range(N_REL_BUCKETS):
        bias = jnp.where(b == j, rb_ref[h, j], bias)
    qpos = lax.broadcasted_iota(I32, b.shape, 0)
    col = lax.broadcasted_iota(I32, b.shape, 1)
    in_band = jnp.abs(col - Q_BLOCK - qpos) <= WINDOW
    for e in range(EDGE_VARIANTS):
        valid = in_band
        if e & 1:
            valid = valid & (col >= Q_BLOCK)
        if e & 2:
            valid = valid & (col < 2 * Q_BLOCK)
        out_ref[e, 0] = (jnp.where(valid, bias, NEG_INF) * LOG2_E).T


def _bias_call(rel_bias, bucket):
    table = (bucket.shape[1], bucket.shape[0])
    return pl.pallas_call(
        _bias_kernel,
        grid=(N_HEADS,),
        in_specs=[
            pl.BlockSpec(memory_space=pltpu.SMEM),
            pl.BlockSpec(bucket.shape, lambda h: (0, 0)),
        ],
        out_specs=pl.BlockSpec((EDGE_VARIANTS, 1) + table, lambda h: (0, h, 0, 0)),
        out_shape=jax.ShapeDtypeStruct((EDGE_VARIANTS, N_HEADS) + table, F32),
        compiler_params=_params(("parallel",)),
        name="rel_bias_table",
    )(rel_bias, bucket)


def _window_kernel(sink_ref, qt_ref, k_ref, vt_ref, bias_ref, o_ref, *s_refs, blocks_per_tile):
    hk = pl.program_id(1)
    tile = pl.program_id(2)
    nb = k_ref.shape[2] // Q_BLOCK
    ring = len(s_refs)
    pairs = range(KV_GROUP // 2)

    sinks = []
    for p in pairs:
        first = lax.broadcasted_iota(I32, (1, MXU_DIM), 1) < Q_BLOCK
        sinks.append(jnp.where(first, sink_ref[hk * KV_GROUP + 2 * p],
                               sink_ref[hk * KV_GROUP + 2 * p + 1]) * LOG2_E)

    def key_starts(j):
        n = tile * blocks_per_tile + j
        return n, [pl.multiple_of(jnp.maximum(n - 1, 0) * Q_BLOCK, Q_BLOCK),
                   pl.multiple_of(n * Q_BLOCK, Q_BLOCK),
                   pl.multiple_of(jnp.minimum(n + 1, nb - 1) * Q_BLOCK, Q_BLOCK)]

    def qk_block(j, dst_ref):
        n, starts = key_starts(j)
        kcat = jnp.concatenate([k_ref[0, 0, pl.ds(s, Q_BLOCK), :] for s in starts], axis=0)
        edge = (n == 0).astype(I32) + 2 * (n == nb - 1).astype(I32)
        maxes = []
        for p in pairs:
            heads = (2 * p, 2 * p + 1)
            qt = jnp.concatenate(
                [qt_ref[0, h, :, j * Q_BLOCK:(j + 1) * Q_BLOCK] for h in heads], axis=-1)
            bias = jnp.concatenate([bias_ref[edge, h] for h in heads], axis=-1)
            s = jnp.dot(kcat, qt, preferred_element_type=F32) + bias
            dst_ref[:, p * MXU_DIM:(p + 1) * MXU_DIM] = s
            maxes.append(jnp.max(s, axis=0, keepdims=True))
        return maxes

    def attend_block(j, src_ref, maxes):
        _, starts = key_starts(j)
        vcat = jnp.concatenate([vt_ref[0, 0, :, pl.ds(s, Q_BLOCK)] for s in starts], axis=-1)
        for p in pairs:
            m = jnp.maximum(maxes[p], sinks[p])
            pr = jnp.exp2(src_ref[:, p * MXU_DIM:(p + 1) * MXU_DIM] - m).astype(BF16)
            ov = jnp.dot(vcat, pr, preferred_element_type=F32)
            o = ov[:HEAD_DIM] / (ov[HEAD_DIM:HEAD_DIM + 1] + jnp.exp2(sinks[p] - m))
            two_heads = jnp.concatenate([o[:, :Q_BLOCK], o[:, Q_BLOCK:]], axis=0)
            o_ref[0, j * Q_BLOCK:(j + 1) * Q_BLOCK, p * LANES:(p + 1) * LANES] = two_heads.T.astype(BF16)

    maxes = qk_block(0, s_refs[0])
    for j in range(blocks_per_tile):
        nxt = qk_block(j + 1, s_refs[(j + 1) % ring]) if j + 1 < blocks_per_tile else None
        attend_block(j, s_refs[j % ring], maxes)
        maxes = nxt


def _window_call(sink, q, k, v, bias, *, tq):
    B, _, _, S = q.shape
    kern = functools.partial(_window_kernel, blocks_per_tile=tq // Q_BLOCK)
    return pl.pallas_call(
        kern,
        grid=(B, N_KV_HEADS, S // tq),
        in_specs=[
            pl.BlockSpec(memory_space=pltpu.SMEM),
            pl.BlockSpec((1, KV_GROUP, HEAD_DIM, tq), lambda b, h, i: (b, h, 0, i)),
            pl.BlockSpec((1, 1, S, HEAD_DIM), lambda b, h, i: (b, h, 0, 0)),
            pl.BlockSpec((1, 1, V_ROWS, S), lambda b, h, i: (b, h, 0, 0)),
            pl.BlockSpec((EDGE_VARIANTS, KV_GROUP, 3 * Q_BLOCK, Q_BLOCK), lambda b, h, i: (0, h, 0, 0)),
        ],
        out_specs=pl.BlockSpec((1, tq, KV_GROUP * HEAD_DIM), lambda b, h, i: (b, i, h)),
        out_shape=jax.ShapeDtypeStruct((B, S, N_HEADS * HEAD_DIM), BF16),
        scratch_shapes=[pltpu.VMEM((3 * Q_BLOCK, KV_GROUP * Q_BLOCK), F32)] * 2,
        compiler_params=_params(("parallel", "parallel", "parallel")),
        name="window_attn",
    )(sink, q, k, v, bias)


ROW_DMA_UNROLL = 8


def _subl(c, n_rows):
    return pl.ds(c, n_rows, stride=SUBLANES)


def _row_tile(ref, row):
    return ref.at[pl.ds(pl.multiple_of(row * SUBLANES, SUBLANES), SUBLANES)]


def _issue_row_copies(n_rows, make_copy):
    def group(g, carry):
        for u in range(ROW_DMA_UNROLL):
            make_copy(g * ROW_DMA_UNROLL + u).start(priority=u % 2)
        return carry

    lax.fori_loop(0, n_rows // ROW_DMA_UNROLL, group, 0)


def _f32_bits(x):
    return lax.bitcast_convert_type(x, U32)


def _pack_bf16_pair(lo_half, hi_half):
    lo = lax.shift_right_logical(_f32_bits(lo_half.astype(BF16).astype(F32)), jnp.uint32(16))
    hi = _f32_bits(hi_half.astype(BF16).astype(F32)) & jnp.uint32(0xFFFF0000)
    return lo | hi


def _unpack_bf16_pair(words):
    lo = lax.bitcast_convert_type(lax.shift_left(words, jnp.uint32(16)), F32)
    hi = lax.bitcast_convert_type(words & jnp.uint32(0xFFFF0000), F32)
    return lo, hi


def _column_to_lanes(col):
    tm = col.shape[0]
    wide = jnp.broadcast_to(col, (tm, LANES))
    return jnp.concatenate(
        [wide[b * LANES:(b + 1) * LANES].T[0:1] for b in range(tm // LANES)], axis=0)


def _post_kernel(x_ref, o_ref, wo_ref, g_ref, wr_ref,
                 xn_ref, row_ref, cls_ref, rank_ref, cnt_ref, run_ref):
    step = pl.program_id(0)
    tm = x_ref.shape[0]
    half = x_ref.shape[1] // 2

    @pl.when(step == 0)
    def _():
        run_ref[...] = jnp.zeros_like(run_ref)

    xn = x_ref[...] + jnp.dot(o_ref[...], wo_ref[...], preferred_element_type=F32)
    xn_ref[...] = xn
    ms = jnp.mean(xn * xn, axis=-1, keepdims=True)
    t = xn * lax.rsqrt(ms + NORM_EPS) * g_ref[...]
    t_hi = t.astype(BF16)
    t_lo = (t - t_hi.astype(F32)).astype(BF16)
    hi_both = jnp.dot(t_hi, wr_ref[...], preferred_element_type=F32)
    lo_hi = jnp.dot(t_lo, wr_ref[:, :LANES], preferred_element_type=F32)
    logits = hi_both[:, :LANES] + (hi_both[:, LANES:] + lo_hi)

    lane = lax.broadcasted_iota(I32, (tm, LANES), 1)
    lane_f = lane.astype(F32)
    big = jnp.float32(LANES)

    def first_argmax(vals):
        top = jnp.max(vals, axis=-1, keepdims=True)
        idx = jnp.min(jnp.where(vals == top, lane_f, big), axis=-1, keepdims=True)
        return top, idx

    is_group = lane < N_GROUPS
    g_top, g_idx = first_argmax(jnp.where(is_group, logits, -jnp.inf))
    g_sum = jnp.sum(jnp.where(is_group, jnp.exp(logits - g_top), 0.0), axis=-1, keepdims=True)
    g_prob = 1.0 / g_sum

    base = N_GROUPS + EXPERTS_PER_GROUP * g_idx
    in_group = (lane_f >= base) & (lane_f < base + EXPERTS_PER_GROUP)
    e_logits = jnp.where(in_group, logits, -jnp.inf)
    e1, i1 = first_argmax(e_logits)
    e2, i2 = first_argmax(jnp.where(lane_f == i1, -jnp.inf, e_logits))
    r = jnp.exp(e2 - e1)
    w_first = (1.0 / (1.0 + r)) * g_prob
    w_second = (r / (1.0 + r)) * g_prob

    j1 = i1 - base
    j2 = i2 - base
    first_is_lo = j1 < j2
    a = jnp.minimum(j1, j2)
    b = jnp.maximum(j1, j2)
    pair = a * 3.0 - a * (a - 1.0) * 0.5 + (b - a - 1.0)
    cls_f = g_idx * PAIRS_PER_GROUP + pair
    w_lo = jnp.where(first_is_lo, w_first, w_second)
    w_hi = jnp.where(first_is_lo, w_second, w_first)

    onehot = lane_f == cls_f
    before = (lax.broadcasted_iota(I32, (tm, tm), 0) > lax.broadcasted_iota(I32, (tm, tm), 1))
    earlier = jnp.dot(before.astype(BF16), onehot.astype(BF16), preferred_element_type=F32)
    rank = jnp.sum(jnp.where(onehot, earlier + run_ref[...], 0.0), axis=-1, keepdims=True)
    run_ref[...] += jnp.sum(onehot.astype(F32), axis=0, keepdims=True)
    cnt_ref[...] = run_ref[...]

    cls_ref[0] = _column_to_lanes(cls_f).astype(I32)
    rank_ref[0] = _column_to_lanes(rank).astype(I32)

    words = _pack_bf16_pair(t[:, :half], t[:, half:])
    x_subl = half // LANES
    for c in range(x_subl):
        row_ref[_subl(c, tm), :] = words[:, c * LANES:(c + 1) * LANES]
    weights = jnp.where(lane == 0, w_lo, jnp.where(lane == 1, w_hi, 0.0))
    row_ref[_subl(x_subl, tm), :] = _f32_bits(weights)
    for c in range(x_subl + 1, SUBLANES):
        row_ref[_subl(c, tm), :] = jnp.zeros((tm, LANES), U32)


def _post_call(x2, o2, wo, gain, wr, *, tm):
    T, D = x2.shape
    assert D // 2 + LANES <= SUBLANES * LANES
    return pl.pallas_call(
        _post_kernel,
        grid=(T // tm,),
        in_specs=[
            pl.BlockSpec((tm, D), lambda i: (i, 0)),
            pl.BlockSpec((tm, D), lambda i: (i, 0)),
            pl.BlockSpec((D, D), lambda i: (0, 0)),
            pl.BlockSpec((1, D), lambda i: (0, 0)),
            pl.BlockSpec((D, 2 * LANES), lambda i: (0, 0)),
        ],
        out_specs=[
            pl.BlockSpec((tm, D), lambda i: (i, 0)),
            pl.BlockSpec((tm * SUBLANES, LANES), lambda i: (i, 0)),
            pl.BlockSpec((1, tm // LANES, LANES), lambda i: (i, 0, 0)),
            pl.BlockSpec((1, tm // LANES, LANES), lambda i: (i, 0, 0)),
            pl.BlockSpec((1, LANES), lambda i: (0, 0)),
        ],
        out_shape=[
            jax.ShapeDtypeStruct((T, D), F32),
            jax.ShapeDtypeStruct((T * SUBLANES, LANES), U32),
            jax.ShapeDtypeStruct((T // tm, tm // LANES, LANES), I32),
            jax.ShapeDtypeStruct((T // tm, tm // LANES, LANES), I32),
            jax.ShapeDtypeStruct((1, LANES), F32),
        ],
        scratch_shapes=[pltpu.VMEM((1, LANES), F32)],
        compiler_params=_params(("arbitrary",)),
        name="post_attn_router",
    )(x2, o2, wo, gain, wr)


def _pos_kernel(start_ref, cls_ref, rank_ref, pos_ref):
    cls = cls_ref[...]
    base = jnp.zeros(cls.shape, I32)
    for c in range(N_CLASSES):
        base = jnp.where(cls == c, start_ref[c], base)
    pos_ref[...] = base + rank_ref[...]


def _pos_call(row_start, cls, rank):
    return pl.pallas_call(
        _pos_kernel,
        in_specs=[
            pl.BlockSpec(memory_space=pltpu.SMEM),
            pl.BlockSpec(memory_space=pltpu.VMEM),
            pl.BlockSpec(memory_space=pltpu.VMEM),
        ],
        out_specs=pl.BlockSpec(memory_space=pltpu.VMEM),
        out_shape=jax.ShapeDtypeStruct(cls.shape, I32),
        name="moe_positions",
    )(row_start, cls, rank)


def _dispatch_kernel(pos_ref, row_ref, init_ref, out_ref, sem):
    del init_ref
    ts = row_ref.shape[0] // SUBLANES
    _issue_row_copies(
        ts, lambda r: pltpu.make_async_copy(
            _row_tile(row_ref, r), _row_tile(out_ref, pos_ref[0, 0, r]), sem))
    pltpu.make_async_copy(row_ref, out_ref.at[pl.ds(0, ts * SUBLANES)], sem).wait()


def _dispatch_call(pos3, rows, init, *, ts):
    T = rows.shape[0] // SUBLANES
    return pl.pallas_call(
        _dispatch_kernel,
        grid=(T // ts,),
        in_specs=[
            pl.BlockSpec((1, 1, ts), lambda i: (i, 0, 0), memory_space=pltpu.SMEM),
            pl.BlockSpec((ts * SUBLANES, LANES), lambda i: (i, 0)),
            pl.BlockSpec(memory_space=pl.ANY),
        ],
        out_specs=pl.BlockSpec(memory_space=pl.ANY),
        out_shape=jax.ShapeDtypeStruct(init.shape, init.dtype),
        scratch_shapes=[pltpu.SemaphoreType.DMA(())],
        input_output_aliases={2: 0},
        compiler_params=_params(("arbitrary",)),
        name="moe_dispatch",
    )(pos3, rows, init)


def _moe_kernel(ea_ref, eb_ref, used_ref, row_ref, w1a_ref, w3a_ref, w2a_ref,
                w1b_ref, w3b_ref, w2b_ref, y_ref):
    del ea_ref, eb_ref
    i = pl.program_id(0)
    x_subl = w1a_ref.shape[2] // (2 * LANES)

    @pl.when(i < used_ref[0])
    def _():
        tmx = y_ref.shape[0] // SUBLANES
        words = jnp.concatenate([row_ref[_subl(c, tmx), :] for c in range(x_subl)], axis=-1)
        lo, hi = _unpack_bf16_pair(words)
        x = jnp.concatenate([lo, hi], axis=-1).astype(BF16)
        wts = lax.bitcast_convert_type(row_ref[_subl(x_subl, tmx), :], F32)
        w_lo = wts[:, 0:1]
        w_hi = wts[:, 1:2]

        def expert(w1_ref, w3_ref, w2_ref):
            z = jnp.dot(x, w1_ref[0, 0], preferred_element_type=F32)
            u = jnp.dot(x, w3_ref[0, 0], preferred_element_type=F32)
            h = (z * (1.0 / (1.0 + jnp.exp(-z)))) * u
            return jnp.dot(h.astype(BF16), w2_ref[0, 0], preferred_element_type=F32)

        y = (w_lo * expert(w1a_ref, w3a_ref, w2a_ref)
             + w_hi * expert(w1b_ref, w3b_ref, w2b_ref))
        for c in range(SUBLANES):
            y_ref[_subl(c, tmx), :] = y[:, c * LANES:(c + 1) * LANES]

    @pl.when(i >= used_ref[0])
    def _():
        y_ref[...] = jnp.zeros_like(y_ref)


def _moe_call(tile_ea, tile_eb, n_used, rows, w1, w3, w2, *, layer, tmx):
    rows_pad = rows.shape[0] // SUBLANES
    _, _, D, F = w1.shape
    assert D == SUBLANES * LANES
    wa = lambda i, ea, eb, used: (layer, ea[i], 0, 0)
    wb = lambda i, ea, eb, used: (layer, eb[i], 0, 0)
    grid_spec = pltpu.PrefetchScalarGridSpec(
        num_scalar_prefetch=3,
        grid=(rows_pad // tmx,),
        in_specs=[
            pl.BlockSpec((tmx * SUBLANES, LANES), lambda i, ea, eb, used: (i, 0)),
            pl.BlockSpec((1, 1, D, F), wa),
            pl.BlockSpec((1, 1, D, F), wa),
            pl.BlockSpec((1, 1, F, D), wa),
            pl.BlockSpec((1, 1, D, F), wb),
            pl.BlockSpec((1, 1, D, F), wb),
            pl.BlockSpec((1, 1, F, D), wb),
        ],
        out_specs=pl.BlockSpec((tmx * SUBLANES, LANES), lambda i, ea, eb, used: (i, 0)),
    )
    return pl.pallas_call(
        _moe_kernel,
        grid_spec=grid_spec,
        out_shape=jax.ShapeDtypeStruct((rows_pad * SUBLANES, LANES), F32),
        compiler_params=_params(("arbitrary",)),
        name="moe_experts",
    )(tile_ea, tile_eb, n_used, rows, w1, w3, w2, w1, w3, w2)


def _combine_kernel(pos_ref, pos_next_ref, x_ref, y_ref, out_ref, buf_ref, sem):
    i = pl.program_id(0)
    n = pl.num_programs(0)
    tc = x_ref.shape[0]
    slot = i % 2

    def gather(p_ref, s):
        _issue_row_copies(
            tc, lambda r: pltpu.make_async_copy(
                _row_tile(y_ref, p_ref[0, 0, r]), _row_tile(buf_ref.at[s], r), sem.at[s]))

    @pl.when(i == 0)
    def _():
        gather(pos_ref, 0)

    @pl.when(i + 1 < n)
    def _():
        gather(pos_next_ref, 1 - slot)

    pltpu.make_async_copy(y_ref.at[pl.ds(0, tc * SUBLANES)], buf_ref.at[slot], sem.at[slot]).wait()
    for c in range(SUBLANES):
        cols = slice(c * LANES, (c + 1) * LANES)
        out_ref[:, cols] = x_ref[:, cols] + buf_ref[slot, _subl(c, tc), :]


def _combine_call(pos3, xn, y, *, tc):
    T, D = xn.shape
    n = T // tc
    return pl.pallas_call(
        _combine_kernel,
        grid=(n,),
        in_specs=[
            pl.BlockSpec((1, 1, tc), lambda i: (i, 0, 0), memory_space=pltpu.SMEM),
            pl.BlockSpec((1, 1, tc), lambda i: (jnp.minimum(i + 1, n - 1), 0, 0), memory_space=pltpu.SMEM),
            pl.BlockSpec((tc, D), lambda i: (i, 0)),
            pl.BlockSpec(memory_space=pl.ANY),
        ],
        out_specs=pl.BlockSpec((tc, D), lambda i: (i, 0)),
        out_shape=jax.ShapeDtypeStruct((T, D), F32),
        scratch_shapes=[pltpu.VMEM((2, tc * SUBLANES, LANES), F32),
                        pltpu.SemaphoreType.DMA((2,))],
        compiler_params=_params(("arbitrary",)),
        name="moe_combine",
    )(pos3, pos3, xn, y)


def _rope_tables(seq):
    half = ROPE_AXIS_DIM // 2
    freqs = ROPE_THETA ** (-(jnp.arange(half, dtype=F32) * 2.0 / ROPE_AXIS_DIM))
    t = jnp.arange(seq)
    ang_row = (t // GRID_W).astype(F32)[:, None] * freqs[None, :]
    ang_col = (t % GRID_W).astype(F32)[:, None] * freqs[None, :]

    def axis_tables(ang):
        c, s = jnp.cos(ang), jnp.sin(ang)
        return jnp.concatenate([c, c], axis=-1), jnp.concatenate([-s, s], axis=-1)

    cr, sr = axis_tables(ang_row)
    cc, sc = axis_tables(ang_col)
    cos = jnp.concatenate([cr, cc], axis=-1)
    sin = jnp.concatenate([sr, sc], axis=-1)
    reps = LANES // HEAD_DIM
    return jnp.tile(cos, (1, reps)), jnp.tile(sin, (1, reps))


def _t5_bucket_table():
    import math
    rel = jnp.arange(3 * Q_BLOCK)[None, :] - Q_BLOCK - jnp.arange(Q_BLOCK)[:, None]
    nb = N_REL_BUCKETS // 2
    max_exact = nb // 2
    ret = jnp.where(rel > 0, nb, 0)
    n = jnp.abs(rel)
    large = max_exact + (jnp.log(jnp.maximum(n, 1).astype(F32) / max_exact)
                         / math.log(REL_MAX_DIST / max_exact) * (nb - max_exact)).astype(I32)
    large = jnp.minimum(large, nb - 1)
    return (ret + jnp.where(n < max_exact, n, large)).astype(I32)


def _tile_tables(counts, tmx, n_tiles):
    tiles_per_class = (counts + tmx - 1) // tmx
    tile_end = jnp.cumsum(tiles_per_class)
    row_start = ((tile_end - tiles_per_class) * tmx).astype(I32)
    n_used = tile_end[-1].astype(I32)
    tile_id = jnp.minimum(jnp.arange(n_tiles, dtype=I32), jnp.maximum(n_used - 1, 0))
    tile_cls = jnp.minimum(jnp.sum(tile_id[:, None] >= tile_end[None, :], axis=1), N_CLASSES - 1).astype(I32)
    group = tile_cls // PAIRS_PER_GROUP
    pair = tile_cls % PAIRS_PER_GROUP
    tile_ea = group * EXPERTS_PER_GROUP + jnp.asarray(PAIR_LO, I32)[pair]
    tile_eb = group * EXPERTS_PER_GROUP + jnp.asarray(PAIR_HI, I32)[pair]
    return row_start, tile_ea.astype(I32), tile_eb.astype(I32), n_used.reshape(1)


def _tile(n, pref):
    t = min(n, pref)
    assert n % t == 0, (n, pref)
    return t


def kernel(x, ln_mix, w_qkv, q_norm, k_norm, w_o, rel_bias, sinks, ln_ffn, w_group, w_expert, w1, w3, w2):
    B, S, D = x.shape
    depth = w_qkv.shape[0]
    T = B * S
    assert S % GRID_W == 0 and S % Q_BLOCK == 0 and D == N_HEADS * HEAD_DIM

    tm = _tile(S, 512)
    tq_flash = _tile(S, 1024)
    tk_flash = _tile(S, 512)
    tq_win = _tile(S, 1024)
    ts = _tile(T, 2048)
    tc = _tile(T, 1024)
    tmx = _tile(T, 256)
    n_tiles = T // tmx + N_CLASSES
    rows_pad = n_tiles * tmx

    cos, sin = _rope_tables(S)
    bias = _bias_call(rel_bias.astype(F32), _t5_bucket_table())

    w_qkv_b = w_qkv.astype(BF16)
    w_o_b = w_o.astype(BF16)
    w1_b, w3_b, w2_b = w1.astype(BF16), w3.astype(BF16), w2.astype(BF16)
    reps = MXU_DIM // HEAD_DIM
    router_f32 = jnp.concatenate(
        [w_group, w_expert,
         jnp.zeros((depth, D, LANES - N_GROUPS - N_EXPERTS), F32)], axis=-1).astype(F32)
    router_hi = router_f32.astype(BF16)
    router_lo = (router_f32 - router_hi.astype(F32)).astype(BF16)
    router = jnp.concatenate([router_hi, router_lo], axis=-1)

    x2 = x.reshape(T, D).astype(F32)
    sorted_rows = jnp.zeros((rows_pad * SUBLANES, LANES), U32)
    for i in range(depth):
        q, k, v = _qkv_call(
            x2, ln_mix[i].reshape(1, D).astype(F32), w_qkv_b[i],
            jnp.tile(q_norm[i].astype(F32), reps).reshape(1, MXU_DIM),
            jnp.tile(k_norm[i].astype(F32), reps).reshape(1, MXU_DIM),
            cos, sin, batch=B, seq=S, rope=(i % N_MIXERS == 0), tm=tm)
        if i % N_MIXERS == 0:
            o = _flash_call(q, k, v, tq=tq_flash, tk=tk_flash)
        else:
            o = _window_call(sinks[i // N_MIXERS].astype(F32), q, k, v, bias, tq=tq_win)
        xn, rows, cls, rank, counts = _post_call(
            x2, o.reshape(T, D), w_o_b[i], ln_ffn[i].reshape(1, D).astype(F32), router[i], tm=tm)

        row_start, tile_ea, tile_eb, n_used = _tile_tables(
            counts[0, :N_CLASSES].astype(I32), tmx, n_tiles)
        pos = _pos_call(row_start, cls.reshape(T // LANES, LANES), rank.reshape(T // LANES, LANES))
        sorted_rows = _dispatch_call(pos.reshape(T // ts, 1, ts), rows, sorted_rows, ts=ts)
        y = _moe_call(tile_ea, tile_eb, n_used, sorted_rows, w1_b, w3_b, w2_b, layer=i, tmx=tmx)
        x2 = _combine_call(pos.reshape(T // tc, 1, tc), xn, y, tc=tc)
    return x2.reshape(B, S, D).astype(x.dtype)
```

```python
import functools

import jax
import jax.numpy as jnp
from jax import lax
from jax.experimental import pallas as pl
from jax.experimental.pallas import tpu as pltpu

N_HEADS = 16
N_KV_HEADS = 4
HEAD_DIM = 64
KV_GROUP = N_HEADS // N_KV_HEADS
SCALE = HEAD_DIM ** -0.5
GRID_W = 64
ROPE_THETA = 10000.0
ROPE_AXIS_DIM = HEAD_DIM // 2
Q_BLOCK = 128
WINDOW = 128
N_MIXERS = 2
N_REL_BUCKETS = 32
REL_MAX_DIST = 128
N_GROUPS = 4
EXPERTS_PER_GROUP = 4
N_EXPERTS = N_GROUPS * EXPERTS_PER_GROUP
NORM_EPS = 1e-6
NEG_INF = -1e30
LOG2_E = 1.4426950408889634
V_ROWS = HEAD_DIM + 16

PAIRS_PER_GROUP = 6
N_CLASSES = N_GROUPS * PAIRS_PER_GROUP
PAIR_LO = (0, 0, 0, 1, 1, 2)
PAIR_HI = (1, 2, 3, 2, 3, 3)

LANES = 128
SUBLANES = 8
MXU_DIM = 256
VMEM_LIMIT = 48 * 1024 * 1024

F32 = jnp.float32
BF16 = jnp.bfloat16
U32 = jnp.uint32
I32 = jnp.int32


def _params(sem, flags=None):
    return pltpu.CompilerParams(dimension_semantics=sem, vmem_limit_bytes=VMEM_LIMIT, flags=flags)


ROW_DMA_UNROLL = 8


def _subl(c, n_rows):
    return pl.ds(c, n_rows, stride=SUBLANES)


def _row_tile(ref, row):
    return ref.at[pl.ds(pl.multiple_of(row * SUBLANES, SUBLANES), SUBLANES)]


def _issue_row_copies(n_rows, make_copy):
    def group(g, carry):
        for u in range(ROW_DMA_UNROLL):
            make_copy(g * ROW_DMA_UNROLL + u).start(priority=u % 2)
        return carry

    lax.fori_loop(0, n_rows // ROW_DMA_UNROLL, group, 0)


def _gathered_rows(pos_ref, pos_next_ref, y_ref, buf_ref, sem, n_rows):
    i = pl.program_id(0)
    slot = i % 2

    def gather(p_ref, s):
        _issue_row_copies(
            n_rows, lambda r: pltpu.make_async_copy(
                _row_tile(y_ref, p_ref[0, 0, r]), _row_tile(buf_ref.at[s], r), sem.at[s]))

    @pl.when(i == 0)
    def _():
        gather(pos_ref, 0)

    @pl.when(i + 1 < pl.num_programs(0))
    def _():
        gather(pos_next_ref, 1 - slot)

    pltpu.make_async_copy(y_ref.at[pl.ds(0, n_rows * SUBLANES)], buf_ref.at[slot], sem.at[slot]).wait()
    return slot


def _segment_sum_matrix():
    r = lax.broadcasted_iota(I32, (MXU_DIM, MXU_DIM), 0) // HEAD_DIM
    c = lax.broadcasted_iota(I32, (MXU_DIM, MXU_DIM), 1) // HEAD_DIM
    return (r == c).astype(BF16)


def _head_rmsnorm(t, gain, seg):
    outs = []
    for c in range(t.shape[1] // MXU_DIM):
        tc = t[:, c * MXU_DIM:(c + 1) * MXU_DIM]
        sq = tc * tc
        hi = sq.astype(BF16)
        lo = (sq - hi.astype(F32)).astype(BF16)
        ss = (jnp.dot(hi, seg, preferred_element_type=F32)
              + jnp.dot(lo, seg, preferred_element_type=F32))
        outs.append(tc * lax.rsqrt(ss * (1.0 / HEAD_DIM) + NORM_EPS) * gain)
    return outs


def _rope(chunks, cos, sin):
    lane = lax.broadcasted_iota(I32, cos.shape, 1)
    upper = (lane & (ROPE_AXIS_DIM // 2)) != 0
    outs = []
    for tc in chunks:
        halves = []
        for j in range(MXU_DIM // LANES):
            xc = tc[:, j * LANES:(j + 1) * LANES]
            partner = jnp.where(upper,
                                pltpu.roll(xc, ROPE_AXIS_DIM // 2, 1),
                                pltpu.roll(xc, LANES - ROPE_AXIS_DIM // 2, 1))
            halves.append(xc * cos + partner * sin)
        outs.append(jnp.concatenate(halves, axis=-1))
    return outs


def _qkv_kernel(*refs, rope, combine):
    if combine:
        (pos_ref, pos_next_ref, x_ref, y_ref, g_ref, w_ref, qg_ref, kg_ref, cos_ref, sin_ref,
         q_ref, k_ref, v_ref, xo_ref, buf_ref, sem) = refs
        tm = x_ref.shape[0]
        slot = _gathered_rows(pos_ref, pos_next_ref, y_ref, buf_ref, sem, tm)
        x = jnp.concatenate(
            [x_ref[:, c * LANES:(c + 1) * LANES] + buf_ref[slot, _subl(c, tm), :]
             for c in range(SUBLANES)], axis=-1)
        xo_ref[...] = x
    else:
        x_ref, g_ref, w_ref, qg_ref, kg_ref, cos_ref, sin_ref, q_ref, k_ref, v_ref = refs
        x = x_ref[...]
    ms = jnp.mean(x * x, axis=-1, keepdims=True)
    h = (x * lax.rsqrt(ms + NORM_EPS) * g_ref[...]).astype(BF16)
    qkv = jnp.dot(h, w_ref[...], preferred_element_type=F32)
    nq = N_HEADS * HEAD_DIM
    nk = N_KV_HEADS * HEAD_DIM
    seg = _segment_sum_matrix()
    qs = _head_rmsnorm(qkv[:, :nq], qg_ref[...], seg)
    ks = _head_rmsnorm(qkv[:, nq:nq + nk], kg_ref[...], seg)
    if rope:
        cos = cos_ref[...]
        sin = sin_ref[...]
        qs = _rope(qs, cos, sin)
        ks = _rope(ks, cos, sin)
    heads_per_chunk = MXU_DIM // HEAD_DIM
    v = qkv[:, nq + nk:]
    for c, kc in enumerate(ks):
        for j in range(heads_per_chunk):
            k_ref[0, c * heads_per_chunk + j] = kc[:, j * HEAD_DIM:(j + 1) * HEAD_DIM].astype(BF16)
    for c, qc in enumerate(qs):
        qt = (qc * (SCALE * LOG2_E)).T
        for j in range(heads_per_chunk):
            q_ref[0, c * heads_per_chunk + j] = qt[j * HEAD_DIM:(j + 1) * HEAD_DIM].astype(BF16)
    vt = v.T
    ones = jnp.ones((V_ROWS - HEAD_DIM, vt.shape[1]), F32)
    for j in range(N_KV_HEADS):
        v_ref[0, j] = jnp.concatenate([vt[j * HEAD_DIM:(j + 1) * HEAD_DIM], ones], axis=0).astype(BF16)


def _qkv_call(x2, gain, w, qg, kg, cos, sin, *, batch, seq, rope, tm, combine=None):
    T, D = x2.shape
    spt = seq // tm
    n = T // tm
    qkv_dim = w.shape[1]
    kern = functools.partial(_qkv_kernel, rope=rope, combine=combine is not None)
    head_map = lambda i: (i // spt, 0, i % spt, 0)
    feat_map = lambda i: (i // spt, 0, 0, i % spt)
    in_specs = [
        pl.BlockSpec((tm, D), lambda i: (i, 0)),
        pl.BlockSpec((1, D), lambda i: (0, 0)),
        pl.BlockSpec((D, qkv_dim), lambda i: (0, 0)),
        pl.BlockSpec((1, MXU_DIM), lambda i: (0, 0)),
        pl.BlockSpec((1, MXU_DIM), lambda i: (0, 0)),
        pl.BlockSpec((tm, LANES), lambda i: (i % spt, 0)),
        pl.BlockSpec((tm, LANES), lambda i: (i % spt, 0)),
    ]
    out_specs = [
        pl.BlockSpec((1, N_HEADS, HEAD_DIM, tm), feat_map),
        pl.BlockSpec((1, N_KV_HEADS, tm, HEAD_DIM), head_map),
        pl.BlockSpec((1, N_KV_HEADS, V_ROWS, tm), feat_map),
    ]
    out_shape = [
        jax.ShapeDtypeStruct((batch, N_HEADS, HEAD_DIM, seq), BF16),
        jax.ShapeDtypeStruct((batch, N_KV_HEADS, seq, HEAD_DIM), BF16),
        jax.ShapeDtypeStruct((batch, N_KV_HEADS, V_ROWS, seq), BF16),
    ]
    args = [x2, gain, w, qg, kg, cos, sin]
    scratch = []
    if combine is not None:
        pos, y = combine
        pos3 = pos.reshape(n, 1, tm)
        in_specs = [
            pl.BlockSpec((1, 1, tm), lambda i: (i, 0, 0), memory_space=pltpu.SMEM),
            pl.BlockSpec((1, 1, tm), lambda i: (jnp.minimum(i + 1, n - 1), 0, 0), memory_space=pltpu.SMEM),
            in_specs[0],
            pl.BlockSpec(memory_space=pl.ANY),
        ] + in_specs[1:]
        out_specs.append(pl.BlockSpec((tm, D), lambda i: (i, 0)))
        out_shape.append(jax.ShapeDtypeStruct((T, D), F32))
        args = [pos3, pos3, x2, y] + args[1:]
        scratch = [pltpu.VMEM((2, tm * SUBLANES, LANES), F32), pltpu.SemaphoreType.DMA((2,))]
    return pl.pallas_call(
        kern,
        grid=(n,),
        in_specs=in_specs,
        out_specs=out_specs,
        out_shape=out_shape,
        scratch_shapes=scratch,
        compiler_params=_params(("arbitrary",)),
        name="qkv_proj",
    )(*args)


FLASH_RING = 2


def _heads_to_rows(o, width):
    pairs = []
    for g in range(0, KV_GROUP, 2):
        two_heads = jnp.concatenate(
            [o[:, g * width:(g + 1) * width], o[:, (g + 1) * width:(g + 2) * width]], axis=0)
        pairs.append(two_heads.T)
    return jnp.concatenate(pairs, axis=-1)


def _flash_kernel(qt_ref, k_ref, vt_ref, o_ref, acc_ref, *s_refs, tk):
    ring = len(s_refs)
    tq = qt_ref.shape[3]
    seq = k_ref.shape[2]
    n_chunks = seq // tk
    key_tiles = tk // MXU_DIM
    units = [(g, g * tq + h * MXU_DIM, h * MXU_DIM) for g in range(KV_GROUP) for h in range(tq // MXU_DIM)]

    def tile(c0, kt):
        return (slice(kt * MXU_DIM, (kt + 1) * MXU_DIM), slice(c0, c0 + MXU_DIM))

    def qk_unit(chunk, unit, dst_ref):
        g, c0, q0 = unit
        mx = None
        for kt in range(key_tiles):
            off = pl.multiple_of(chunk * tk + kt * MXU_DIM, MXU_DIM)
            s = jnp.dot(k_ref[0, 0, pl.ds(off, MXU_DIM), :], qt_ref[0, g, :, q0:q0 + MXU_DIM],
                        preferred_element_type=F32)
            dst_ref[tile(c0, kt)] = s
            t = jnp.max(s, axis=0, keepdims=True)
            mx = t if mx is None else jnp.maximum(mx, t)
        return mx

    def attend_unit(chunk, unit, src_ref, mx, m):
        _, c0, _ = unit
        m_new = jnp.maximum(m, mx)
        pv = None
        for kt in range(key_tiles):
            off = pl.multiple_of(chunk * tk + kt * MXU_DIM, MXU_DIM)
            p = jnp.exp2(src_ref[tile(c0, kt)] - m_new).astype(BF16)
            d = jnp.dot(vt_ref[0, 0, :, pl.ds(off, MXU_DIM)], p, preferred_element_type=F32)
            pv = d if pv is None else pv + d
        cols = slice(c0, c0 + MXU_DIM)
        acc_ref[:, cols] = jnp.exp2(m - m_new) * acc_ref[:, cols] + pv
        return m_new

    def step(chunk, u, mx, m, last=False):
        mx_next, m_out = [], []
        for i, unit in enumerate(units):
            if not last:
                mx_next.append(qk_unit(chunk + 1, unit, s_refs[(u + 1) % ring]))
            m_out.append(attend_unit(chunk, unit, s_refs[u], mx[i], m[i]))
        return tuple(mx_next), tuple(m_out)

    def body(j, carry):
        mx, m = carry
        for u in range(ring):
            mx, m = step(ring * j + u, u, mx, m)
        return mx, m

    acc_ref[...] = jnp.zeros_like(acc_ref)
    m = tuple(jnp.full((1, MXU_DIM), NEG_INF, F32) for _ in units)
    mx = tuple(qk_unit(0, unit, s_refs[0]) for unit in units)
    loops = (n_chunks - 1) // ring
    mx, m = lax.fori_loop(0, loops, body, (mx, m))
    for c in range(loops * ring, n_chunks - 1):
        mx, m = step(c, c % ring, mx, m)
    step(n_chunks - 1, (n_chunks - 1) % ring, mx, m, last=True)
    acc = acc_ref[...]
    o_ref[0] = _heads_to_rows(acc[:HEAD_DIM] / acc[HEAD_DIM:HEAD_DIM + 1], tq).astype(BF16)


def _flash_call(q, k, v, *, tq, tk):
    B, _, _, S = q.shape
    kern = functools.partial(_flash_kernel, tk=tk)
    return pl.pallas_call(
        kern,
        grid=(B, N_KV_HEADS, S // tq),
        in_specs=[
            pl.BlockSpec((1, KV_GROUP, HEAD_DIM, tq), lambda b, h, i: (b, h, 0, i)),
            pl.BlockSpec((1, 1, S, HEAD_DIM), lambda b, h, i: (b, h, 0, 0)),
            pl.BlockSpec((1, 1, V_ROWS, S), lambda b, h, i: (b, h, 0, 0)),
        ],
        out_specs=pl.BlockSpec((1, tq, KV_GROUP * HEAD_DIM), lambda b, h, i: (b, i, h)),
        out_shape=jax.ShapeDtypeStruct((B, S, N_HEADS * HEAD_DIM), BF16),
        scratch_shapes=([pltpu.VMEM((V_ROWS, KV_GROUP * tq), F32)]
                        + [pltpu.VMEM((tk, KV_GROUP * tq), F32)] * FLASH_RING),
        compiler_params=_params(("parallel", "parallel", "parallel")),
        name="flash_global",
    )(q, k, v)


EDGE_VARIANTS = 4


def _bias_kernel(rb_ref, bucket_ref, out_ref):
    h = pl.program_id(0)
    b = bucket_ref[...]
    bias = jnp.zeros(b.shape, F32)
    for j in range(N_REL_BUCKETS):
        bias = jnp.where(b == j, rb_ref[h, j], bias)
    qpos = lax.broadcasted_iota(I32, b.shape, 0)
    col = lax.broadcasted_iota(I32, b.shape, 1)
    in_band = jnp.abs(col - Q_BLOCK - qpos) <= WINDOW
    for e in range(EDGE_VARIANTS):
        valid = in_band
        if e & 1:
            valid = valid & (col >= Q_BLOCK)
        if e & 2:
            valid = valid & (col < 2 * Q_BLOCK)
        out_ref[e, 0] = (jnp.where(valid, bias, NEG_INF) * LOG2_E).T


def _bias_call(rel_bias, bucket):
    table = (bucket.shape[1], bucket.shape[0])
    return pl.pallas_call(
        _bias_kernel,
        grid=(N_HEADS,),
        in_specs=[
            pl.BlockSpec(memory_space=pltpu.SMEM),
            pl.BlockSpec(bucket.shape, lambda h: (0, 0)),
        ],
        out_specs=pl.BlockSpec((EDGE_VARIANTS, 1) + table, lambda h: (0, h, 0, 0)),
        out_shape=jax.ShapeDtypeStruct((EDGE_VARIANTS, N_HEADS) + table, F32),
        compiler_params=_params(("parallel",)),
        name="rel_bias_table",
    )(rel_bias, bucket)


def _window_kernel(sink_ref, qt_ref, k_ref, vt_ref, bias_ref, o_ref, *s_refs, blocks_per_tile):
    hk = pl.program_id(1)
    tile = pl.program_id(2)
    nb = k_ref.shape[2] // Q_BLOCK
    ring = len(s_refs)
    pairs = range(KV_GROUP // 2)

    sinks = []
    for p in pairs:
        first = lax.broadcasted_iota(I32, (1, MXU_DIM), 1) < Q_BLOCK
        sinks.append(jnp.where(first, sink_ref[hk * KV_GROUP + 2 * p],
                               sink_ref[hk * KV_GROUP + 2 * p + 1]) * LOG2_E)

    def key_starts(j):
        n = tile * blocks_per_tile + j
        return n, [pl.multiple_of(jnp.maximum(n - 1, 0) * Q_BLOCK, Q_BLOCK),
                   pl.multiple_of(n * Q_BLOCK, Q_BLOCK),
                   pl.multiple_of(jnp.minimum(n + 1, nb - 1) * Q_BLOCK, Q_BLOCK)]

    def qk_block(j, dst_ref):
        n, starts = key_starts(j)
        kcat = jnp.concatenate([k_ref[0, 0, pl.ds(s, Q_BLOCK), :] for s in starts], axis=0)
        edge = (n == 0).astype(I32) + 2 * (n == nb - 1).astype(I32)
        maxes = []
        for p in pairs:
            heads = (2 * p, 2 * p + 1)
            qt = jnp.concatenate(
                [qt_ref[0, h, :, j * Q_BLOCK:(j + 1) * Q_BLOCK] for h in heads], axis=-1)
            bias = jnp.concatenate([bias_ref[edge, h] for h in heads], axis=-1)
            s = jnp.dot(kcat, qt, preferred_element_type=F32) + bias
            dst_ref[:, p * MXU_DIM:(p + 1) * MXU_DIM] = s
            maxes.append(jnp.max(s, axis=0, keepdims=True))
        return maxes

    def attend_block(j, src_ref, maxes):
        _, starts = key_starts(j)
        vcat = jnp.concatenate([vt_ref[0, 0, :, pl.ds(s, Q_BLOCK)] for s in starts], axis=-1)
        for p in pairs:
            m = jnp.maximum(maxes[p], sinks[p])
            pr = jnp.exp2(src_ref[:, p * MXU_DIM:(p + 1) * MXU_DIM] - m).astype(BF16)
            ov = jnp.dot(vcat, pr, preferred_element_type=F32)
            o = ov[:HEAD_DIM] / (ov[HEAD_DIM:HEAD_DIM + 1] + jnp.exp2(sinks[p] - m))
            two_heads = jnp.concatenate([o[:, :Q_BLOCK], o[:, Q_BLOCK:]], axis=0)
            o_ref[0, j * Q_BLOCK:(j + 1) * Q_BLOCK, p * LANES:(p + 1) * LANES] = two_heads.T.astype(BF16)

    maxes = qk_block(0, s_refs[0])
    for j in range(blocks_per_tile):
        nxt = qk_block(j + 1, s_refs[(j + 1) % ring]) if j + 1 < blocks_per_tile else None
        attend_block(j, s_refs[j % ring], maxes)
        maxes = nxt


def _window_call(sink, q, k, v, bias, *, tq):
    B, _, _, S = q.shape
    kern = functools.partial(_window_kernel, blocks_per_tile=tq // Q_BLOCK)
    return pl.pallas_call(
        kern,
        grid=(B, N_KV_HEADS, S // tq),
        in_specs=[
            pl.BlockSpec(memory_space=pltpu.SMEM),
            pl.BlockSpec((1, KV_GROUP, HEAD_DIM, tq), lambda b, h, i: (b, h, 0, i)),
            pl.BlockSpec((1, 1, S, HEAD_DIM), lambda b, h, i: (b, h, 0, 0)),
            pl.BlockSpec((1, 1, V_ROWS, S), lambda b, h, i: (b, h, 0, 0)),
            pl.BlockSpec((EDGE_VARIANTS, KV_GROUP, 3 * Q_BLOCK, Q_BLOCK), lambda b, h, i: (0, h, 0, 0)),
        ],
        out_specs=pl.BlockSpec((1, tq, KV_GROUP * HEAD_DIM), lambda b, h, i: (b, i, h)),
        out_shape=jax.ShapeDtypeStruct((B, S, N_HEADS * HEAD_DIM), BF16),
        scratch_shapes=[pltpu.VMEM((3 * Q_BLOCK, KV_GROUP * Q_BLOCK), F32)] * 2,
        compiler_params=_params(("parallel", "parallel", "parallel")),
        name="window_attn",
    )(sink, q, k, v, bias)


def _f32_bits(x):
    return lax.bitcast_convert_type(x, U32)


def _pack_bf16_pair(lo_half, hi_half):
    lo = lax.shift_right_logical(_f32_bits(lo_half.astype(BF16).astype(F32)), jnp.uint32(16))
    hi = _f32_bits(hi_half.astype(BF16).astype(F32)) & jnp.uint32(0xFFFF0000)
    return lo | hi


def _unpack_bf16_pair(words):
    lo = lax.bitcast_convert_type(lax.shift_left(words, jnp.uint32(16)), F32)
    hi = lax.bitcast_convert_type(words & jnp.uint32(0xFFFF0000), F32)
    return lo, hi


def _column_to_lanes(col):
    tm = col.shape[0]
    wide = jnp.broadcast_to(col, (tm, LANES))
    return jnp.concatenate(
        [wide[b * LANES:(b + 1) * LANES].T[0:1] for b in range(tm // LANES)], axis=0)


def _post_kernel(x_ref, o_ref, wo_ref, g_ref, wr_ref,
                 xn_ref, row_ref, cls_ref, rank_ref, cnt_ref, run_ref):
    step = pl.program_id(0)
    tm = x_ref.shape[0]
    half = x_ref.shape[1] // 2

    @pl.when(step == 0)
    def _():
        run_ref[...] = jnp.zeros_like(run_ref)

    xn = x_ref[...] + jnp.dot(o_ref[...], wo_ref[...], preferred_element_type=F32)
    xn_ref[...] = xn
    ms = jnp.mean(xn * xn, axis=-1, keepdims=True)
    t = xn * lax.rsqrt(ms + NORM_EPS) * g_ref[...]
    t_hi = t.astype(BF16)
    t_lo = (t - t_hi.astype(F32)).astype(BF16)
    hi_both = jnp.dot(t_hi, wr_ref[...], preferred_element_type=F32)
    lo_hi = jnp.dot(t_lo, wr_ref[:, :LANES], preferred_element_type=F32)
    logits = hi_both[:, :LANES] + (hi_both[:, LANES:] + lo_hi)

    lane = lax.broadcasted_iota(I32, (tm, LANES), 1)
    lane_f = lane.astype(F32)
    big = jnp.float32(LANES)

    def first_argmax(vals):
        top = jnp.max(vals, axis=-1, keepdims=True)
        idx = jnp.min(jnp.where(vals == top, lane_f, big), axis=-1, keepdims=True)
        return top, idx

    is_group = lane < N_GROUPS
    g_top, g_idx = first_argmax(jnp.where(is_group, logits, -jnp.inf))
    g_sum = jnp.sum(jnp.where(is_group, jnp.exp(logits - g_top), 0.0), axis=-1, keepdims=True)
    g_prob = 1.0 / g_sum

    base = N_GROUPS + EXPERTS_PER_GROUP * g_idx
    in_group = (lane_f >= base) & (lane_f < base + EXPERTS_PER_GROUP)
    e_logits = jnp.where(in_group, logits, -jnp.inf)
    e1, i1 = first_argmax(e_logits)
    e2, i2 = first_argmax(jnp.where(lane_f == i1, -jnp.inf, e_logits))
    r = jnp.exp(e2 - e1)
    w_first = (1.0 / (1.0 + r)) * g_prob
    w_second = (r / (1.0 + r)) * g_prob

    j1 = i1 - base
    j2 = i2 - base
    first_is_lo = j1 < j2
    a = jnp.minimum(j1, j2)
    b = jnp.maximum(j1, j2)
    pair = a * 3.0 - a * (a - 1.0) * 0.5 + (b - a - 1.0)
    cls_f = g_idx * PAIRS_PER_GROUP + pair
    w_lo = jnp.where(first_is_lo, w_first, w_second)
    w_hi = jnp.where(first_is_lo, w_second, w_first)

    onehot = lane_f == cls_f
    before = (lax.broadcasted_iota(I32, (tm, tm), 0) > lax.broadcasted_iota(I32, (tm, tm), 1))
    earlier = jnp.dot(before.astype(BF16), onehot.astype(BF16), preferred_element_type=F32)
    rank = jnp.sum(jnp.where(onehot, earlier + run_ref[...], 0.0), axis=-1, keepdims=True)
    run_ref[...] += jnp.sum(onehot.astype(F32), axis=0, keepdims=True)
    cnt_ref[...] = run_ref[...]

    cls_ref[0] = _column_to_lanes(cls_f).astype(I32)
    rank_ref[0] = _column_to_lanes(rank).astype(I32)

    words = _pack_bf16_pair(t[:, :half], t[:, half:])
    x_subl = half // LANES
    for c in range(x_subl):
        row_ref[_subl(c, tm), :] = words[:, c * LANES:(c + 1) * LANES]
    weights = jnp.where(lane == 0, w_lo, jnp.where(lane == 1, w_hi, 0.0))
    row_ref[_subl(x_subl, tm), :] = _f32_bits(weights)
    for c in range(x_subl + 1, SUBLANES):
        row_ref[_subl(c, tm), :] = jnp.zeros((tm, LANES), U32)


def _post_call(x2, o2, wo, gain, wr, *, tm):
    T, D = x2.shape
    assert D // 2 + LANES <= SUBLANES * LANES
    return pl.pallas_call(
        _post_kernel,
        grid=(T // tm,),
        in_specs=[
            pl.BlockSpec((tm, D), lambda i: (i, 0)),
            pl.BlockSpec((tm, D), lambda i: (i, 0)),
            pl.BlockSpec((D, D), lambda i: (0, 0)),
            pl.BlockSpec((1, D), lambda i: (0, 0)),
            pl.BlockSpec((D, 2 * LANES), lambda i: (0, 0)),
        ],
        out_specs=[
            pl.BlockSpec((tm, D), lambda i: (i, 0)),
            pl.BlockSpec((tm * SUBLANES, LANES), lambda i: (i, 0)),
            pl.BlockSpec((1, tm // LANES, LANES), lambda i: (i, 0, 0)),
            pl.BlockSpec((1, tm // LANES, LANES), lambda i: (i, 0, 0)),
            pl.BlockSpec((1, LANES), lambda i: (0, 0)),
        ],
        out_shape=[
            jax.ShapeDtypeStruct((T, D), F32),
            jax.ShapeDtypeStruct((T * SUBLANES, LANES), U32),
            jax.ShapeDtypeStruct((T // tm, tm // LANES, LANES), I32),
            jax.ShapeDtypeStruct((T // tm, tm // LANES, LANES), I32),
            jax.ShapeDtypeStruct((1, LANES), F32),
        ],
        scratch_shapes=[pltpu.VMEM((1, LANES), F32)],
        compiler_params=_params(("arbitrary",)),
        name="post_attn_router",
    )(x2, o2, wo, gain, wr)


def _pos_kernel(start_ref, cls_ref, rank_ref, pos_ref):
    cls = cls_ref[...]
    base = jnp.zeros(cls.shape, I32)
    for c in range(N_CLASSES):
        base = jnp.where(cls == c, start_ref[c], base)
    pos_ref[...] = base + rank_ref[...]


def _pos_call(row_start, cls, rank):
    return pl.pallas_call(
        _pos_kernel,
        in_specs=[
            pl.BlockSpec(memory_space=pltpu.SMEM),
            pl.BlockSpec(memory_space=pltpu.VMEM),
            pl.BlockSpec(memory_space=pltpu.VMEM),
        ],
        out_specs=pl.BlockSpec(memory_space=pltpu.VMEM),
        out_shape=jax.ShapeDtypeStruct(cls.shape, I32),
        name="moe_positions",
    )(row_start, cls, rank)


def _dispatch_kernel(pos_ref, row_ref, init_ref, out_ref, sem):
    del init_ref
    ts = row_ref.shape[0] // SUBLANES
    _issue_row_copies(
        ts, lambda r: pltpu.make_async_copy(
            _row_tile(row_ref, r), _row_tile(out_ref, pos_ref[0, 0, r]), sem))
    pltpu.make_async_copy(row_ref, out_ref.at[pl.ds(0, ts * SUBLANES)], sem).wait()


def _dispatch_call(pos3, rows, init, *, ts):
    T = rows.shape[0] // SUBLANES
    return pl.pallas_call(
        _dispatch_kernel,
        grid=(T // ts,),
        in_specs=[
            pl.BlockSpec((1, 1, ts), lambda i: (i, 0, 0), memory_space=pltpu.SMEM),
            pl.BlockSpec((ts * SUBLANES, LANES), lambda i: (i, 0)),
            pl.BlockSpec(memory_space=pl.ANY),
        ],
        out_specs=pl.BlockSpec(memory_space=pl.ANY),
        out_shape=jax.ShapeDtypeStruct(init.shape, init.dtype),
        scratch_shapes=[pltpu.SemaphoreType.DMA(())],
        input_output_aliases={2: 0},
        compiler_params=_params(("arbitrary",)),
        name="moe_dispatch",
    )(pos3, rows, init)


def _moe_kernel(ea_ref, eb_ref, used_ref, row_ref, w1a_ref, w3a_ref, w2a_ref,
                w1b_ref, w3b_ref, w2b_ref, y_ref):
    del ea_ref, eb_ref
    i = pl.program_id(0)
    x_subl = w1a_ref.shape[2] // (2 * LANES)

    @pl.when(i < used_ref[0])
    def _():
        tmx = y_ref.shape[0] // SUBLANES
        words = jnp.concatenate([row_ref[_subl(c, tmx), :] for c in range(x_subl)], axis=-1)
        lo, hi = _unpack_bf16_pair(words)
        x = jnp.concatenate([lo, hi], axis=-1).astype(BF16)
        wts = lax.bitcast_convert_type(row_ref[_subl(x_subl, tmx), :], F32)
        w_lo = wts[:, 0:1]
        w_hi = wts[:, 1:2]

        def expert(w1_ref, w3_ref, w2_ref):
            z = jnp.dot(x, w1_ref[0, 0], preferred_element_type=F32)
            u = jnp.dot(x, w3_ref[0, 0], preferred_element_type=F32)
            h = (z * (1.0 / (1.0 + jnp.exp(-z)))) * u
            return jnp.dot(h.astype(BF16), w2_ref[0, 0], preferred_element_type=F32)

        y = (w_lo * expert(w1a_ref, w3a_ref, w2a_ref)
             + w_hi * expert(w1b_ref, w3b_ref, w2b_ref))
        for c in range(SUBLANES):
            y_ref[_subl(c, tmx), :] = y[:, c * LANES:(c + 1) * LANES]

    @pl.when(i >= used_ref[0])
    def _():
        y_ref[...] = jnp.zeros_like(y_ref)


def _moe_call(tile_ea, tile_eb, n_used, rows, w1, w3, w2, *, layer, tmx):
    rows_pad = rows.shape[0] // SUBLANES
    _, _, D, F = w1.shape
    assert D == SUBLANES * LANES
    wa = lambda i, ea, eb, used: (layer, ea[i], 0, 0)
    wb = lambda i, ea, eb, used: (layer, eb[i], 0, 0)
    grid_spec = pltpu.PrefetchScalarGridSpec(
        num_scalar_prefetch=3,
        grid=(rows_pad // tmx,),
        in_specs=[
            pl.BlockSpec((tmx * SUBLANES, LANES), lambda i, ea, eb, used: (i, 0)),
            pl.BlockSpec((1, 1, D, F), wa),
            pl.BlockSpec((1, 1, D, F), wa),
            pl.BlockSpec((1, 1, F, D), wa),
            pl.BlockSpec((1, 1, D, F), wb),
            pl.BlockSpec((1, 1, D, F), wb),
            pl.BlockSpec((1, 1, F, D), wb),
        ],
        out_specs=pl.BlockSpec((tmx * SUBLANES, LANES), lambda i, ea, eb, used: (i, 0)),
    )
    return pl.pallas_call(
        _moe_kernel,
        grid_spec=grid_spec,
        out_shape=jax.ShapeDtypeStruct((rows_pad * SUBLANES, LANES), F32),
        compiler_params=_params(("arbitrary",)),
        name="moe_experts",
    )(tile_ea, tile_eb, n_used, rows, w1, w3, w2, w1, w3, w2)


def _combine_kernel(pos_ref, pos_next_ref, x_ref, y_ref, out_ref, buf_ref, sem):
    tc = x_ref.shape[0]
    slot = _gathered_rows(pos_ref, pos_next_ref, y_ref, buf_ref, sem, tc)
    for c in range(SUBLANES):
        cols = slice(c * LANES, (c + 1) * LANES)
        out_ref[:, cols] = x_ref[:, cols] + buf_ref[slot, _subl(c, tc), :]


def _combine_call(pos3, xn, y, *, tc):
    T, D = xn.shape
    n = T // tc
    return pl.pallas_call(
        _combine_kernel,
        grid=(n,),
        in_specs=[
            pl.BlockSpec((1, 1, tc), lambda i: (i, 0, 0), memory_space=pltpu.SMEM),
            pl.BlockSpec((1, 1, tc), lambda i: (jnp.minimum(i + 1, n - 1), 0, 0), memory_space=pltpu.SMEM),
            pl.BlockSpec((tc, D), lambda i: (i, 0)),
            pl.BlockSpec(memory_space=pl.ANY),
        ],
        out_specs=pl.BlockSpec((tc, D), lambda i: (i, 0)),
        out_shape=jax.ShapeDtypeStruct((T, D), F32),
        scratch_shapes=[pltpu.VMEM((2, tc * SUBLANES, LANES), F32),
                        pltpu.SemaphoreType.DMA((2,))],
        compiler_params=_params(("arbitrary",)),
        name="moe_combine",
    )(pos3, pos3, xn, y)


def _rope_tables(seq):
    half = ROPE_AXIS_DIM // 2
    freqs = ROPE_THETA ** (-(jnp.arange(half, dtype=F32) * 2.0 / ROPE_AXIS_DIM))
    t = jnp.arange(seq)
    ang_row = (t // GRID_W).astype(F32)[:, None] * freqs[None, :]
    ang_col = (t % GRID_W).astype(F32)[:, None] * freqs[None, :]

    def axis_tables(ang):
        c, s = jnp.cos(ang), jnp.sin(ang)
        return jnp.concatenate([c, c], axis=-1), jnp.concatenate([-s, s], axis=-1)

    cr, sr = axis_tables(ang_row)
    cc, sc = axis_tables(ang_col)
    cos = jnp.concatenate([cr, cc], axis=-1)
    sin = jnp.concatenate([sr, sc], axis=-1)
    reps = LANES // HEAD_DIM
    return jnp.tile(cos, (1, reps)), jnp.tile(sin, (1, reps))


def _t5_bucket_table():
    import math
    rel = jnp.arange(3 * Q_BLOCK)[None, :] - Q_BLOCK - jnp.arange(Q_BLOCK)[:, None]
    nb = N_REL_BUCKETS // 2
    max_exact = nb // 2
    ret = jnp.where(rel > 0, nb, 0)
    n = jnp.abs(rel)
    large = max_exact + (jnp.log(jnp.maximum(n, 1).astype(F32) / max_exact)
                         / math.log(REL_MAX_DIST / max_exact) * (nb - max_exact)).astype(I32)
    large = jnp.minimum(large, nb - 1)
    return (ret + jnp.where(n < max_exact, n, large)).astype(I32)


def _tile_tables(counts, tmx, n_tiles):
    tiles_per_class = (counts + tmx - 1) // tmx
    tile_end = jnp.cumsum(tiles_per_class)
    row_start = ((tile_end - tiles_per_class) * tmx).astype(I32)
    n_used = tile_end[-1].astype(I32)
    tile_id = jnp.minimum(jnp.arange(n_tiles, dtype=I32), jnp.maximum(n_used - 1, 0))
    tile_cls = jnp.minimum(jnp.sum(tile_id[:, None] >= tile_end[None, :], axis=1), N_CLASSES - 1).astype(I32)
    group = tile_cls // PAIRS_PER_GROUP
    pair = tile_cls % PAIRS_PER_GROUP
    tile_ea = group * EXPERTS_PER_GROUP + jnp.asarray(PAIR_LO, I32)[pair]
    tile_eb = group * EXPERTS_PER_GROUP + jnp.asarray(PAIR_HI, I32)[pair]
    return row_start, tile_ea.astype(I32), tile_eb.astype(I32), n_used.reshape(1)


def _tile(n, pref):
    t = min(n, pref)
    assert n % t == 0, (n, pref)
    return t


def kernel(x, ln_mix, w_qkv, q_norm, k_norm, w_o, rel_bias, sinks, ln_ffn, w_group, w_expert, w1, w3, w2):
    B, S, D = x.shape
    depth = w_qkv.shape[0]
    T = B * S
    assert S % GRID_W == 0 and S % Q_BLOCK == 0 and D == N_HEADS * HEAD_DIM

    tm = _tile(S, 512)
    tq_flash = _tile(S, 1024)
    tk_flash = _tile(S, 512)
    tq_win = _tile(S, 1024)
    ts = _tile(T, 2048)
    tc = _tile(T, 1024)
    tmx = _tile(T, 256)
    n_tiles = T // tmx + N_CLASSES
    rows_pad = n_tiles * tmx

    cos, sin = _rope_tables(S)
    bias = _bias_call(rel_bias.astype(F32), _t5_bucket_table())

    w_qkv_b = w_qkv.astype(BF16)
    w_o_b = w_o.astype(BF16)
    w1_b, w3_b, w2_b = w1.astype(BF16), w3.astype(BF16), w2.astype(BF16)
    reps = MXU_DIM // HEAD_DIM
    router_f32 = jnp.concatenate(
        [w_group, w_expert,
         jnp.zeros((depth, D, LANES - N_GROUPS - N_EXPERTS), F32)], axis=-1).astype(F32)
    router_hi = router_f32.astype(BF16)
    router_lo = (router_f32 - router_hi.astype(F32)).astype(BF16)
    router = jnp.concatenate([router_hi, router_lo], axis=-1)

    x2 = x.reshape(T, D).astype(F32)
    sorted_rows = jnp.zeros((rows_pad * SUBLANES, LANES), U32)
    pending = None
    for i in range(depth):
        outs = _qkv_call(
            x2, ln_mix[i].reshape(1, D).astype(F32), w_qkv_b[i],
            jnp.tile(q_norm[i].astype(F32), reps).reshape(1, MXU_DIM),
            jnp.tile(k_norm[i].astype(F32), reps).reshape(1, MXU_DIM),
            cos, sin, batch=B, seq=S, rope=(i % N_MIXERS == 0), tm=tm, combine=pending)
        q, k, v = outs[:3]
        if pending is not None:
            x2 = outs[3]
        if i % N_MIXERS == 0:
            o = _flash_call(q, k, v, tq=tq_flash, tk=tk_flash)
        else:
            o = _window_call(sinks[i // N_MIXERS].astype(F32), q, k, v, bias, tq=tq_win)
        x2, rows, cls, rank, counts = _post_call(
            x2, o.reshape(T, D), w_o_b[i], ln_ffn[i].reshape(1, D).astype(F32), router[i], tm=tm)

        row_start, tile_ea, tile_eb, n_used = _tile_tables(
            counts[0, :N_CLASSES].astype(I32), tmx, n_tiles)
        pos = _pos_call(row_start, cls.reshape(T // LANES, LANES), rank.reshape(T // LANES, LANES))
        sorted_rows = _dispatch_call(pos.reshape(T // ts, 1, ts), rows, sorted_rows, ts=ts)
        y = _moe_call(tile_ea, tile_eb, n_used, sorted_rows, w1_b, w3_b, w2_b, layer=i, tmx=tmx)
        pending = (pos, y)
    pos, y = pending
    x2 = _combine_call(pos.reshape(T // tc, 1, tc), x2, y, tc=tc)
    return x2.reshape(B, S, D).astype(x.dtype)
```

```python
import functools

import jax
import jax.numpy as jnp
from jax import lax
from jax.experimental import pallas as pl
from jax.experimental.pallas import tpu as pltpu

N_HEADS = 16
N_KV_HEADS = 4
HEAD_DIM = 64
KV_GROUP = N_HEADS // N_KV_HEADS
SCALE = HEAD_DIM ** -0.5
GRID_W = 64
ROPE_THETA = 10000.0
ROPE_AXIS_DIM = HEAD_DIM // 2
Q_BLOCK = 128
WINDOW = 128
N_MIXERS = 2
N_REL_BUCKETS = 32
REL_MAX_DIST = 128
N_GROUPS = 4
EXPERTS_PER_GROUP = 4
N_EXPERTS = N_GROUPS * EXPERTS_PER_GROUP
NORM_EPS = 1e-6
NEG_INF = -1e30
LOG2_E = 1.4426950408889634
V_ROWS = HEAD_DIM + 16

PAIRS_PER_GROUP = 6
N_CLASSES = N_GROUPS * PAIRS_PER_GROUP
PAIR_LO = (0, 0, 0, 1, 1, 2)
PAIR_HI = (1, 2, 3, 2, 3, 3)

LANES = 128
SUBLANES = 8
MXU_DIM = 256
VMEM_LIMIT = 48 * 1024 * 1024

F32 = jnp.float32
BF16 = jnp.bfloat16
U32 = jnp.uint32
I32 = jnp.int32


def _params(sem, flags=None):
    return pltpu.CompilerParams(dimension_semantics=sem, vmem_limit_bytes=VMEM_LIMIT, flags=flags)


ROW_DMA_UNROLL = 8


def _subl(c, n_rows):
    return pl.ds(c, n_rows, stride=SUBLANES)


def _row_tile(ref, row):
    return ref.at[pl.ds(pl.multiple_of(row * SUBLANES, SUBLANES), SUBLANES)]


def _issue_row_copies(n_rows, make_copy):
    def group(g, carry):
        for u in range(ROW_DMA_UNROLL):
            make_copy(g * ROW_DMA_UNROLL + u).start(priority=u % 2)
        return carry

    lax.fori_loop(0, n_rows // ROW_DMA_UNROLL, group, 0)


def _gathered_rows(pos_ref, pos_next_ref, y_ref, buf_ref, sem, n_rows):
    i = pl.program_id(0)
    slot = i % 2

    def gather(p_ref, s):
        _issue_row_copies(
            n_rows, lambda r: pltpu.make_async_copy(
                _row_tile(y_ref, p_ref[0, 0, r]), _row_tile(buf_ref.at[s], r), sem.at[s]))

    @pl.when(i == 0)
    def _():
        gather(pos_ref, 0)

    @pl.when(i + 1 < pl.num_programs(0))
    def _():
        gather(pos_next_ref, 1 - slot)

    pltpu.make_async_copy(y_ref.at[pl.ds(0, n_rows * SUBLANES)], buf_ref.at[slot], sem.at[slot]).wait()
    return slot


def _segment_sum_matrix():
    r = lax.broadcasted_iota(I32, (MXU_DIM, MXU_DIM), 0) // HEAD_DIM
    c = lax.broadcasted_iota(I32, (MXU_DIM, MXU_DIM), 1) // HEAD_DIM
    return (r == c).astype(BF16)


def _head_rmsnorm(t, gain, seg):
    outs = []
    for c in range(t.shape[1] // MXU_DIM):
        tc = t[:, c * MXU_DIM:(c + 1) * MXU_DIM]
        sq = tc * tc
        hi = sq.astype(BF16)
        lo = (sq - hi.astype(F32)).astype(BF16)
        ss = (jnp.dot(hi, seg, preferred_element_type=F32)
              + jnp.dot(lo, seg, preferred_element_type=F32))
        outs.append(tc * lax.rsqrt(ss * (1.0 / HEAD_DIM) + NORM_EPS) * gain)
    return outs


def _rope(chunks, cos, sin):
    lane = lax.broadcasted_iota(I32, cos.shape, 1)
    upper = (lane & (ROPE_AXIS_DIM // 2)) != 0
    outs = []
    for tc in chunks:
        halves = []
        for j in range(MXU_DIM // LANES):
            xc = tc[:, j * LANES:(j + 1) * LANES]
            partner = jnp.where(upper,
                                pltpu.roll(xc, ROPE_AXIS_DIM // 2, 1),
                                pltpu.roll(xc, LANES - ROPE_AXIS_DIM // 2, 1))
            halves.append(xc * cos + partner * sin)
        outs.append(jnp.concatenate(halves, axis=-1))
    return outs


def _qkv_kernel(*refs, rope, combine):
    if combine:
        (pos_ref, pos_next_ref, x_ref, y_ref, g_ref, w_ref, qg_ref, kg_ref, cos_ref, sin_ref,
         q_ref, k_ref, v_ref, xo_ref, buf_ref, sem) = refs
        tm = x_ref.shape[0]
        slot = _gathered_rows(pos_ref, pos_next_ref, y_ref, buf_ref, sem, tm)
        x = jnp.concatenate(
            [x_ref[:, c * LANES:(c + 1) * LANES] + buf_ref[slot, _subl(c, tm), :]
             for c in range(SUBLANES)], axis=-1)
        xo_ref[...] = x
    else:
        x_ref, g_ref, w_ref, qg_ref, kg_ref, cos_ref, sin_ref, q_ref, k_ref, v_ref = refs
        x = x_ref[...]
    ms = jnp.mean(x * x, axis=-1, keepdims=True)
    h = (x * lax.rsqrt(ms + NORM_EPS) * g_ref[...]).astype(BF16)
    qkv = jnp.dot(h, w_ref[...], preferred_element_type=F32)
    nq = N_HEADS * HEAD_DIM
    nk = N_KV_HEADS * HEAD_DIM
    seg = _segment_sum_matrix()
    qs = _head_rmsnorm(qkv[:, :nq], qg_ref[...], seg)
    ks = _head_rmsnorm(qkv[:, nq:nq + nk], kg_ref[...], seg)
    if rope:
        cos = cos_ref[...]
        sin = sin_ref[...]
        qs = _rope(qs, cos, sin)
        ks = _rope(ks, cos, sin)
    heads_per_chunk = MXU_DIM // HEAD_DIM
    v = qkv[:, nq + nk:]
    for c, kc in enumerate(ks):
        for j in range(heads_per_chunk):
            k_ref[0, c * heads_per_chunk + j] = kc[:, j * HEAD_DIM:(j + 1) * HEAD_DIM].astype(BF16)
    for c, qc in enumerate(qs):
        qt = (qc * (SCALE * LOG2_E)).T
        for j in range(heads_per_chunk):
            q_ref[0, c * heads_per_chunk + j] = qt[j * HEAD_DIM:(j + 1) * HEAD_DIM].astype(BF16)
    vt = v.T
    ones = jnp.ones((V_ROWS - HEAD_DIM, vt.shape[1]), F32)
    for j in range(N_KV_HEADS):
        v_ref[0, j] = jnp.concatenate([vt[j * HEAD_DIM:(j + 1) * HEAD_DIM], ones], axis=0).astype(BF16)


def _qkv_call(x2, gain, w, qg, kg, cos, sin, *, batch, seq, rope, tm, combine=None):
    T, D = x2.shape
    spt = seq // tm
    n = T // tm
    qkv_dim = w.shape[1]
    kern = functools.partial(_qkv_kernel, rope=rope, combine=combine is not None)
    head_map = lambda i: (i // spt, 0, i % spt, 0)
    feat_map = lambda i: (i // spt, 0, 0, i % spt)
    in_specs = [
        pl.BlockSpec((tm, D), lambda i: (i, 0)),
        pl.BlockSpec((1, D), lambda i: (0, 0)),
        pl.BlockSpec((D, qkv_dim), lambda i: (0, 0)),
        pl.BlockSpec((1, MXU_DIM), lambda i: (0, 0)),
        pl.BlockSpec((1, MXU_DIM), lambda i: (0, 0)),
        pl.BlockSpec((tm, LANES), lambda i: (i % spt, 0)),
        pl.BlockSpec((tm, LANES), lambda i: (i % spt, 0)),
    ]
    out_specs = [
        pl.BlockSpec((1, N_HEADS, HEAD_DIM, tm), feat_map),
        pl.BlockSpec((1, N_KV_HEADS, tm, HEAD_DIM), head_map),
        pl.BlockSpec((1, N_KV_HEADS, V_ROWS, tm), feat_map),
    ]
    out_shape = [
        jax.ShapeDtypeStruct((batch, N_HEADS, HEAD_DIM, seq), BF16),
        jax.ShapeDtypeStruct((batch, N_KV_HEADS, seq, HEAD_DIM), BF16),
        jax.ShapeDtypeStruct((batch, N_KV_HEADS, V_ROWS, seq), BF16),
    ]
    args = [x2, gain, w, qg, kg, cos, sin]
    scratch = []
    if combine is not None:
        pos, y = combine
        pos3 = pos.reshape(n, 1, tm)
        in_specs = [
            pl.BlockSpec((1, 1, tm), lambda i: (i, 0, 0), memory_space=pltpu.SMEM),
            pl.BlockSpec((1, 1, tm), lambda i: (jnp.minimum(i + 1, n - 1), 0, 0), memory_space=pltpu.SMEM),
            in_specs[0],
            pl.BlockSpec(memory_space=pl.ANY),
        ] + in_specs[1:]
        out_specs.append(pl.BlockSpec((tm, D), lambda i: (i, 0)))
        out_shape.append(jax.ShapeDtypeStruct((T, D), F32))
        args = [pos3, pos3, x2, y] + args[1:]
        scratch = [pltpu.VMEM((2, tm * SUBLANES, LANES), F32), pltpu.SemaphoreType.DMA((2,))]
    return pl.pallas_call(
        kern,
        grid=(n,),
        in_specs=in_specs,
        out_specs=out_specs,
        out_shape=out_shape,
        scratch_shapes=scratch,
        compiler_params=_params(("arbitrary",)),
        name="qkv_proj",
    )(*args)


FLASH_RING = 2


def _heads_to_rows(o, width):
    pairs = []
    for g in range(0, KV_GROUP, 2):
        two_heads = jnp.concatenate(
            [o[:, g * width:(g + 1) * width], o[:, (g + 1) * width:(g + 2) * width]], axis=0)
        pairs.append(two_heads.T)
    return jnp.concatenate(pairs, axis=-1)


def _flash_kernel(qt_ref, k_ref, vt_ref, o_ref, acc_ref, *s_refs, tk):
    ring = len(s_refs)
    tq = qt_ref.shape[3]
    seq = k_ref.shape[2]
    n_chunks = seq // tk
    key_tiles = tk // MXU_DIM
    units = [(g, g * tq + h * MXU_DIM, h * MXU_DIM) for g in range(KV_GROUP) for h in range(tq // MXU_DIM)]

    def tile(c0, kt):
        return (slice(kt * MXU_DIM, (kt + 1) * MXU_DIM), slice(c0, c0 + MXU_DIM))

    def qk_unit(chunk, unit, dst_ref):
        g, c0, q0 = unit
        mx = None
        for kt in range(key_tiles):
            off = pl.multiple_of(chunk * tk + kt * MXU_DIM, MXU_DIM)
            s = jnp.dot(k_ref[0, 0, pl.ds(off, MXU_DIM), :], qt_ref[0, g, :, q0:q0 + MXU_DIM],
                        preferred_element_type=F32)
            dst_ref[tile(c0, kt)] = s
            t = jnp.max(s, axis=0, keepdims=True)
            mx = t if mx is None else jnp.maximum(mx, t)
        return mx

    def attend_unit(chunk, unit, src_ref, mx, m):
        _, c0, _ = unit
        m_new = jnp.maximum(m, mx)
        pv = None
        for kt in range(key_tiles):
            off = pl.multiple_of(chunk * tk + kt * MXU_DIM, MXU_DIM)
            p = jnp.exp2(src_ref[tile(c0, kt)] - m_new).astype(BF16)
            d = jnp.dot(vt_ref[0, 0, :, pl.ds(off, MXU_DIM)], p, preferred_element_type=F32)
            pv = d if pv is None else pv + d
        cols = slice(c0, c0 + MXU_DIM)
        acc_ref[:, cols] = jnp.exp2(m - m_new) * acc_ref[:, cols] + pv
        return m_new

    def step(chunk, u, mx, m, last=False):
        mx_next, m_out = [], []
        for i, unit in enumerate(units):
            if not last:
                mx_next.append(qk_unit(chunk + 1, unit, s_refs[(u + 1) % ring]))
            m_out.append(attend_unit(chunk, unit, s_refs[u], mx[i], m[i]))
        return tuple(mx_next), tuple(m_out)

    def body(j, carry):
        mx, m = carry
        for u in range(ring):
            mx, m = step(ring * j + u, u, mx, m)
        return mx, m

    acc_ref[...] = jnp.zeros_like(acc_ref)
    m = tuple(jnp.full((1, MXU_DIM), NEG_INF, F32) for _ in units)
    mx = tuple(qk_unit(0, unit, s_refs[0]) for unit in units)
    loops = (n_chunks - 1) // ring
    mx, m = lax.fori_loop(0, loops, body, (mx, m))
    for c in range(loops * ring, n_chunks - 1):
        mx, m = step(c, c % ring, mx, m)
    step(n_chunks - 1, (n_chunks - 1) % ring, mx, m, last=True)
    acc = acc_ref[...]
    o_ref[0] = _heads_to_rows(acc[:HEAD_DIM] / acc[HEAD_DIM:HEAD_DIM + 1], tq).astype(BF16)


def _flash_call(q, k, v, *, tq, tk):
    B, _, _, S = q.shape
    kern = functools.partial(_flash_kernel, tk=tk)
    return pl.pallas_call(
        kern,
        grid=(B, N_KV_HEADS, S // tq),
        in_specs=[
            pl.BlockSpec((1, KV_GROUP, HEAD_DIM, tq), lambda b, h, i: (b, h, 0, i)),
            pl.BlockSpec((1, 1, S, HEAD_DIM), lambda b, h, i: (b, h, 0, 0)),
            pl.BlockSpec((1, 1, V_ROWS, S), lambda b, h, i: (b, h, 0, 0)),
        ],
        out_specs=pl.BlockSpec((1, tq, KV_GROUP * HEAD_DIM), lambda b, h, i: (b, i, h)),
        out_shape=jax.ShapeDtypeStruct((B, S, N_HEADS * HEAD_DIM), BF16),
        scratch_shapes=([pltpu.VMEM((V_ROWS, KV_GROUP * tq), F32)]
                        + [pltpu.VMEM((tk, KV_GROUP * tq), F32)] * FLASH_RING),
        compiler_params=_params(("parallel", "parallel", "parallel")),
        name="flash_global",
    )(q, k, v)


EDGE_VARIANTS = 4


def _bias_kernel(rb_ref, bucket_ref, out_ref):
    h = pl.program_id(0)
    b = bucket_ref[...]
    bias = jnp.zeros(b.shape, F32)
    for j in range(N_REL_BUCKETS):
        bias = jnp.where(b == j, rb_ref[h, j], bias)
    qpos = lax.broadcasted_iota(I32, b.shape, 0)
    col = lax.broadcasted_iota(I32, b.shape, 1)
    in_band = jnp.abs(col - Q_BLOCK - qpos) <= WINDOW
    for e in range(EDGE_VARIANTS):
        valid = in_band
        if e & 1:
            valid = valid & (col >= Q_BLOCK)
        if e & 2:
            valid = valid & (col < 2 * Q_BLOCK)
        out_ref[e, 0] = (jnp.where(valid, bias, NEG_INF) * LOG2_E).T


def _bias_call(rel_bias, bucket):
    table = (bucket.shape[1], bucket.shape[0])
    return pl.pallas_call(
        _bias_kernel,
        grid=(N_HEADS,),
        in_specs=[
            pl.BlockSpec(memory_space=pltpu.SMEM),
            pl.BlockSpec(bucket.shape, lambda h: (0, 0)),
        ],
        out_specs=pl.BlockSpec((EDGE_VARIANTS, 1) + table, lambda h: (0, h, 0, 0)),
        out_shape=jax.ShapeDtypeStruct((EDGE_VARIANTS, N_HEADS) + table, F32),
        compiler_params=_params(("parallel",)),
        name="rel_bias_table",
    )(rel_bias, bucket)


def _window_kernel(sink_ref, qt_ref, k_ref, vt_ref, bias_ref, o_ref, *s_refs, blocks_per_tile):
    hk = pl.program_id(1)
    tile = pl.program_id(2)
    nb = k_ref.shape[2] // Q_BLOCK
    ring = len(s_refs)
    pairs = range(KV_GROUP // 2)

    sinks = []
    for p in pairs:
        first = lax.broadcasted_iota(I32, (1, MXU_DIM), 1) < Q_BLOCK
        sinks.append(jnp.where(first, sink_ref[hk * KV_GROUP + 2 * p],
                               sink_ref[hk * KV_GROUP + 2 * p + 1]) * LOG2_E)

    def key_starts(j):
        n = tile * blocks_per_tile + j
        return n, [pl.multiple_of(jnp.maximum(n - 1, 0) * Q_BLOCK, Q_BLOCK),
                   pl.multiple_of(n * Q_BLOCK, Q_BLOCK),
                   pl.multiple_of(jnp.minimum(n + 1, nb - 1) * Q_BLOCK, Q_BLOCK)]

    def qk_block(j, dst_ref):
        n, starts = key_starts(j)
        kcat = jnp.concatenate([k_ref[0, 0, pl.ds(s, Q_BLOCK), :] for s in starts], axis=0)
        edge = (n == 0).astype(I32) + 2 * (n == nb - 1).astype(I32)
        maxes = []
        for p in pairs:
            heads = (2 * p, 2 * p + 1)
            qt = jnp.concatenate(
                [qt_ref[0, h, :, j * Q_BLOCK:(j + 1) * Q_BLOCK] for h in heads], axis=-1)
            bias = jnp.concatenate([bias_ref[edge, h] for h in heads], axis=-1)
            s = jnp.dot(kcat, qt, preferred_element_type=F32) + bias
            dst_ref[:, p * MXU_DIM:(p + 1) * MXU_DIM] = s
            maxes.append(jnp.max(s, axis=0, keepdims=True))
        return maxes

    def attend_block(j, src_ref, maxes):
        _, starts = key_starts(j)
        vcat = jnp.concatenate([vt_ref[0, 0, :, pl.ds(s, Q_BLOCK)] for s in starts], axis=-1)
        for p in pairs:
            m = jnp.maximum(maxes[p], sinks[p])
            pr = jnp.exp2(src_ref[:, p * MXU_DIM:(p + 1) * MXU_DIM] - m).astype(BF16)
            ov = jnp.dot(vcat, pr, preferred_element_type=F32)
            o = ov[:HEAD_DIM] / (ov[HEAD_DIM:HEAD_DIM + 1] + jnp.exp2(sinks[p] - m))
            two_heads = jnp.concatenate([o[:, :Q_BLOCK], o[:, Q_BLOCK:]], axis=0)
            o_ref[0, j * Q_BLOCK:(j + 1) * Q_BLOCK, p * LANES:(p + 1) * LANES] = two_heads.T.astype(BF16)

    maxes = qk_block(0, s_refs[0])
    for j in range(blocks_per_tile):
        nxt = qk_block(j + 1, s_refs[(j + 1) % ring]) if j + 1 < blocks_per_tile else None
        attend_block(j, s_refs[j % ring], maxes)
        maxes = nxt


def _window_call(sink, q, k, v, bias, *, tq):
    B, _, _, S = q.shape
    kern = functools.partial(_window_kernel, blocks_per_tile=tq // Q_BLOCK)
    return pl.pallas_call(
        kern,
        grid=(B, N_KV_HEADS, S // tq),
        in_specs=[
            pl.BlockSpec(memory_space=pltpu.SMEM),
            pl.BlockSpec((1, KV_GROUP, HEAD_DIM, tq), lambda b, h, i: (b, h, 0, i)),
            pl.BlockSpec((1, 1, S, HEAD_DIM), lambda b, h, i: (b, h, 0, 0)),
            pl.BlockSpec((1, 1, V_ROWS, S), lambda b, h, i: (b, h, 0, 0)),
            pl.BlockSpec((EDGE_VARIANTS, KV_GROUP, 3 * Q_BLOCK, Q_BLOCK), lambda b, h, i: (0, h, 0, 0)),
        ],
        out_specs=pl.BlockSpec((1, tq, KV_GROUP * HEAD_DIM), lambda b, h, i: (b, i, h)),
        out_shape=jax.ShapeDtypeStruct((B, S, N_HEADS * HEAD_DIM), BF16),
        scratch_shapes=[pltpu.VMEM((3 * Q_BLOCK, KV_GROUP * Q_BLOCK), F32)] * 2,
        compiler_params=_params(("parallel", "parallel", "parallel")),
        name="window_attn",
    )(sink, q, k, v, bias)


def _f32_bits(x):
    return lax.bitcast_convert_type(x, U32)


def _pack_bf16_pair(lo_half, hi_half):
    lo = lax.shift_right_logical(_f32_bits(lo_half.astype(BF16).astype(F32)), jnp.uint32(16))
    hi = _f32_bits(hi_half.astype(BF16).astype(F32)) & jnp.uint32(0xFFFF0000)
    return lo | hi


def _unpack_bf16_pair(words):
    lo = lax.bitcast_convert_type(lax.shift_left(words, jnp.uint32(16)), F32)
    hi = lax.bitcast_convert_type(words & jnp.uint32(0xFFFF0000), F32)
    return lo, hi


ROUTER_ROWS = 32


def _post_kernel(x_ref, o_ref, wo_ref, g_ref, wr_ref,
                 xn_ref, row_ref, cls_ref, rank_ref, cnt_ref, run_ref):
    step = pl.program_id(0)
    tm = x_ref.shape[0]
    half = x_ref.shape[1] // 2

    @pl.when(step == 0)
    def _():
        run_ref[...] = jnp.zeros_like(run_ref)

    xn = x_ref[...] + jnp.dot(o_ref[...], wo_ref[...], preferred_element_type=F32)
    xn_ref[...] = xn
    ms = jnp.mean(xn * xn, axis=-1, keepdims=True)
    t = xn * lax.rsqrt(ms + NORM_EPS) * g_ref[...]
    t_hi = t.astype(BF16)
    t_lo = (t - t_hi.astype(F32)).astype(BF16)
    hi_both = jnp.dot(t_hi, wr_ref[...], preferred_element_type=F32)
    lo_hi = jnp.dot(t_lo, wr_ref[:, :LANES], preferred_element_type=F32)
    logits = hi_both[:, :LANES] + (hi_both[:, LANES:] + lo_hi)

    lt = logits.T[:ROUTER_ROWS]
    row = lax.broadcasted_iota(I32, (ROUTER_ROWS, tm), 0).astype(F32)
    big = jnp.float32(ROUTER_ROWS)

    def first_argmax(vals):
        top = jnp.max(vals, axis=0, keepdims=True)
        idx = jnp.min(jnp.where(vals == top, row, big), axis=0, keepdims=True)
        return top, idx

    is_group = row < N_GROUPS
    g_top, g_idx = first_argmax(jnp.where(is_group, lt, -jnp.inf))
    g_sum = jnp.sum(jnp.where(is_group, jnp.exp(lt - g_top), 0.0), axis=0, keepdims=True)
    g_prob = 1.0 / g_sum

    base = N_GROUPS + EXPERTS_PER_GROUP * g_idx
    in_group = (row >= base) & (row < base + EXPERTS_PER_GROUP)
    e_logits = jnp.where(in_group, lt, -jnp.inf)
    e1, i1 = first_argmax(e_logits)
    e2, i2 = first_argmax(jnp.where(row == i1, -jnp.inf, e_logits))
    r = jnp.exp(e2 - e1)
    w_first = (1.0 / (1.0 + r)) * g_prob
    w_second = (r / (1.0 + r)) * g_prob

    j1 = i1 - base
    j2 = i2 - base
    first_is_lo = j1 < j2
    a = jnp.minimum(j1, j2)
    b = jnp.maximum(j1, j2)
    pair = a * 3.0 - a * (a - 1.0) * 0.5 + (b - a - 1.0)
    cls_f = g_idx * PAIRS_PER_GROUP + pair
    w_lo = jnp.where(first_is_lo, w_first, w_second)
    w_hi = jnp.where(first_is_lo, w_second, w_first)

    onehot = row == cls_f
    earlier_tok = (lax.broadcasted_iota(I32, (tm, tm), 0) < lax.broadcasted_iota(I32, (tm, tm), 1))
    earlier = jnp.dot(onehot.astype(BF16), earlier_tok.astype(BF16), preferred_element_type=F32)
    rank = jnp.sum(jnp.where(onehot, earlier + run_ref[:, 0:1], 0.0), axis=0, keepdims=True)
    run_ref[...] += jnp.sum(onehot.astype(F32), axis=1, keepdims=True)
    cnt_ref[...] = run_ref[...]

    def lanes_to_rows(v):
        return jnp.concatenate([v[:, c * LANES:(c + 1) * LANES] for c in range(tm // LANES)], axis=0)

    cls_ref[0] = lanes_to_rows(cls_f).astype(I32)
    rank_ref[0] = lanes_to_rows(rank).astype(I32)

    words = _pack_bf16_pair(t[:, :half], t[:, half:])
    x_subl = half // LANES
    for c in range(x_subl):
        row_ref[_subl(c, tm), :] = words[:, c * LANES:(c + 1) * LANES]
    weights = jnp.concatenate([w_lo, w_hi, jnp.zeros((LANES - 2, tm), F32)], axis=0).T
    row_ref[_subl(x_subl, tm), :] = _f32_bits(weights)
    for c in range(x_subl + 1, SUBLANES):
        row_ref[_subl(c, tm), :] = jnp.zeros((tm, LANES), U32)


def _post_call(x2, o2, wo, gain, wr, *, tm):
    T, D = x2.shape
    assert D // 2 + LANES <= SUBLANES * LANES
    return pl.pallas_call(
        _post_kernel,
        grid=(T // tm,),
        in_specs=[
            pl.BlockSpec((tm, D), lambda i: (i, 0)),
            pl.BlockSpec((tm, D), lambda i: (i, 0)),
            pl.BlockSpec((D, D), lambda i: (0, 0)),
            pl.BlockSpec((1, D), lambda i: (0, 0)),
            pl.BlockSpec((D, 2 * LANES), lambda i: (0, 0)),
        ],
        out_specs=[
            pl.BlockSpec((tm, D), lambda i: (i, 0)),
            pl.BlockSpec((tm * SUBLANES, LANES), lambda i: (i, 0)),
            pl.BlockSpec((1, tm // LANES, LANES), lambda i: (i, 0, 0)),
            pl.BlockSpec((1, tm // LANES, LANES), lambda i: (i, 0, 0)),
            pl.BlockSpec((ROUTER_ROWS, LANES), lambda i: (0, 0)),
        ],
        out_shape=[
            jax.ShapeDtypeStruct((T, D), F32),
            jax.ShapeDtypeStruct((T * SUBLANES, LANES), U32),
            jax.ShapeDtypeStruct((T // tm, tm // LANES, LANES), I32),
            jax.ShapeDtypeStruct((T // tm, tm // LANES, LANES), I32),
            jax.ShapeDtypeStruct((ROUTER_ROWS, LANES), F32),
        ],
        scratch_shapes=[pltpu.VMEM((ROUTER_ROWS, LANES), F32)],
        compiler_params=_params(("arbitrary",)),
        name="post_attn_router",
    )(x2, o2, wo, gain, wr)


def _pos_kernel(start_ref, cls_ref, rank_ref, pos_ref):
    cls = cls_ref[...]
    base = jnp.zeros(cls.shape, I32)
    for c in range(N_CLASSES):
        base = jnp.where(cls == c, start_ref[c], base)
    pos_ref[...] = base + rank_ref[...]


def _pos_call(row_start, cls, rank):
    return pl.pallas_call(
        _pos_kernel,
        in_specs=[
            pl.BlockSpec(memory_space=pltpu.SMEM),
            pl.BlockSpec(memory_space=pltpu.VMEM),
            pl.BlockSpec(memory_space=pltpu.VMEM),
        ],
        out_specs=pl.BlockSpec(memory_space=pltpu.VMEM),
        out_shape=jax.ShapeDtypeStruct(cls.shape, I32),
        name="moe_positions",
    )(row_start, cls, rank)


def _dispatch_kernel(pos_ref, row_ref, init_ref, out_ref, sem):
    del init_ref
    ts = row_ref.shape[0] // SUBLANES
    _issue_row_copies(
        ts, lambda r: pltpu.make_async_copy(
            _row_tile(row_ref, r), _row_tile(out_ref, pos_ref[0, 0, r]), sem))
    pltpu.make_async_copy(row_ref, out_ref.at[pl.ds(0, ts * SUBLANES)], sem).wait()


def _dispatch_call(pos3, rows, init, *, ts):
    T = rows.shape[0] // SUBLANES
    return pl.pallas_call(
        _dispatch_kernel,
        grid=(T // ts,),
        in_specs=[
            pl.BlockSpec((1, 1, ts), lambda i: (i, 0, 0), memory_space=pltpu.SMEM),
            pl.BlockSpec((ts * SUBLANES, LANES), lambda i: (i, 0)),
            pl.BlockSpec(memory_space=pl.ANY),
        ],
        out_specs=pl.BlockSpec(memory_space=pl.ANY),
        out_shape=jax.ShapeDtypeStruct(init.shape, init.dtype),
        scratch_shapes=[pltpu.SemaphoreType.DMA(())],
        input_output_aliases={2: 0},
        compiler_params=_params(("arbitrary",)),
        name="moe_dispatch",
    )(pos3, rows, init)


def _moe_kernel(ea_ref, eb_ref, used_ref, row_ref, w1a_ref, w3a_ref, w2a_ref,
                w1b_ref, w3b_ref, w2b_ref, y_ref):
    del ea_ref, eb_ref
    i = pl.program_id(0)
    x_subl = w1a_ref.shape[2] // (2 * LANES)

    @pl.when(i < used_ref[0])
    def _():
        tmx = y_ref.shape[0] // SUBLANES
        words = jnp.concatenate([row_ref[_subl(c, tmx), :] for c in range(x_subl)], axis=-1)
        lo, hi = _unpack_bf16_pair(words)
        x = jnp.concatenate([lo, hi], axis=-1).astype(BF16)
        wts = lax.bitcast_convert_type(row_ref[_subl(x_subl, tmx), :], F32)
        w_lo = wts[:, 0:1]
        w_hi = wts[:, 1:2]

        ff = w1a_ref.shape[3]

        def hidden(w1_ref, w3_ref):
            blocks = []
            for n0 in range(0, ff, MXU_DIM):
                z = jnp.dot(x, w1_ref[0, 0, :, n0:n0 + MXU_DIM], preferred_element_type=F32)
                u = jnp.dot(x, w3_ref[0, 0, :, n0:n0 + MXU_DIM], preferred_element_type=F32)
                blocks.append(((z * (1.0 / (1.0 + jnp.exp(-z)))) * u).astype(BF16))
            return jnp.concatenate(blocks, axis=-1)

        ha = hidden(w1a_ref, w3a_ref)
        hb = hidden(w1b_ref, w3b_ref)
        for n0 in range(0, SUBLANES * LANES, MXU_DIM):
            wide = slice(n0, n0 + MXU_DIM)
            y = (w_lo * jnp.dot(ha, w2a_ref[0, 0, :, wide], preferred_element_type=F32)
                 + w_hi * jnp.dot(hb, w2b_ref[0, 0, :, wide], preferred_element_type=F32))
            for c in range(MXU_DIM // LANES):
                y_ref[_subl(n0 // LANES + c, tmx), :] = y[:, c * LANES:(c + 1) * LANES]

    @pl.when(i >= used_ref[0])
    def _():
        y_ref[...] = jnp.zeros_like(y_ref)


def _moe_call(tile_ea, tile_eb, n_used, rows, w1, w3, w2, *, layer, tmx):
    rows_pad = rows.shape[0] // SUBLANES
    _, _, D, F = w1.shape
    assert D == SUBLANES * LANES
    wa = lambda i, ea, eb, used: (layer, ea[i], 0, 0)
    wb = lambda i, ea, eb, used: (layer, eb[i], 0, 0)
    grid_spec = pltpu.PrefetchScalarGridSpec(
        num_scalar_prefetch=3,
        grid=(rows_pad // tmx,),
        in_specs=[
            pl.BlockSpec((tmx * SUBLANES, LANES), lambda i, ea, eb, used: (i, 0)),
            pl.BlockSpec((1, 1, D, F), wa),
            pl.BlockSpec((1, 1, D, F), wa),
            pl.BlockSpec((1, 1, F, D), wa),
            pl.BlockSpec((1, 1, D, F), wb),
            pl.BlockSpec((1, 1, D, F), wb),
            pl.BlockSpec((1, 1, F, D), wb),
        ],
        out_specs=pl.BlockSpec((tmx * SUBLANES, LANES), lambda i, ea, eb, used: (i, 0)),
    )
    return pl.pallas_call(
        _moe_kernel,
        grid_spec=grid_spec,
        out_shape=jax.ShapeDtypeStruct((rows_pad * SUBLANES, LANES), F32),
        compiler_params=_params(("arbitrary",)),
        name="moe_experts",
    )(tile_ea, tile_eb, n_used, rows, w1, w3, w2, w1, w3, w2)


def _combine_kernel(pos_ref, pos_next_ref, x_ref, y_ref, out_ref, buf_ref, sem):
    tc = x_ref.shape[0]
    slot = _gathered_rows(pos_ref, pos_next_ref, y_ref, buf_ref, sem, tc)
    for c in range(SUBLANES):
        cols = slice(c * LANES, (c + 1) * LANES)
        out_ref[:, cols] = x_ref[:, cols] + buf_ref[slot, _subl(c, tc), :]


def _combine_call(pos3, xn, y, *, tc):
    T, D = xn.shape
    n = T // tc
    return pl.pallas_call(
        _combine_kernel,
        grid=(n,),
        in_specs=[
            pl.BlockSpec((1, 1, tc), lambda i: (i, 0, 0), memory_space=pltpu.SMEM),
            pl.BlockSpec((1, 1, tc), lambda i: (jnp.minimum(i + 1, n - 1), 0, 0), memory_space=pltpu.SMEM),
            pl.BlockSpec((tc, D), lambda i: (i, 0)),
            pl.BlockSpec(memory_space=pl.ANY),
        ],
        out_specs=pl.BlockSpec((tc, D), lambda i: (i, 0)),
        out_shape=jax.ShapeDtypeStruct((T, D), F32),
        scratch_shapes=[pltpu.VMEM((2, tc * SUBLANES, LANES), F32),
                        pltpu.SemaphoreType.DMA((2,))],
        compiler_params=_params(("arbitrary",)),
        name="moe_combine",
    )(pos3, pos3, xn, y)


def _rope_tables(seq):
    half = ROPE_AXIS_DIM // 2
    freqs = ROPE_THETA ** (-(jnp.arange(half, dtype=F32) * 2.0 / ROPE_AXIS_DIM))
    t = jnp.arange(seq)
    ang_row = (t // GRID_W).astype(F32)[:, None] * freqs[None, :]
    ang_col = (t % GRID_W).astype(F32)[:, None] * freqs[None, :]

    def axis_tables(ang):
        c, s = jnp.cos(ang), jnp.sin(ang)
        return jnp.concatenate([c, c], axis=-1), jnp.concatenate([-s, s], axis=-1)

    cr, sr = axis_tables(ang_row)
    cc, sc = axis_tables(ang_col)
    cos = jnp.concatenate([cr, cc], axis=-1)
    sin = jnp.concatenate([sr, sc], axis=-1)
    reps = LANES // HEAD_DIM
    return jnp.tile(cos, (1, reps)), jnp.tile(sin, (1, reps))


def _t5_bucket_table():
    import math
    rel = jnp.arange(3 * Q_BLOCK)[None, :] - Q_BLOCK - jnp.arange(Q_BLOCK)[:, None]
    nb = N_REL_BUCKETS // 2
    max_exact = nb // 2
    ret = jnp.where(rel > 0, nb, 0)
    n = jnp.abs(rel)
    large = max_exact + (jnp.log(jnp.maximum(n, 1).astype(F32) / max_exact)
                         / math.log(REL_MAX_DIST / max_exact) * (nb - max_exact)).astype(I32)
    large = jnp.minimum(large, nb - 1)
    return (ret + jnp.where(n < max_exact, n, large)).astype(I32)


def _tile_tables(counts, tmx, n_tiles):
    tiles_per_class = (counts + tmx - 1) // tmx
    tile_end = jnp.cumsum(tiles_per_class)
    row_start = ((tile_end - tiles_per_class) * tmx).astype(I32)
    n_used = tile_end[-1].astype(I32)
    tile_id = jnp.minimum(jnp.arange(n_tiles, dtype=I32), jnp.maximum(n_used - 1, 0))
    tile_cls = jnp.minimum(jnp.sum(tile_id[:, None] >= tile_end[None, :], axis=1), N_CLASSES - 1).astype(I32)
    group = tile_cls // PAIRS_PER_GROUP
    pair = tile_cls % PAIRS_PER_GROUP
    tile_ea = group * EXPERTS_PER_GROUP + jnp.asarray(PAIR_LO, I32)[pair]
    tile_eb = group * EXPERTS_PER_GROUP + jnp.asarray(PAIR_HI, I32)[pair]
    return row_start, tile_ea.astype(I32), tile_eb.astype(I32), n_used.reshape(1)


def _tile(n, pref):
    t = min(n, pref)
    assert n % t == 0, (n, pref)
    return t


def kernel(x, ln_mix, w_qkv, q_norm, k_norm, w_o, rel_bias, sinks, ln_ffn, w_group, w_expert, w1, w3, w2):
    B, S, D = x.shape
    depth = w_qkv.shape[0]
    T = B * S
    assert S % GRID_W == 0 and S % Q_BLOCK == 0 and D == N_HEADS * HEAD_DIM

    tm = _tile(S, 512)
    tq_flash = _tile(S, 1024)
    tk_flash = _tile(S, 512)
    tq_win = _tile(S, 1024)
    ts = _tile(T, 2048)
    tc = _tile(T, 1024)
    tmx = _tile(T, 256)
    n_tiles = T // tmx + N_CLASSES
    rows_pad = n_tiles * tmx

    cos, sin = _rope_tables(S)
    bias = _bias_call(rel_bias.astype(F32), _t5_bucket_table())

    w_qkv_b = w_qkv.astype(BF16)
    w_o_b = w_o.astype(BF16)
    w1_b, w3_b, w2_b = w1.astype(BF16), w3.astype(BF16), w2.astype(BF16)
    reps = MXU_DIM // HEAD_DIM
    router_f32 = jnp.concatenate(
        [w_group, w_expert,
         jnp.zeros((depth, D, LANES - N_GROUPS - N_EXPERTS), F32)], axis=-1).astype(F32)
    router_hi = router_f32.astype(BF16)
    router_lo = (router_f32 - router_hi.astype(F32)).astype(BF16)
    router = jnp.concatenate([router_hi, router_lo], axis=-1)

    x2 = x.reshape(T, D).astype(F32)
    sorted_rows = jnp.zeros((rows_pad * SUBLANES, LANES), U32)
    pending = None
    for i in range(depth):
        outs = _qkv_call(
            x2, ln_mix[i].reshape(1, D).astype(F32), w_qkv_b[i],
            jnp.tile(q_norm[i].astype(F32), reps).reshape(1, MXU_DIM),
            jnp.tile(k_norm[i].astype(F32), reps).reshape(1, MXU_DIM),
            cos, sin, batch=B, seq=S, rope=(i % N_MIXERS == 0), tm=tm, combine=pending)
        q, k, v = outs[:3]
        if pending is not None:
            x2 = outs[3]
        if i % N_MIXERS == 0:
            o = _flash_call(q, k, v, tq=tq_flash, tk=tk_flash)
        else:
            o = _window_call(sinks[i // N_MIXERS].astype(F32), q, k, v, bias, tq=tq_win)
        x2, rows, cls, rank, counts = _post_call(
            x2, o.reshape(T, D), w_o_b[i], ln_ffn[i].reshape(1, D).astype(F32), router[i], tm=tm)

        row_start, tile_ea, tile_eb, n_used = _tile_tables(
            counts[:N_CLASSES, 0].astype(I32), tmx, n_tiles)
        pos = _pos_call(row_start, cls.reshape(T // LANES, LANES), rank.reshape(T // LANES, LANES))
        sorted_rows = _dispatch_call(pos.reshape(T // ts, 1, ts), rows, sorted_rows, ts=ts)
        y = _moe_call(tile_ea, tile_eb, n_used, sorted_rows, w1_b, w3_b, w2_b, layer=i, tmx=tmx)
        pending = (pos, y)
    pos, y = pending
    x2 = _combine_call(pos.reshape(T // tc, 1, tc), x2, y, tc=tc)
    return x2.reshape(B, S, D).astype(x.dtype)
```

```python
import functools
import math

import jax
import jax.numpy as jnp
from jax import lax
from jax.experimental import pallas as pl
from jax.experimental.pallas import tpu as pltpu

N_HEADS = 16
N_KV_HEADS = 4
HEAD_DIM = 64
KV_GROUP = N_HEADS // N_KV_HEADS
SCALE = HEAD_DIM ** -0.5
GRID_W = 64
ROPE_THETA = 10000.0
ROPE_AXIS_DIM = HEAD_DIM // 2
Q_BLOCK = 128
WINDOW = 128
N_MIXERS = 2
N_REL_BUCKETS = 32
REL_MAX_DIST = 128
N_GROUPS = 4
EXPERTS_PER_GROUP = 4
N_EXPERTS = N_GROUPS * EXPERTS_PER_GROUP
NORM_EPS = 1e-6
NEG_INF = -1e30
LOG2_E = 1.4426950408889634
BF16_TILE_ROWS = 16
V_ROWS = HEAD_DIM + BF16_TILE_ROWS

PAIRS_PER_GROUP = 6
N_CLASSES = N_GROUPS * PAIRS_PER_GROUP
PAIR_LO = (0, 0, 0, 1, 1, 2)
PAIR_HI = (1, 2, 3, 2, 3, 3)

LANES = 128
SUBLANES = 8
MXU_DIM = 256
VMEM_LIMIT = 48 * 1024 * 1024

F32 = jnp.float32
BF16 = jnp.bfloat16
U32 = jnp.uint32
I32 = jnp.int32


def _params(sem, flags=None):
    return pltpu.CompilerParams(dimension_semantics=sem, vmem_limit_bytes=VMEM_LIMIT, flags=flags)


ROW_DMA_UNROLL = 8


def _subl(c, n_rows):
    return pl.ds(c, n_rows, stride=SUBLANES)


def _row_tile(ref, row):
    return ref.at[pl.ds(pl.multiple_of(row * SUBLANES, SUBLANES), SUBLANES)]


def _issue_row_copies(n_rows, make_copy):
    def group(g, carry):
        for u in range(ROW_DMA_UNROLL):
            make_copy(g * ROW_DMA_UNROLL + u).start(priority=u % 2)
        return carry

    lax.fori_loop(0, n_rows // ROW_DMA_UNROLL, group, 0)


def _gathered_rows(pos_ref, pos_next_ref, y_ref, buf_ref, sem, n_rows):
    i = pl.program_id(0)
    slot = i % 2

    def gather(p_ref, s):
        _issue_row_copies(
            n_rows, lambda r: pltpu.make_async_copy(
                _row_tile(y_ref, p_ref[0, 0, r]), _row_tile(buf_ref.at[s], r), sem.at[s]))

    @pl.when(i == 0)
    def _():
        gather(pos_ref, 0)

    @pl.when(i + 1 < pl.num_programs(0))
    def _():
        gather(pos_next_ref, 1 - slot)

    pltpu.make_async_copy(y_ref.at[pl.ds(0, n_rows * SUBLANES)], buf_ref.at[slot], sem.at[slot]).wait()
    return slot


def _segment_sum_matrix():
    r = lax.broadcasted_iota(I32, (MXU_DIM, MXU_DIM), 0) // HEAD_DIM
    c = lax.broadcasted_iota(I32, (MXU_DIM, MXU_DIM), 1) // HEAD_DIM
    return (r == c).astype(BF16)


def _head_rmsnorm(t, gain, seg):
    outs = []
    for c in range(t.shape[1] // MXU_DIM):
        tc = t[:, c * MXU_DIM:(c + 1) * MXU_DIM]
        sq = tc * tc
        hi = sq.astype(BF16)
        lo = (sq - hi.astype(F32)).astype(BF16)
        ss = (jnp.dot(hi, seg, preferred_element_type=F32)
              + jnp.dot(lo, seg, preferred_element_type=F32))
        outs.append(tc * lax.rsqrt(ss * (1.0 / HEAD_DIM) + NORM_EPS) * gain)
    return outs


def _rope(chunks, cos, sin):
    lane = lax.broadcasted_iota(I32, cos.shape, 1)
    upper = (lane & (ROPE_AXIS_DIM // 2)) != 0
    outs = []
    for tc in chunks:
        halves = []
        for j in range(MXU_DIM // LANES):
            xc = tc[:, j * LANES:(j + 1) * LANES]
            partner = jnp.where(upper,
                                pltpu.roll(xc, ROPE_AXIS_DIM // 2, 1),
                                pltpu.roll(xc, LANES - ROPE_AXIS_DIM // 2, 1))
            halves.append(xc * cos + partner * sin)
        outs.append(jnp.concatenate(halves, axis=-1))
    return outs


def _qkv_kernel(*refs, rope, combine):
    if combine:
        (pos_ref, pos_next_ref, x_ref, y_ref, g_ref, w_ref, qg_ref, kg_ref, cos_ref, sin_ref,
         cost_ref, sint_ref, q_ref, k_ref, v_ref, xo_ref, buf_ref, sem) = refs
        tm = x_ref.shape[0]
        slot = _gathered_rows(pos_ref, pos_next_ref, y_ref, buf_ref, sem, tm)
        x = jnp.concatenate(
            [x_ref[:, c * LANES:(c + 1) * LANES] + buf_ref[slot, _subl(c, tm), :]
             for c in range(SUBLANES)], axis=-1)
        xo_ref[...] = x
    else:
        (x_ref, g_ref, w_ref, qg_ref, kg_ref, cos_ref, sin_ref, cost_ref, sint_ref,
         q_ref, k_ref, v_ref) = refs
        x = x_ref[...]
    ms = jnp.mean(x * x, axis=-1, keepdims=True)
    h = (x * lax.rsqrt(ms + NORM_EPS) * g_ref[...]).astype(BF16)
    qkv = jnp.dot(h, w_ref[...], preferred_element_type=F32)
    nq = N_HEADS * HEAD_DIM
    nk = N_KV_HEADS * HEAD_DIM
    seg = _segment_sum_matrix()
    qs = _head_rmsnorm(qkv[:, :nq], qg_ref[...], seg)
    ks = _head_rmsnorm(qkv[:, nq:nq + nk], kg_ref[...], seg)
    if rope:
        ks = _rope(ks, cos_ref[...], sin_ref[...])
    heads_per_chunk = MXU_DIM // HEAD_DIM
    v = qkv[:, nq + nk:]
    for c, kc in enumerate(ks):
        for j in range(heads_per_chunk):
            k_ref[0, c * heads_per_chunk + j] = kc[:, j * HEAD_DIM:(j + 1) * HEAD_DIM].astype(BF16)
    half = ROPE_AXIS_DIM // 2
    for c, qc in enumerate(qs):
        qt = (qc * (SCALE * LOG2_E)).T
        if rope:
            partner = jnp.concatenate(
                [qt[r0 + (half if (r0 // half) % 2 == 0 else -half):][:half]
                 for r0 in range(0, MXU_DIM, half)], axis=0)
            cos_t = jnp.concatenate([cost_ref[...]] * heads_per_chunk, axis=0)
            sin_t = jnp.concatenate([sint_ref[...]] * heads_per_chunk, axis=0)
            qt = qt * cos_t + partner * sin_t
        for j in range(heads_per_chunk):
            q_ref[0, c * heads_per_chunk + j] = qt[j * HEAD_DIM:(j + 1) * HEAD_DIM].astype(BF16)
    vt = v.T
    ones = jnp.ones((V_ROWS - HEAD_DIM, vt.shape[1]), F32)
    for j in range(N_KV_HEADS):
        v_ref[0, j] = jnp.concatenate([vt[j * HEAD_DIM:(j + 1) * HEAD_DIM], ones], axis=0).astype(BF16)


def _qkv_call(x2, gain, w, qg, kg, cos, sin, cos_t, sin_t, *, batch, seq, rope, tm, combine=None):
    T, D = x2.shape
    spt = seq // tm
    n = T // tm
    qkv_dim = w.shape[1]
    kern = functools.partial(_qkv_kernel, rope=rope, combine=combine is not None)
    head_map = lambda i: (i // spt, 0, i % spt, 0)
    feat_map = lambda i: (i // spt, 0, 0, i % spt)
    in_specs = [
        pl.BlockSpec((tm, D), lambda i: (i, 0)),
        pl.BlockSpec((1, D), lambda i: (0, 0)),
        pl.BlockSpec((D, qkv_dim), lambda i: (0, 0)),
        pl.BlockSpec((1, MXU_DIM), lambda i: (0, 0)),
        pl.BlockSpec((1, MXU_DIM), lambda i: (0, 0)),
        pl.BlockSpec((tm, LANES), lambda i: (i % spt, 0)),
        pl.BlockSpec((tm, LANES), lambda i: (i % spt, 0)),
        pl.BlockSpec((HEAD_DIM, tm), lambda i: (0, i % spt)),
        pl.BlockSpec((HEAD_DIM, tm), lambda i: (0, i % spt)),
    ]
    out_specs = [
        pl.BlockSpec((1, N_HEADS, HEAD_DIM, tm), feat_map),
        pl.BlockSpec((1, N_KV_HEADS, tm, HEAD_DIM), head_map),
        pl.BlockSpec((1, N_KV_HEADS, V_ROWS, tm), feat_map),
    ]
    out_shape = [
        jax.ShapeDtypeStruct((batch, N_HEADS, HEAD_DIM, seq), BF16),
        jax.ShapeDtypeStruct((batch, N_KV_HEADS, seq, HEAD_DIM), BF16),
        jax.ShapeDtypeStruct((batch, N_KV_HEADS, V_ROWS, seq), BF16),
    ]
    args = [x2, gain, w, qg, kg, cos, sin, cos_t, sin_t]
    scratch = []
    if combine is not None:
        pos, y = combine
        pos3 = pos.reshape(n, 1, tm)
        in_specs = [
            pl.BlockSpec((1, 1, tm), lambda i: (i, 0, 0), memory_space=pltpu.SMEM),
            pl.BlockSpec((1, 1, tm), lambda i: (jnp.minimum(i + 1, n - 1), 0, 0), memory_space=pltpu.SMEM),
            in_specs[0],
            pl.BlockSpec(memory_space=pl.ANY),
        ] + in_specs[1:]
        out_specs.append(pl.BlockSpec((tm, D), lambda i: (i, 0)))
        out_shape.append(jax.ShapeDtypeStruct((T, D), F32))
        args = [pos3, pos3, x2, y] + args[1:]
        scratch = [pltpu.VMEM((2, tm * SUBLANES, LANES), F32), pltpu.SemaphoreType.DMA((2,))]
    return pl.pallas_call(
        kern,
        grid=(n,),
        in_specs=in_specs,
        out_specs=out_specs,
        out_shape=out_shape,
        scratch_shapes=scratch,
        compiler_params=_params(("arbitrary",)),
        name="qkv_proj",
    )(*args)


FLASH_RING = 2


def _heads_to_rows(o, width):
    pairs = []
    for g in range(0, KV_GROUP, 2):
        two_heads = jnp.concatenate(
            [o[:, g * width:(g + 1) * width], o[:, (g + 1) * width:(g + 2) * width]], axis=0)
        pairs.append(two_heads.T)
    return jnp.concatenate(pairs, axis=-1)


def _flash_kernel(qt_ref, k_ref, vt_ref, o_ref, acc_ref, *s_refs, tk):
    ring = len(s_refs)
    tq = qt_ref.shape[3]
    seq = k_ref.shape[2]
    n_chunks = seq // tk
    key_tiles = tk // MXU_DIM
    units = [(g, g * tq + h * MXU_DIM, h * MXU_DIM) for g in range(KV_GROUP) for h in range(tq // MXU_DIM)]

    def tile(c0, kt):
        return (slice(kt * MXU_DIM, (kt + 1) * MXU_DIM), slice(c0, c0 + MXU_DIM))

    def qk_unit(chunk, unit, dst_ref):
        g, c0, q0 = unit
        mx = None
        for kt in range(key_tiles):
            off = pl.multiple_of(chunk * tk + kt * MXU_DIM, MXU_DIM)
            s = jnp.dot(k_ref[0, 0, pl.ds(off, MXU_DIM), :], qt_ref[0, g, :, q0:q0 + MXU_DIM],
                        preferred_element_type=F32)
            dst_ref[tile(c0, kt)] = s
            t = jnp.max(s, axis=0, keepdims=True)
            mx = t if mx is None else jnp.maximum(mx, t)
        return mx

    def attend_unit(chunk, unit, src_ref, mx, m):
        _, c0, _ = unit
        m_new = jnp.maximum(m, mx)
        pv = None
        for kt in range(key_tiles):
            off = pl.multiple_of(chunk * tk + kt * MXU_DIM, MXU_DIM)
            p = jnp.exp2(src_ref[tile(c0, kt)] - m_new).astype(BF16)
            d = jnp.dot(vt_ref[0, 0, :, pl.ds(off, MXU_DIM)], p, preferred_element_type=F32)
            pv = d if pv is None else pv + d
        cols = slice(c0, c0 + MXU_DIM)
        acc_ref[:, cols] = jnp.exp2(m - m_new) * acc_ref[:, cols] + pv
        return m_new

    def step(chunk, u, mx, m, last=False):
        mx_next, m_out = [], []
        for i, unit in enumerate(units):
            if not last:
                mx_next.append(qk_unit(chunk + 1, unit, s_refs[(u + 1) % ring]))
            m_out.append(attend_unit(chunk, unit, s_refs[u], mx[i], m[i]))
        return tuple(mx_next), tuple(m_out)

    def body(j, carry):
        mx, m = carry
        for u in range(ring):
            mx, m = step(ring * j + u, u, mx, m)
        return mx, m

    acc_ref[...] = jnp.zeros_like(acc_ref)
    m = tuple(jnp.full((1, MXU_DIM), NEG_INF, F32) for _ in units)
    mx = tuple(qk_unit(0, unit, s_refs[0]) for unit in units)
    loops = (n_chunks - 1) // ring
    mx, m = lax.fori_loop(0, loops, body, (mx, m))
    for c in range(loops * ring, n_chunks - 1):
        mx, m = step(c, c % ring, mx, m)
    step(n_chunks - 1, (n_chunks - 1) % ring, mx, m, last=True)
    acc = acc_ref[...]
    o_ref[0] = _heads_to_rows(acc[:HEAD_DIM] / acc[HEAD_DIM:HEAD_DIM + 1], tq).astype(BF16)


def _flash_call(q, k, v, *, tq, tk):
    B, _, _, S = q.shape
    kern = functools.partial(_flash_kernel, tk=tk)
    return pl.pallas_call(
        kern,
        grid=(B, N_KV_HEADS, S // tq),
        in_specs=[
            pl.BlockSpec((1, KV_GROUP, HEAD_DIM, tq), lambda b, h, i: (b, h, 0, i)),
            pl.BlockSpec((1, 1, S, HEAD_DIM), lambda b, h, i: (b, h, 0, 0)),
            pl.BlockSpec((1, 1, V_ROWS, S), lambda b, h, i: (b, h, 0, 0)),
        ],
        out_specs=pl.BlockSpec((1, tq, KV_GROUP * HEAD_DIM), lambda b, h, i: (b, i, h)),
        out_shape=jax.ShapeDtypeStruct((B, S, N_HEADS * HEAD_DIM), BF16),
        scratch_shapes=([pltpu.VMEM((V_ROWS, KV_GROUP * tq), F32)]
                        + [pltpu.VMEM((tk, KV_GROUP * tq), F32)] * FLASH_RING),
        compiler_params=_params(("parallel", "parallel", "parallel")),
        name="flash_global",
    )(q, k, v)


EDGE_VARIANTS = 4


def _bias_kernel(rb_ref, bucket_ref, out_ref):
    h = pl.program_id(0)
    b = bucket_ref[...]
    bias = jnp.zeros(b.shape, F32)
    for j in range(N_REL_BUCKETS):
        bias = jnp.where(b == j, rb_ref[h, j], bias)
    qpos = lax.broadcasted_iota(I32, b.shape, 0)
    col = lax.broadcasted_iota(I32, b.shape, 1)
    in_band = jnp.abs(col - Q_BLOCK - qpos) <= WINDOW
    for e in range(EDGE_VARIANTS):
        valid = in_band
        if e & 1:
            valid = valid & (col >= Q_BLOCK)
        if e & 2:
            valid = valid & (col < 2 * Q_BLOCK)
        out_ref[e, 0] = (jnp.where(valid, bias, NEG_INF) * LOG2_E).T


def _bias_call(rel_bias, bucket):
    table = (bucket.shape[1], bucket.shape[0])
    return pl.pallas_call(
        _bias_kernel,
        grid=(N_HEADS,),
        in_specs=[
            pl.BlockSpec(memory_space=pltpu.SMEM),
            pl.BlockSpec(bucket.shape, lambda h: (0, 0)),
        ],
        out_specs=pl.BlockSpec((EDGE_VARIANTS, 1) + table, lambda h: (0, h, 0, 0)),
        out_shape=jax.ShapeDtypeStruct((EDGE_VARIANTS, N_HEADS) + table, F32),
        compiler_params=_params(("parallel",)),
        name="rel_bias_table",
    )(rel_bias, bucket)


def _window_kernel(sink_ref, qt_ref, k_ref, vt_ref, bias_ref, o_ref, *s_refs, blocks_per_tile):
    hk = pl.program_id(1)
    tile = pl.program_id(2)
    nb = k_ref.shape[2] // Q_BLOCK
    ring = len(s_refs)
    pairs = range(KV_GROUP // 2)

    sinks = []
    for p in pairs:
        first = lax.broadcasted_iota(I32, (1, MXU_DIM), 1) < Q_BLOCK
        sinks.append(jnp.where(first, sink_ref[hk * KV_GROUP + 2 * p],
                               sink_ref[hk * KV_GROUP + 2 * p + 1]) * LOG2_E)

    def key_starts(j):
        n = tile * blocks_per_tile + j
        return n, [pl.multiple_of(jnp.maximum(n - 1, 0) * Q_BLOCK, Q_BLOCK),
                   pl.multiple_of(n * Q_BLOCK, Q_BLOCK),
                   pl.multiple_of(jnp.minimum(n + 1, nb - 1) * Q_BLOCK, Q_BLOCK)]

    def qk_block(j, dst_ref):
        n, starts = key_starts(j)
        kcat = jnp.concatenate([k_ref[0, 0, pl.ds(s, Q_BLOCK), :] for s in starts], axis=0)
        edge = (n == 0).astype(I32) + 2 * (n == nb - 1).astype(I32)
        maxes = []
        for p in pairs:
            heads = (2 * p, 2 * p + 1)
            qt = jnp.concatenate(
                [qt_ref[0, h, :, j * Q_BLOCK:(j + 1) * Q_BLOCK] for h in heads], axis=-1)
            bias = jnp.concatenate([bias_ref[edge, h] for h in heads], axis=-1)
            s = jnp.dot(kcat, qt, preferred_element_type=F32) + bias
            dst_ref[:, p * MXU_DIM:(p + 1) * MXU_DIM] = s
            maxes.append(jnp.max(s, axis=0, keepdims=True))
        return maxes

    def attend_block(j, src_ref, maxes):
        _, starts = key_starts(j)
        vcat = jnp.concatenate([vt_ref[0, 0, :, pl.ds(s, Q_BLOCK)] for s in starts], axis=-1)
        for p in pairs:
            m = jnp.maximum(maxes[p], sinks[p])
            pr = jnp.exp2(src_ref[:, p * MXU_DIM:(p + 1) * MXU_DIM] - m).astype(BF16)
            ov = jnp.dot(vcat, pr, preferred_element_type=F32)
            o = ov[:HEAD_DIM] / (ov[HEAD_DIM:HEAD_DIM + 1] + jnp.exp2(sinks[p] - m))
            two_heads = jnp.concatenate([o[:, :Q_BLOCK], o[:, Q_BLOCK:]], axis=0)
            o_ref[0, j * Q_BLOCK:(j + 1) * Q_BLOCK, p * LANES:(p + 1) * LANES] = two_heads.T.astype(BF16)

    maxes = qk_block(0, s_refs[0])
    for j in range(blocks_per_tile):
        nxt = qk_block(j + 1, s_refs[(j + 1) % ring]) if j + 1 < blocks_per_tile else None
        attend_block(j, s_refs[j % ring], maxes)
        maxes = nxt


def _window_call(sink, q, k, v, bias, *, tq):
    B, _, _, S = q.shape
    kern = functools.partial(_window_kernel, blocks_per_tile=tq // Q_BLOCK)
    return pl.pallas_call(
        kern,
        grid=(B, N_KV_HEADS, S // tq),
        in_specs=[
            pl.BlockSpec(memory_space=pltpu.SMEM),
            pl.BlockSpec((1, KV_GROUP, HEAD_DIM, tq), lambda b, h, i: (b, h, 0, i)),
            pl.BlockSpec((1, 1, S, HEAD_DIM), lambda b, h, i: (b, h, 0, 0)),
            pl.BlockSpec((1, 1, V_ROWS, S), lambda b, h, i: (b, h, 0, 0)),
            pl.BlockSpec((EDGE_VARIANTS, KV_GROUP, 3 * Q_BLOCK, Q_BLOCK), lambda b, h, i: (0, h, 0, 0)),
        ],
        out_specs=pl.BlockSpec((1, tq, KV_GROUP * HEAD_DIM), lambda b, h, i: (b, i, h)),
        out_shape=jax.ShapeDtypeStruct((B, S, N_HEADS * HEAD_DIM), BF16),
        scratch_shapes=[pltpu.VMEM((3 * Q_BLOCK, KV_GROUP * Q_BLOCK), F32)] * 2,
        compiler_params=_params(("parallel", "parallel", "parallel")),
        name="window_attn",
    )(sink, q, k, v, bias)


def _f32_bits(x):
    return lax.bitcast_convert_type(x, U32)


def _pack_bf16_pair(lo_half, hi_half):
    lo = lax.shift_right_logical(_f32_bits(lo_half.astype(BF16).astype(F32)), jnp.uint32(16))
    hi = _f32_bits(hi_half.astype(BF16).astype(F32)) & jnp.uint32(0xFFFF0000)
    return lo | hi


def _unpack_bf16_pair(words):
    lo = lax.bitcast_convert_type(lax.shift_left(words, jnp.uint32(16)), F32)
    hi = lax.bitcast_convert_type(words & jnp.uint32(0xFFFF0000), F32)
    return lo, hi


ROUTER_ROWS = 32


def _post_kernel(x_ref, o_ref, wo_ref, g_ref, wr_ref,
                 xn_ref, row_ref, cls_ref, rank_ref, cnt_ref, run_ref):
    step = pl.program_id(0)
    tm = x_ref.shape[0]
    half = x_ref.shape[1] // 2

    @pl.when(step == 0)
    def _():
        run_ref[...] = jnp.zeros_like(run_ref)

    xn = x_ref[...] + jnp.dot(o_ref[...], wo_ref[...], preferred_element_type=F32)
    xn_ref[...] = xn
    ms = jnp.mean(xn * xn, axis=-1, keepdims=True)
    t = xn * lax.rsqrt(ms + NORM_EPS) * g_ref[...]
    t_hi = t.astype(BF16)
    t_lo = (t - t_hi.astype(F32)).astype(BF16)
    hi_both = jnp.dot(t_hi, wr_ref[...], preferred_element_type=F32)
    lo_hi = jnp.dot(t_lo, wr_ref[:, :LANES], preferred_element_type=F32)
    logits = hi_both[:, :LANES] + (hi_both[:, LANES:] + lo_hi)

    lt = logits.T[:ROUTER_ROWS]
    row = lax.broadcasted_iota(I32, (ROUTER_ROWS, tm), 0).astype(F32)
    big = jnp.float32(ROUTER_ROWS)

    def first_argmax(vals):
        top = jnp.max(vals, axis=0, keepdims=True)
        idx = jnp.min(jnp.where(vals == top, row, big), axis=0, keepdims=True)
        return top, idx

    is_group = row < N_GROUPS
    g_top, g_idx = first_argmax(jnp.where(is_group, lt, -jnp.inf))
    g_sum = jnp.sum(jnp.where(is_group, jnp.exp(lt - g_top), 0.0), axis=0, keepdims=True)
    g_prob = 1.0 / g_sum

    base = N_GROUPS + EXPERTS_PER_GROUP * g_idx
    in_group = (row >= base) & (row < base + EXPERTS_PER_GROUP)
    e_logits = jnp.where(in_group, lt, -jnp.inf)
    e1, i1 = first_argmax(e_logits)
    e2, i2 = first_argmax(jnp.where(row == i1, -jnp.inf, e_logits))
    r = jnp.exp(e2 - e1)
    w_first = (1.0 / (1.0 + r)) * g_prob
    w_second = (r / (1.0 + r)) * g_prob

    j1 = i1 - base
    j2 = i2 - base
    first_is_lo = j1 < j2
    a = jnp.minimum(j1, j2)
    b = jnp.maximum(j1, j2)
    pair = a * 3.0 - a * (a - 1.0) * 0.5 + (b - a - 1.0)
    cls_f = g_idx * PAIRS_PER_GROUP + pair
    w_lo = jnp.where(first_is_lo, w_first, w_second)
    w_hi = jnp.where(first_is_lo, w_second, w_first)

    onehot = row == cls_f
    earlier_tok = (lax.broadcasted_iota(I32, (tm, tm), 0) < lax.broadcasted_iota(I32, (tm, tm), 1))
    earlier = jnp.dot(onehot.astype(BF16), earlier_tok.astype(BF16), preferred_element_type=F32)
    rank = jnp.sum(jnp.where(onehot, earlier + run_ref[:, 0:1], 0.0), axis=0, keepdims=True)
    run_ref[...] += jnp.sum(onehot.astype(F32), axis=1, keepdims=True)
    cnt_ref[...] = run_ref[...]

    def lanes_to_rows(v):
        return jnp.concatenate([v[:, c * LANES:(c + 1) * LANES] for c in range(tm // LANES)], axis=0)

    cls_ref[0] = lanes_to_rows(cls_f).astype(I32)
    rank_ref[0] = lanes_to_rows(rank).astype(I32)

    words = _pack_bf16_pair(t[:, :half], t[:, half:])
    x_subl = half // LANES
    for c in range(x_subl):
        row_ref[_subl(c, tm), :] = words[:, c * LANES:(c + 1) * LANES]
    weights = jnp.concatenate([w_lo, w_hi, jnp.zeros((LANES - 2, tm), F32)], axis=0).T
    row_ref[_subl(x_subl, tm), :] = _f32_bits(weights)
    for c in range(x_subl + 1, SUBLANES):
        row_ref[_subl(c, tm), :] = jnp.zeros((tm, LANES), U32)


def _post_call(x2, o2, wo, gain, wr, *, tm):
    T, D = x2.shape
    assert D // 2 + LANES <= SUBLANES * LANES
    return pl.pallas_call(
        _post_kernel,
        grid=(T // tm,),
        in_specs=[
            pl.BlockSpec((tm, D), lambda i: (i, 0)),
            pl.BlockSpec((tm, D), lambda i: (i, 0)),
            pl.BlockSpec((D, D), lambda i: (0, 0)),
            pl.BlockSpec((1, D), lambda i: (0, 0)),
            pl.BlockSpec((D, 2 * LANES), lambda i: (0, 0)),
        ],
        out_specs=[
            pl.BlockSpec((tm, D), lambda i: (i, 0)),
            pl.BlockSpec((tm * SUBLANES, LANES), lambda i: (i, 0)),
            pl.BlockSpec((1, tm // LANES, LANES), lambda i: (i, 0, 0)),
            pl.BlockSpec((1, tm // LANES, LANES), lambda i: (i, 0, 0)),
            pl.BlockSpec((ROUTER_ROWS, LANES), lambda i: (0, 0)),
        ],
        out_shape=[
            jax.ShapeDtypeStruct((T, D), F32),
            jax.ShapeDtypeStruct((T * SUBLANES, LANES), U32),
            jax.ShapeDtypeStruct((T // tm, tm // LANES, LANES), I32),
            jax.ShapeDtypeStruct((T // tm, tm // LANES, LANES), I32),
            jax.ShapeDtypeStruct((ROUTER_ROWS, LANES), F32),
        ],
        scratch_shapes=[pltpu.VMEM((ROUTER_ROWS, LANES), F32)],
        compiler_params=_params(("arbitrary",)),
        name="post_attn_router",
    )(x2, o2, wo, gain, wr)


def _pos_kernel(start_ref, cls_ref, rank_ref, pos_ref):
    cls = cls_ref[...]
    base = jnp.zeros(cls.shape, I32)
    for c in range(N_CLASSES):
        base = jnp.where(cls == c, start_ref[c], base)
    pos_ref[...] = base + rank_ref[...]


def _pos_call(row_start, cls, rank):
    return pl.pallas_call(
        _pos_kernel,
        in_specs=[
            pl.BlockSpec(memory_space=pltpu.SMEM),
            pl.BlockSpec(memory_space=pltpu.VMEM),
            pl.BlockSpec(memory_space=pltpu.VMEM),
        ],
        out_specs=pl.BlockSpec(memory_space=pltpu.VMEM),
        out_shape=jax.ShapeDtypeStruct(cls.shape, I32),
        name="moe_positions",
    )(row_start, cls, rank)


def _dispatch_kernel(pos_ref, row_ref, init_ref, out_ref, sem):
    del init_ref
    ts = row_ref.shape[0] // SUBLANES
    _issue_row_copies(
        ts, lambda r: pltpu.make_async_copy(
            _row_tile(row_ref, r), _row_tile(out_ref, pos_ref[0, 0, r]), sem))
    pltpu.make_async_copy(row_ref, out_ref.at[pl.ds(0, ts * SUBLANES)], sem).wait()


def _dispatch_call(pos3, rows, init, *, ts):
    T = rows.shape[0] // SUBLANES
    return pl.pallas_call(
        _dispatch_kernel,
        grid=(T // ts,),
        in_specs=[
            pl.BlockSpec((1, 1, ts), lambda i: (i, 0, 0), memory_space=pltpu.SMEM),
            pl.BlockSpec((ts * SUBLANES, LANES), lambda i: (i, 0)),
            pl.BlockSpec(memory_space=pl.ANY),
        ],
        out_specs=pl.BlockSpec(memory_space=pl.ANY),
        out_shape=jax.ShapeDtypeStruct(init.shape, init.dtype),
        scratch_shapes=[pltpu.SemaphoreType.DMA(())],
        input_output_aliases={2: 0},
        compiler_params=_params(("arbitrary",)),
        name="moe_dispatch",
    )(pos3, rows, init)


def _moe_kernel(ea_ref, eb_ref, used_ref, row_ref, w1a_ref, w3a_ref, w2a_ref,
                w1b_ref, w3b_ref, w2b_ref, y_ref):
    del ea_ref, eb_ref
    i = pl.program_id(0)
    x_subl = w1a_ref.shape[2] // (2 * LANES)

    @pl.when(i < used_ref[0])
    def _():
        tmx = y_ref.shape[0] // SUBLANES
        words = jnp.concatenate([row_ref[_subl(c, tmx), :] for c in range(x_subl)], axis=-1)
        lo, hi = _unpack_bf16_pair(words)
        x = jnp.concatenate([lo, hi], axis=-1).astype(BF16)
        wts = lax.bitcast_convert_type(row_ref[_subl(x_subl, tmx), :], F32)
        w_lo = wts[:, 0:1]
        w_hi = wts[:, 1:2]

        ff = w1a_ref.shape[3]

        def hidden(w1_ref, w3_ref):
            blocks = []
            for n0 in range(0, ff, MXU_DIM):
                z = jnp.dot(x, w1_ref[0, 0, :, n0:n0 + MXU_DIM], preferred_element_type=F32)
                u = jnp.dot(x, w3_ref[0, 0, :, n0:n0 + MXU_DIM], preferred_element_type=F32)
                blocks.append(((z * (1.0 / (1.0 + jnp.exp(-z)))) * u).astype(BF16))
            return jnp.concatenate(blocks, axis=-1)

        ha = hidden(w1a_ref, w3a_ref)
        hb = hidden(w1b_ref, w3b_ref)
        for n0 in range(0, SUBLANES * LANES, MXU_DIM):
            wide = slice(n0, n0 + MXU_DIM)
            y = (w_lo * jnp.dot(ha, w2a_ref[0, 0, :, wide], preferred_element_type=F32)
                 + w_hi * jnp.dot(hb, w2b_ref[0, 0, :, wide], preferred_element_type=F32))
            for c in range(MXU_DIM // LANES):
                y_ref[_subl(n0 // LANES + c, tmx), :] = y[:, c * LANES:(c + 1) * LANES]

    @pl.when(i >= used_ref[0])
    def _():
        y_ref[...] = jnp.zeros_like(y_ref)


def _moe_call(tile_ea, tile_eb, n_used, rows, w1, w3, w2, *, layer, tmx):
    rows_pad = rows.shape[0] // SUBLANES
    _, _, D, F = w1.shape
    assert D == SUBLANES * LANES
    wa = lambda i, ea, eb, used: (layer, ea[i], 0, 0)
    wb = lambda i, ea, eb, used: (layer, eb[i], 0, 0)
    grid_spec = pltpu.PrefetchScalarGridSpec(
        num_scalar_prefetch=3,
        grid=(rows_pad // tmx,),
        in_specs=[
            pl.BlockSpec((tmx * SUBLANES, LANES), lambda i, ea, eb, used: (i, 0)),
            pl.BlockSpec((1, 1, D, F), wa),
            pl.BlockSpec((1, 1, D, F), wa),
            pl.BlockSpec((1, 1, F, D), wa),
            pl.BlockSpec((1, 1, D, F), wb),
            pl.BlockSpec((1, 1, D, F), wb),
            pl.BlockSpec((1, 1, F, D), wb),
        ],
        out_specs=pl.BlockSpec((tmx * SUBLANES, LANES), lambda i, ea, eb, used: (i, 0)),
    )
    return pl.pallas_call(
        _moe_kernel,
        grid_spec=grid_spec,
        out_shape=jax.ShapeDtypeStruct((rows_pad * SUBLANES, LANES), F32),
        compiler_params=_params(("arbitrary",)),
        name="moe_experts",
    )(tile_ea, tile_eb, n_used, rows, w1, w3, w2, w1, w3, w2)


def _combine_kernel(pos_ref, pos_next_ref, x_ref, y_ref, out_ref, buf_ref, sem):
    tc = x_ref.shape[0]
    slot = _gathered_rows(pos_ref, pos_next_ref, y_ref, buf_ref, sem, tc)
    for c in range(SUBLANES):
        cols = slice(c * LANES, (c + 1) * LANES)
        out_ref[:, cols] = x_ref[:, cols] + buf_ref[slot, _subl(c, tc), :]


def _combine_call(pos3, xn, y, *, tc):
    T, D = xn.shape
    n = T // tc
    return pl.pallas_call(
        _combine_kernel,
        grid=(n,),
        in_specs=[
            pl.BlockSpec((1, 1, tc), lambda i: (i, 0, 0), memory_space=pltpu.SMEM),
            pl.BlockSpec((1, 1, tc), lambda i: (jnp.minimum(i + 1, n - 1), 0, 0), memory_space=pltpu.SMEM),
            pl.BlockSpec((tc, D), lambda i: (i, 0)),
            pl.BlockSpec(memory_space=pl.ANY),
        ],
        out_specs=pl.BlockSpec((tc, D), lambda i: (i, 0)),
        out_shape=jax.ShapeDtypeStruct((T, D), F32),
        scratch_shapes=[pltpu.VMEM((2, tc * SUBLANES, LANES), F32),
                        pltpu.SemaphoreType.DMA((2,))],
        compiler_params=_params(("arbitrary",)),
        name="moe_combine",
    )(pos3, pos3, xn, y)


def _rope_tables(seq):
    half = ROPE_AXIS_DIM // 2
    freqs = ROPE_THETA ** (-(jnp.arange(half, dtype=F32) * 2.0 / ROPE_AXIS_DIM))
    t = jnp.arange(seq)
    ang_row = (t // GRID_W).astype(F32)[:, None] * freqs[None, :]
    ang_col = (t % GRID_W).astype(F32)[:, None] * freqs[None, :]

    def axis_tables(ang):
        c, s = jnp.cos(ang), jnp.sin(ang)
        return jnp.concatenate([c, c], axis=-1), jnp.concatenate([-s, s], axis=-1)

    cr, sr = axis_tables(ang_row)
    cc, sc = axis_tables(ang_col)
    cos = jnp.concatenate([cr, cc], axis=-1)
    sin = jnp.concatenate([sr, sc], axis=-1)
    reps = LANES // HEAD_DIM
    return jnp.tile(cos, (1, reps)), jnp.tile(sin, (1, reps))


def _t5_bucket_table():
    rel = jnp.arange(3 * Q_BLOCK)[None, :] - Q_BLOCK - jnp.arange(Q_BLOCK)[:, None]
    nb = N_REL_BUCKETS // 2
    max_exact = nb // 2
    ret = jnp.where(rel > 0, nb, 0)
    n = jnp.abs(rel)
    large = max_exact + (jnp.log(jnp.maximum(n, 1).astype(F32) / max_exact)
                         / math.log(REL_MAX_DIST / max_exact) * (nb - max_exact)).astype(I32)
    large = jnp.minimum(large, nb - 1)
    return (ret + jnp.where(n < max_exact, n, large)).astype(I32)


def _tile_tables(counts, tmx, n_tiles):
    tiles_per_class = (counts + tmx - 1) // tmx
    tile_end = jnp.cumsum(tiles_per_class)
    row_start = ((tile_end - tiles_per_class) * tmx).astype(I32)
    n_used = tile_end[-1].astype(I32)
    tile_id = jnp.minimum(jnp.arange(n_tiles, dtype=I32), jnp.maximum(n_used - 1, 0))
    tile_cls = jnp.minimum(jnp.sum(tile_id[:, None] >= tile_end[None, :], axis=1), N_CLASSES - 1).astype(I32)
    group = tile_cls // PAIRS_PER_GROUP
    pair = tile_cls % PAIRS_PER_GROUP
    tile_ea = group * EXPERTS_PER_GROUP + jnp.asarray(PAIR_LO, I32)[pair]
    tile_eb = group * EXPERTS_PER_GROUP + jnp.asarray(PAIR_HI, I32)[pair]
    return row_start, tile_ea.astype(I32), tile_eb.astype(I32), n_used.reshape(1)


def _tile(n, pref):
    t = min(n, pref)
    assert n % t == 0, (n, pref)
    return t


def kernel(x, ln_mix, w_qkv, q_norm, k_norm, w_o, rel_bias, sinks, ln_ffn, w_group, w_expert, w1, w3, w2):
    B, S, D = x.shape
    depth = w_qkv.shape[0]
    T = B * S
    assert S % GRID_W == 0 and S % Q_BLOCK == 0 and D == N_HEADS * HEAD_DIM

    tm = _tile(S, 512)
    tq_flash = _tile(S, 1024)
    tk_flash = _tile(S, 512)
    tq_win = _tile(S, 1024)
    ts = _tile(T, 2048)
    tc = _tile(T, 1024)
    tmx = _tile(T, 256)
    n_tiles = T // tmx + N_CLASSES
    rows_pad = n_tiles * tmx

    cos, sin = _rope_tables(S)
    cos_t, sin_t = cos[:, :HEAD_DIM].T, sin[:, :HEAD_DIM].T
    bias = _bias_call(rel_bias.astype(F32), _t5_bucket_table())

    w_qkv_b = w_qkv.astype(BF16)
    w_o_b = w_o.astype(BF16)
    w1_b, w3_b, w2_b = w1.astype(BF16), w3.astype(BF16), w2.astype(BF16)
    reps = MXU_DIM // HEAD_DIM
    router_f32 = jnp.concatenate(
        [w_group, w_expert,
         jnp.zeros((depth, D, LANES - N_GROUPS - N_EXPERTS), F32)], axis=-1).astype(F32)
    router_hi = router_f32.astype(BF16)
    router_lo = (router_f32 - router_hi.astype(F32)).astype(BF16)
    router = jnp.concatenate([router_hi, router_lo], axis=-1)

    x2 = x.reshape(T, D).astype(F32)
    sorted_rows = jnp.zeros((rows_pad * SUBLANES, LANES), U32)
    pending = None
    for i in range(depth):
        outs = _qkv_call(
            x2, ln_mix[i].reshape(1, D).astype(F32), w_qkv_b[i],
            jnp.tile(q_norm[i].astype(F32), reps).reshape(1, MXU_DIM),
            jnp.tile(k_norm[i].astype(F32), reps).reshape(1, MXU_DIM),
            cos, sin, cos_t, sin_t, batch=B, seq=S, rope=(i % N_MIXERS == 0), tm=tm, combine=pending)
        q, k, v = outs[:3]
        if pending is not None:
            x2 = outs[3]
        if i % N_MIXERS == 0:
            o = _flash_call(q, k, v, tq=tq_flash, tk=tk_flash)
        else:
            o = _window_call(sinks[i // N_MIXERS].astype(F32), q, k, v, bias, tq=tq_win)
        x2, rows, cls, rank, counts = _post_call(
            x2, o.reshape(T, D), w_o_b[i], ln_ffn[i].reshape(1, D).astype(F32), router[i], tm=tm)

        row_start, tile_ea, tile_eb, n_used = _tile_tables(
            counts[:N_CLASSES, 0].astype(I32), tmx, n_tiles)
        pos = _pos_call(row_start, cls.reshape(T // LANES, LANES), rank.reshape(T // LANES, LANES))
        sorted_rows = _dispatch_call(pos.reshape(T // ts, 1, ts), rows, sorted_rows, ts=ts)
        y = _moe_call(tile_ea, tile_eb, n_used, sorted_rows, w1_b, w3_b, w2_b, layer=i, tmx=tmx)
        pending = (pos, y)
    pos, y = pending
    x2 = _combine_call(pos.reshape(T // tc, 1, tc), x2, y, tc=tc)
    return x2.reshape(B, S, D).astype(x.dtype)
```

```python
import functools
import math

import jax
import jax.numpy as jnp
from jax import lax
from jax.experimental import pallas as pl
from jax.experimental.pallas import tpu as pltpu

N_HEADS = 16
N_KV_HEADS = 4
HEAD_DIM = 64
KV_GROUP = N_HEADS // N_KV_HEADS
SCALE = HEAD_DIM ** -0.5
GRID_W = 64
ROPE_THETA = 10000.0
ROPE_AXIS_DIM = HEAD_DIM // 2
Q_BLOCK = 128
WINDOW = 128
N_MIXERS = 2
N_REL_BUCKETS = 32
REL_MAX_DIST = 128
N_GROUPS = 4
EXPERTS_PER_GROUP = 4
N_EXPERTS = N_GROUPS * EXPERTS_PER_GROUP
NORM_EPS = 1e-6
NEG_INF = -1e30
LOG2_E = 1.4426950408889634
BF16_TILE_ROWS = 16
V_ROWS = HEAD_DIM + BF16_TILE_ROWS

PAIRS_PER_GROUP = 6
N_CLASSES = N_GROUPS * PAIRS_PER_GROUP
PAIR_LO = (0, 0, 0, 1, 1, 2)
PAIR_HI = (1, 2, 3, 2, 3, 3)

LANES = 128
SUBLANES = 8
MXU_DIM = 256
VMEM_LIMIT = 48 * 1024 * 1024

F32 = jnp.float32
BF16 = jnp.bfloat16
U32 = jnp.uint32
I32 = jnp.int32


def _params(sem, vmem_limit=VMEM_LIMIT):
    return pltpu.CompilerParams(dimension_semantics=sem, vmem_limit_bytes=vmem_limit)


ROW_DMA_UNROLL = 8


def _subl(c, n_rows):
    return pl.ds(c, n_rows, stride=SUBLANES)


def _row_tile(ref, row):
    return ref.at[pl.ds(pl.multiple_of(row * SUBLANES, SUBLANES), SUBLANES)]


def _issue_row_copies(n_rows, make_copy):
    def group(g, carry):
        for u in range(ROW_DMA_UNROLL):
            make_copy(g * ROW_DMA_UNROLL + u).start(priority=u % 2)
        return carry

    lax.fori_loop(0, n_rows // ROW_DMA_UNROLL, group, 0)


def _gathered_rows(pos_ref, pos_next_ref, y_ref, buf_ref, sem, n_rows):
    i = pl.program_id(0)
    slot = i % 2

    def gather(p_ref, s):
        _issue_row_copies(
            n_rows, lambda r: pltpu.make_async_copy(
                _row_tile(y_ref, p_ref[0, 0, r]), _row_tile(buf_ref.at[s], r), sem.at[s]))

    @pl.when(i == 0)
    def _():
        gather(pos_ref, 0)

    @pl.when(i + 1 < pl.num_programs(0))
    def _():
        gather(pos_next_ref, 1 - slot)

    pltpu.make_async_copy(y_ref.at[pl.ds(0, n_rows * SUBLANES)], buf_ref.at[slot], sem.at[slot]).wait()
    return slot


def _segment_sum_matrix():
    r = lax.broadcasted_iota(I32, (MXU_DIM, MXU_DIM), 0) // HEAD_DIM
    c = lax.broadcasted_iota(I32, (MXU_DIM, MXU_DIM), 1) // HEAD_DIM
    return (r == c).astype(BF16)


def _head_rmsnorm(t, gain, seg):
    outs = []
    for c in range(t.shape[1] // MXU_DIM):
        tc = t[:, c * MXU_DIM:(c + 1) * MXU_DIM]
        sq = tc * tc
        hi = sq.astype(BF16)
        lo = (sq - hi.astype(F32)).astype(BF16)
        ss = (jnp.dot(hi, seg, preferred_element_type=F32)
              + jnp.dot(lo, seg, preferred_element_type=F32))
        outs.append(tc * lax.rsqrt(ss * (1.0 / HEAD_DIM) + NORM_EPS) * gain)
    return outs


def _rope(chunks, cos, sin):
    lane = lax.broadcasted_iota(I32, cos.shape, 1)
    upper = (lane & (ROPE_AXIS_DIM // 2)) != 0
    outs = []
    for tc in chunks:
        halves = []
        for j in range(MXU_DIM // LANES):
            xc = tc[:, j * LANES:(j + 1) * LANES]
            partner = jnp.where(upper,
                                pltpu.roll(xc, ROPE_AXIS_DIM // 2, 1),
                                pltpu.roll(xc, LANES - ROPE_AXIS_DIM // 2, 1))
            halves.append(xc * cos + partner * sin)
        outs.append(jnp.concatenate(halves, axis=-1))
    return outs


def _qkv_kernel(*refs, rope, combine):
    if combine:
        (pos_ref, pos_next_ref, x_ref, y_ref, g_ref, w_ref, qg_ref, kg_ref, cos_ref, sin_ref,
         cost_ref, sint_ref, q_ref, k_ref, v_ref, xo_ref, buf_ref, sem) = refs
        tm = x_ref.shape[0]
        slot = _gathered_rows(pos_ref, pos_next_ref, y_ref, buf_ref, sem, tm)
        x = jnp.concatenate(
            [x_ref[:, c * LANES:(c + 1) * LANES] + buf_ref[slot, _subl(c, tm), :]
             for c in range(SUBLANES)], axis=-1)
        xo_ref[...] = x
    else:
        (x_ref, g_ref, w_ref, qg_ref, kg_ref, cos_ref, sin_ref, cost_ref, sint_ref,
         q_ref, k_ref, v_ref) = refs
        x = x_ref[...]
    ms = jnp.mean(x * x, axis=-1, keepdims=True)
    h = (x * lax.rsqrt(ms + NORM_EPS) * g_ref[...]).astype(BF16)
    qkv = jnp.dot(h, w_ref[...], preferred_element_type=F32)
    nq = N_HEADS * HEAD_DIM
    nk = N_KV_HEADS * HEAD_DIM
    seg = _segment_sum_matrix()
    qs = _head_rmsnorm(qkv[:, :nq], qg_ref[...], seg)
    ks = _head_rmsnorm(qkv[:, nq:nq + nk], kg_ref[...], seg)
    if rope:
        ks = _rope(ks, cos_ref[...], sin_ref[...])
    heads_per_chunk = MXU_DIM // HEAD_DIM
    v = qkv[:, nq + nk:]
    for c, kc in enumerate(ks):
        for j in range(heads_per_chunk):
            k_ref[0, c * heads_per_chunk + j] = kc[:, j * HEAD_DIM:(j + 1) * HEAD_DIM].astype(BF16)
    half = ROPE_AXIS_DIM // 2
    for c, qc in enumerate(qs):
        qt = (qc * (SCALE * LOG2_E)).T
        if rope:
            partner = jnp.concatenate(
                [qt[r0 + (half if (r0 // half) % 2 == 0 else -half):][:half]
                 for r0 in range(0, MXU_DIM, half)], axis=0)
            cos_t = jnp.concatenate([cost_ref[...]] * heads_per_chunk, axis=0)
            sin_t = jnp.concatenate([sint_ref[...]] * heads_per_chunk, axis=0)
            qt = qt * cos_t + partner * sin_t
        for j in range(heads_per_chunk):
            q_ref[0, c * heads_per_chunk + j] = qt[j * HEAD_DIM:(j + 1) * HEAD_DIM].astype(BF16)
    vt = v.T
    ones = jnp.ones((V_ROWS - HEAD_DIM, vt.shape[1]), F32)
    for j in range(N_KV_HEADS):
        v_ref[0, j] = jnp.concatenate([vt[j * HEAD_DIM:(j + 1) * HEAD_DIM], ones], axis=0).astype(BF16)


def _qkv_call(x2, gain, w, qg, kg, cos, sin, cos_t, sin_t, *, batch, seq, rope, tm, combine=None):
    T, D = x2.shape
    spt = seq // tm
    n = T // tm
    qkv_dim = w.shape[1]
    kern = functools.partial(_qkv_kernel, rope=rope, combine=combine is not None)
    head_map = lambda i: (i // spt, 0, i % spt, 0)
    feat_map = lambda i: (i // spt, 0, 0, i % spt)
    in_specs = [
        pl.BlockSpec((tm, D), lambda i: (i, 0)),
        pl.BlockSpec((1, D), lambda i: (0, 0)),
        pl.BlockSpec((D, qkv_dim), lambda i: (0, 0)),
        pl.BlockSpec((1, MXU_DIM), lambda i: (0, 0)),
        pl.BlockSpec((1, MXU_DIM), lambda i: (0, 0)),
        pl.BlockSpec((tm, LANES), lambda i: (i % spt, 0)),
        pl.BlockSpec((tm, LANES), lambda i: (i % spt, 0)),
        pl.BlockSpec((HEAD_DIM, tm), lambda i: (0, i % spt)),
        pl.BlockSpec((HEAD_DIM, tm), lambda i: (0, i % spt)),
    ]
    out_specs = [
        pl.BlockSpec((1, N_HEADS, HEAD_DIM, tm), feat_map),
        pl.BlockSpec((1, N_KV_HEADS, tm, HEAD_DIM), head_map),
        pl.BlockSpec((1, N_KV_HEADS, V_ROWS, tm), feat_map),
    ]
    out_shape = [
        jax.ShapeDtypeStruct((batch, N_HEADS, HEAD_DIM, seq), BF16),
        jax.ShapeDtypeStruct((batch, N_KV_HEADS, seq, HEAD_DIM), BF16),
        jax.ShapeDtypeStruct((batch, N_KV_HEADS, V_ROWS, seq), BF16),
    ]
    args = [x2, gain, w, qg, kg, cos, sin, cos_t, sin_t]
    scratch = []
    if combine is not None:
        pos, y = combine
        pos3 = pos.reshape(n, 1, tm)
        in_specs = [
            pl.BlockSpec((1, 1, tm), lambda i: (i, 0, 0), memory_space=pltpu.SMEM),
            pl.BlockSpec((1, 1, tm), lambda i: (jnp.minimum(i + 1, n - 1), 0, 0), memory_space=pltpu.SMEM),
            in_specs[0],
            pl.BlockSpec(memory_space=pl.ANY),
        ] + in_specs[1:]
        out_specs.append(pl.BlockSpec((tm, D), lambda i: (i, 0)))
        out_shape.append(jax.ShapeDtypeStruct((T, D), F32))
        args = [pos3, pos3, x2, y] + args[1:]
        scratch = [pltpu.VMEM((2, tm * SUBLANES, LANES), F32), pltpu.SemaphoreType.DMA((2,))]
    return pl.pallas_call(
        kern,
        grid=(n,),
        in_specs=in_specs,
        out_specs=out_specs,
        out_shape=out_shape,
        scratch_shapes=scratch,
        compiler_params=_params(("arbitrary",)),
        name="qkv_proj",
    )(*args)


FLASH_RING = 2
FLASH_VMEM_LIMIT = 56 * 1024 * 1024


def _heads_to_rows(o, width):
    pairs = []
    for g in range(0, KV_GROUP, 2):
        two_heads = jnp.concatenate(
            [o[:, g * width:(g + 1) * width], o[:, (g + 1) * width:(g + 2) * width]], axis=0)
        pairs.append(two_heads.T)
    return jnp.concatenate(pairs, axis=-1)


def _flash_kernel(qt_ref, k_ref, vt_ref, o_ref, acc_ref, *s_refs, tk):
    ring = len(s_refs)
    tq = qt_ref.shape[3]
    seq = k_ref.shape[2]
    n_chunks = seq // tk
    key_tiles = tk // MXU_DIM
    units = [(g, g * tq + h * MXU_DIM, h * MXU_DIM) for g in range(KV_GROUP) for h in range(tq // MXU_DIM)]

    def tile(c0, kt):
        return (slice(kt * MXU_DIM, (kt + 1) * MXU_DIM), slice(c0, c0 + MXU_DIM))

    def qk_unit(chunk, unit, dst_ref):
        g, c0, q0 = unit
        mx = None
        for kt in range(key_tiles):
            off = pl.multiple_of(chunk * tk + kt * MXU_DIM, MXU_DIM)
            s = jnp.dot(k_ref[0, 0, pl.ds(off, MXU_DIM), :], qt_ref[0, g, :, q0:q0 + MXU_DIM],
                        preferred_element_type=F32)
            dst_ref[tile(c0, kt)] = s
            t = jnp.max(s, axis=0, keepdims=True)
            mx = t if mx is None else jnp.maximum(mx, t)
        return mx

    def attend_unit(chunk, unit, src_ref, mx, m):
        _, c0, _ = unit
        m_new = jnp.maximum(m, mx)
        pv = None
        for kt in range(key_tiles):
            off = pl.multiple_of(chunk * tk + kt * MXU_DIM, MXU_DIM)
            p = jnp.exp2(src_ref[tile(c0, kt)] - m_new).astype(BF16)
            d = jnp.dot(vt_ref[0, 0, :, pl.ds(off, MXU_DIM)], p, preferred_element_type=F32)
            pv = d if pv is None else pv + d
        cols = slice(c0, c0 + MXU_DIM)
        acc_ref[:, cols] = jnp.exp2(m - m_new) * acc_ref[:, cols] + pv
        return m_new

    def step(chunk, u, mx, m, last=False):
        mx_next, m_out = [], []
        for i, unit in enumerate(units):
            if not last:
                mx_next.append(qk_unit(chunk + 1, unit, s_refs[(u + 1) % ring]))
            m_out.append(attend_unit(chunk, unit, s_refs[u], mx[i], m[i]))
        return tuple(mx_next), tuple(m_out)

    def body(j, carry):
        mx, m = carry
        for u in range(ring):
            mx, m = step(ring * j + u, u, mx, m)
        return mx, m

    acc_ref[...] = jnp.zeros_like(acc_ref)
    m = tuple(jnp.full((1, MXU_DIM), NEG_INF, F32) for _ in units)
    mx = tuple(qk_unit(0, unit, s_refs[0]) for unit in units)
    loops = (n_chunks - 1) // ring
    mx, m = lax.fori_loop(0, loops, body, (mx, m))
    for c in range(loops * ring, n_chunks - 1):
        mx, m = step(c, c % ring, mx, m)
    step(n_chunks - 1, (n_chunks - 1) % ring, mx, m, last=True)
    acc = acc_ref[...]
    o_ref[0] = _heads_to_rows(acc[:HEAD_DIM] / acc[HEAD_DIM:HEAD_DIM + 1], tq).astype(BF16)


def _flash_call(q, k, v, *, tq, tk):
    B, _, _, S = q.shape
    kern = functools.partial(_flash_kernel, tk=tk)
    return pl.pallas_call(
        kern,
        grid=(B, N_KV_HEADS, S // tq),
        in_specs=[
            pl.BlockSpec((1, KV_GROUP, HEAD_DIM, tq), lambda b, h, i: (b, h, 0, i)),
            pl.BlockSpec((1, 1, S, HEAD_DIM), lambda b, h, i: (b, h, 0, 0)),
            pl.BlockSpec((1, 1, V_ROWS, S), lambda b, h, i: (b, h, 0, 0)),
        ],
        out_specs=pl.BlockSpec((1, tq, KV_GROUP * HEAD_DIM), lambda b, h, i: (b, i, h)),
        out_shape=jax.ShapeDtypeStruct((B, S, N_HEADS * HEAD_DIM), BF16),
        scratch_shapes=([pltpu.VMEM((V_ROWS, KV_GROUP * tq), F32)]
                        + [pltpu.VMEM((tk, KV_GROUP * tq), F32)] * FLASH_RING),
        compiler_params=_params(("parallel", "parallel", "parallel"), FLASH_VMEM_LIMIT),
        name="flash_global",
    )(q, k, v)


EDGE_VARIANTS = 4


def _bias_kernel(rb_ref, bucket_ref, out_ref):
    h = pl.program_id(0)
    b = bucket_ref[...]
    bias = jnp.zeros(b.shape, F32)
    for j in range(N_REL_BUCKETS):
        bias = jnp.where(b == j, rb_ref[h, j], bias)
    qpos = lax.broadcasted_iota(I32, b.shape, 0)
    col = lax.broadcasted_iota(I32, b.shape, 1)
    in_band = jnp.abs(col - Q_BLOCK - qpos) <= WINDOW
    for e in range(EDGE_VARIANTS):
        valid = in_band
        if e & 1:
            valid = valid & (col >= Q_BLOCK)
        if e & 2:
            valid = valid & (col < 2 * Q_BLOCK)
        out_ref[e, 0] = (jnp.where(valid, bias, NEG_INF) * LOG2_E).T


def _bias_call(rel_bias, bucket):
    table = (bucket.shape[1], bucket.shape[0])
    return pl.pallas_call(
        _bias_kernel,
        grid=(N_HEADS,),
        in_specs=[
            pl.BlockSpec(memory_space=pltpu.SMEM),
            pl.BlockSpec(bucket.shape, lambda h: (0, 0)),
        ],
        out_specs=pl.BlockSpec((EDGE_VARIANTS, 1) + table, lambda h: (0, h, 0, 0)),
        out_shape=jax.ShapeDtypeStruct((EDGE_VARIANTS, N_HEADS) + table, F32),
        compiler_params=_params(("parallel",)),
        name="rel_bias_table",
    )(rel_bias, bucket)


def _window_kernel(sink_ref, qt_ref, k_ref, vt_ref, bias_ref, o_ref, *s_refs, blocks_per_tile):
    hk = pl.program_id(1)
    tile = pl.program_id(2)
    nb = k_ref.shape[2] // Q_BLOCK
    ring = len(s_refs)
    pairs = range(KV_GROUP // 2)

    sinks = []
    for p in pairs:
        first = lax.broadcasted_iota(I32, (1, MXU_DIM), 1) < Q_BLOCK
        sinks.append(jnp.where(first, sink_ref[hk * KV_GROUP + 2 * p],
                               sink_ref[hk * KV_GROUP + 2 * p + 1]) * LOG2_E)

    def key_starts(j):
        n = tile * blocks_per_tile + j
        return n, [pl.multiple_of(jnp.maximum(n - 1, 0) * Q_BLOCK, Q_BLOCK),
                   pl.multiple_of(n * Q_BLOCK, Q_BLOCK),
                   pl.multiple_of(jnp.minimum(n + 1, nb - 1) * Q_BLOCK, Q_BLOCK)]

    def qk_block(j, dst_ref):
        n, starts = key_starts(j)
        kcat = jnp.concatenate([k_ref[0, 0, pl.ds(s, Q_BLOCK), :] for s in starts], axis=0)
        edge = (n == 0).astype(I32) + 2 * (n == nb - 1).astype(I32)
        maxes = []
        for p in pairs:
            heads = (2 * p, 2 * p + 1)
            qt = jnp.concatenate(
                [qt_ref[0, h, :, j * Q_BLOCK:(j + 1) * Q_BLOCK] for h in heads], axis=-1)
            bias = jnp.concatenate([bias_ref[edge, h] for h in heads], axis=-1)
            s = jnp.dot(kcat, qt, preferred_element_type=F32) + bias
            dst_ref[:, p * MXU_DIM:(p + 1) * MXU_DIM] = s
            maxes.append(jnp.max(s, axis=0, keepdims=True))
        return maxes

    def attend_block(j, src_ref, maxes):
        _, starts = key_starts(j)
        vcat = jnp.concatenate([vt_ref[0, 0, :, pl.ds(s, Q_BLOCK)] for s in starts], axis=-1)
        for p in pairs:
            m = jnp.maximum(maxes[p], sinks[p])
            pr = jnp.exp2(src_ref[:, p * MXU_DIM:(p + 1) * MXU_DIM] - m).astype(BF16)
            ov = jnp.dot(vcat, pr, preferred_element_type=F32)
            o = ov[:HEAD_DIM] / (ov[HEAD_DIM:HEAD_DIM + 1] + jnp.exp2(sinks[p] - m))
            two_heads = jnp.concatenate([o[:, :Q_BLOCK], o[:, Q_BLOCK:]], axis=0)
            o_ref[0, j * Q_BLOCK:(j + 1) * Q_BLOCK, p * LANES:(p + 1) * LANES] = two_heads.T.astype(BF16)

    maxes = qk_block(0, s_refs[0])
    for j in range(blocks_per_tile):
        nxt = qk_block(j + 1, s_refs[(j + 1) % ring]) if j + 1 < blocks_per_tile else None
        attend_block(j, s_refs[j % ring], maxes)
        maxes = nxt


def _window_call(sink, q, k, v, bias, *, tq):
    B, _, _, S = q.shape
    kern = functools.partial(_window_kernel, blocks_per_tile=tq // Q_BLOCK)
    return pl.pallas_call(
        kern,
        grid=(B, N_KV_HEADS, S // tq),
        in_specs=[
            pl.BlockSpec(memory_space=pltpu.SMEM),
            pl.BlockSpec((1, KV_GROUP, HEAD_DIM, tq), lambda b, h, i: (b, h, 0, i)),
            pl.BlockSpec((1, 1, S, HEAD_DIM), lambda b, h, i: (b, h, 0, 0)),
            pl.BlockSpec((1, 1, V_ROWS, S), lambda b, h, i: (b, h, 0, 0)),
            pl.BlockSpec((EDGE_VARIANTS, KV_GROUP, 3 * Q_BLOCK, Q_BLOCK), lambda b, h, i: (0, h, 0, 0)),
        ],
        out_specs=pl.BlockSpec((1, tq, KV_GROUP * HEAD_DIM), lambda b, h, i: (b, i, h)),
        out_shape=jax.ShapeDtypeStruct((B, S, N_HEADS * HEAD_DIM), BF16),
        scratch_shapes=[pltpu.VMEM((3 * Q_BLOCK, KV_GROUP * Q_BLOCK), F32)] * 2,
        compiler_params=_params(("parallel", "parallel", "parallel")),
        name="window_attn",
    )(sink, q, k, v, bias)


def _f32_bits(x):
    return lax.bitcast_convert_type(x, U32)


def _pack_bf16_pair(lo_half, hi_half):
    lo = lax.shift_right_logical(_f32_bits(lo_half.astype(BF16).astype(F32)), jnp.uint32(16))
    hi = _f32_bits(hi_half.astype(BF16).astype(F32)) & jnp.uint32(0xFFFF0000)
    return lo | hi


def _unpack_bf16_pair(words):
    lo = lax.bitcast_convert_type(lax.shift_left(words, jnp.uint32(16)), F32)
    hi = lax.bitcast_convert_type(words & jnp.uint32(0xFFFF0000), F32)
    return lo, hi


ROUTER_ROWS = 32


def _post_kernel(x_ref, o_ref, wo_ref, g_ref, wr_ref,
                 xn_ref, row_ref, cls_ref, rank_ref, cnt_ref, run_ref):
    step = pl.program_id(0)
    tm = x_ref.shape[0]
    half = x_ref.shape[1] // 2

    @pl.when(step == 0)
    def _():
        run_ref[...] = jnp.zeros_like(run_ref)

    xn = x_ref[...] + jnp.dot(o_ref[...], wo_ref[...], preferred_element_type=F32)
    xn_ref[...] = xn
    ms = jnp.mean(xn * xn, axis=-1, keepdims=True)
    t = xn * lax.rsqrt(ms + NORM_EPS) * g_ref[...]
    t_hi = t.astype(BF16)
    t_lo = (t - t_hi.astype(F32)).astype(BF16)
    hi_both = jnp.dot(t_hi, wr_ref[...], preferred_element_type=F32)
    lo_hi = jnp.dot(t_lo, wr_ref[:, :LANES], preferred_element_type=F32)
    logits = hi_both[:, :LANES] + (hi_both[:, LANES:] + lo_hi)

    lt = logits.T[:ROUTER_ROWS]
    row = lax.broadcasted_iota(I32, (ROUTER_ROWS, tm), 0).astype(F32)
    big = jnp.float32(ROUTER_ROWS)

    def first_argmax(vals):
        top = jnp.max(vals, axis=0, keepdims=True)
        idx = jnp.min(jnp.where(vals == top, row, big), axis=0, keepdims=True)
        return top, idx

    is_group = row < N_GROUPS
    g_top, g_idx = first_argmax(jnp.where(is_group, lt, -jnp.inf))
    g_sum = jnp.sum(jnp.where(is_group, jnp.exp(lt - g_top), 0.0), axis=0, keepdims=True)
    g_prob = 1.0 / g_sum

    base = N_GROUPS + EXPERTS_PER_GROUP * g_idx
    in_group = (row >= base) & (row < base + EXPERTS_PER_GROUP)
    e_logits = jnp.where(in_group, lt, -jnp.inf)
    e1, i1 = first_argmax(e_logits)
    e2, i2 = first_argmax(jnp.where(row == i1, -jnp.inf, e_logits))
    r = jnp.exp(e2 - e1)
    w_first = (1.0 / (1.0 + r)) * g_prob
    w_second = (r / (1.0 + r)) * g_prob

    j1 = i1 - base
    j2 = i2 - base
    first_is_lo = j1 < j2
    a = jnp.minimum(j1, j2)
    b = jnp.maximum(j1, j2)
    pair = a * 3.0 - a * (a - 1.0) * 0.5 + (b - a - 1.0)
    cls_f = g_idx * PAIRS_PER_GROUP + pair
    w_lo = jnp.where(first_is_lo, w_first, w_second)
    w_hi = jnp.where(first_is_lo, w_second, w_first)

    onehot = row == cls_f
    earlier_tok = (lax.broadcasted_iota(I32, (tm, tm), 0) < lax.broadcasted_iota(I32, (tm, tm), 1))
    earlier = jnp.dot(onehot.astype(BF16), earlier_tok.astype(BF16), preferred_element_type=F32)
    rank = jnp.sum(jnp.where(onehot, earlier + run_ref[:, 0:1], 0.0), axis=0, keepdims=True)
    run_ref[...] += jnp.sum(onehot.astype(F32), axis=1, keepdims=True)
    cnt_ref[...] = run_ref[...]

    def lanes_to_rows(v):
        return jnp.concatenate([v[:, c * LANES:(c + 1) * LANES] for c in range(tm // LANES)], axis=0)

    cls_ref[0] = lanes_to_rows(cls_f).astype(I32)
    rank_ref[0] = lanes_to_rows(rank).astype(I32)

    words = _pack_bf16_pair(t[:, :half], t[:, half:])
    x_subl = half // LANES
    for c in range(x_subl):
        row_ref[_subl(c, tm), :] = words[:, c * LANES:(c + 1) * LANES]
    weights = jnp.concatenate([w_lo, w_hi, jnp.zeros((LANES - 2, tm), F32)], axis=0).T
    row_ref[_subl(x_subl, tm), :] = _f32_bits(weights)
    for c in range(x_subl + 1, SUBLANES):
        row_ref[_subl(c, tm), :] = jnp.zeros((tm, LANES), U32)


def _post_call(x2, o2, wo, gain, wr, *, tm):
    T, D = x2.shape
    assert D // 2 + LANES <= SUBLANES * LANES
    return pl.pallas_call(
        _post_kernel,
        grid=(T // tm,),
        in_specs=[
            pl.BlockSpec((tm, D), lambda i: (i, 0)),
            pl.BlockSpec((tm, D), lambda i: (i, 0)),
            pl.BlockSpec((D, D), lambda i: (0, 0)),
            pl.BlockSpec((1, D), lambda i: (0, 0)),
            pl.BlockSpec((D, 2 * LANES), lambda i: (0, 0)),
        ],
        out_specs=[
            pl.BlockSpec((tm, D), lambda i: (i, 0)),
            pl.BlockSpec((tm * SUBLANES, LANES), lambda i: (i, 0)),
            pl.BlockSpec((1, tm // LANES, LANES), lambda i: (i, 0, 0)),
            pl.BlockSpec((1, tm // LANES, LANES), lambda i: (i, 0, 0)),
            pl.BlockSpec((ROUTER_ROWS, LANES), lambda i: (0, 0)),
        ],
        out_shape=[
            jax.ShapeDtypeStruct((T, D), F32),
            jax.ShapeDtypeStruct((T * SUBLANES, LANES), U32),
            jax.ShapeDtypeStruct((T // tm, tm // LANES, LANES), I32),
            jax.ShapeDtypeStruct((T // tm, tm // LANES, LANES), I32),
            jax.ShapeDtypeStruct((ROUTER_ROWS, LANES), F32),
        ],
        scratch_shapes=[pltpu.VMEM((ROUTER_ROWS, LANES), F32)],
        compiler_params=_params(("arbitrary",)),
        name="post_attn_router",
    )(x2, o2, wo, gain, wr)


def _pos_kernel(start_ref, cls_ref, rank_ref, pos_ref):
    cls = cls_ref[...]
    base = jnp.zeros(cls.shape, I32)
    for c in range(N_CLASSES):
        base = jnp.where(cls == c, start_ref[c], base)
    pos_ref[...] = base + rank_ref[...]


def _pos_call(row_start, cls, rank):
    return pl.pallas_call(
        _pos_kernel,
        in_specs=[
            pl.BlockSpec(memory_space=pltpu.SMEM),
            pl.BlockSpec(memory_space=pltpu.VMEM),
            pl.BlockSpec(memory_space=pltpu.VMEM),
        ],
        out_specs=pl.BlockSpec(memory_space=pltpu.VMEM),
        out_shape=jax.ShapeDtypeStruct(cls.shape, I32),
        name="moe_positions",
    )(row_start, cls, rank)


def _dispatch_kernel(pos_ref, row_ref, init_ref, out_ref, sem):
    del init_ref
    ts = row_ref.shape[0] // SUBLANES
    _issue_row_copies(
        ts, lambda r: pltpu.make_async_copy(
            _row_tile(row_ref, r), _row_tile(out_ref, pos_ref[0, 0, r]), sem))
    pltpu.make_async_copy(row_ref, out_ref.at[pl.ds(0, ts * SUBLANES)], sem).wait()


def _dispatch_call(pos3, rows, init, *, ts):
    T = rows.shape[0] // SUBLANES
    return pl.pallas_call(
        _dispatch_kernel,
        grid=(T // ts,),
        in_specs=[
            pl.BlockSpec((1, 1, ts), lambda i: (i, 0, 0), memory_space=pltpu.SMEM),
            pl.BlockSpec((ts * SUBLANES, LANES), lambda i: (i, 0)),
            pl.BlockSpec(memory_space=pl.ANY),
        ],
        out_specs=pl.BlockSpec(memory_space=pl.ANY),
        out_shape=jax.ShapeDtypeStruct(init.shape, init.dtype),
        scratch_shapes=[pltpu.SemaphoreType.DMA(())],
        input_output_aliases={2: 0},
        compiler_params=_params(("arbitrary",)),
        name="moe_dispatch",
    )(pos3, rows, init)


def _moe_kernel(ea_ref, eb_ref, used_ref, row_ref, w1a_ref, w3a_ref, w2a_ref,
                w1b_ref, w3b_ref, w2b_ref, y_ref):
    del ea_ref, eb_ref
    i = pl.program_id(0)
    x_subl = w1a_ref.shape[2] // (2 * LANES)

    @pl.when(i < used_ref[0])
    def _():
        tmx = y_ref.shape[0] // SUBLANES
        words = jnp.concatenate([row_ref[_subl(c, tmx), :] for c in range(x_subl)], axis=-1)
        lo, hi = _unpack_bf16_pair(words)
        x = jnp.concatenate([lo, hi], axis=-1).astype(BF16)
        wts = lax.bitcast_convert_type(row_ref[_subl(x_subl, tmx), :], F32)
        w_lo = wts[:, 0:1]
        w_hi = wts[:, 1:2]

        ff = w1a_ref.shape[3]

        def hidden(w1_ref, w3_ref):
            blocks = []
            for n0 in range(0, ff, MXU_DIM):
                z = jnp.dot(x, w1_ref[0, 0, :, n0:n0 + MXU_DIM], preferred_element_type=F32)
                u = jnp.dot(x, w3_ref[0, 0, :, n0:n0 + MXU_DIM], preferred_element_type=F32)
                blocks.append(((z * (1.0 / (1.0 + jnp.exp(-z)))) * u).astype(BF16))
            return jnp.concatenate(blocks, axis=-1)

        ha = hidden(w1a_ref, w3a_ref)
        hb = hidden(w1b_ref, w3b_ref)
        for n0 in range(0, SUBLANES * LANES, MXU_DIM):
            wide = slice(n0, n0 + MXU_DIM)
            y = (w_lo * jnp.dot(ha, w2a_ref[0, 0, :, wide], preferred_element_type=F32)
                 + w_hi * jnp.dot(hb, w2b_ref[0, 0, :, wide], preferred_element_type=F32))
            for c in range(MXU_DIM // LANES):
                y_ref[_subl(n0 // LANES + c, tmx), :] = y[:, c * LANES:(c + 1) * LANES]

    @pl.when(i >= used_ref[0])
    def _():
        y_ref[...] = jnp.zeros_like(y_ref)


def _moe_call(tile_ea, tile_eb, n_used, rows, w1, w3, w2, *, layer, tmx):
    rows_pad = rows.shape[0] // SUBLANES
    _, _, D, F = w1.shape
    assert D == SUBLANES * LANES
    wa = lambda i, ea, eb, used: (layer, ea[i], 0, 0)
    wb = lambda i, ea, eb, used: (layer, eb[i], 0, 0)
    grid_spec = pltpu.PrefetchScalarGridSpec(
        num_scalar_prefetch=3,
        grid=(rows_pad // tmx,),
        in_specs=[
            pl.BlockSpec((tmx * SUBLANES, LANES), lambda i, ea, eb, used: (i, 0)),
            pl.BlockSpec((1, 1, D, F), wa),
            pl.BlockSpec((1, 1, D, F), wa),
            pl.BlockSpec((1, 1, F, D), wa),
            pl.BlockSpec((1, 1, D, F), wb),
            pl.BlockSpec((1, 1, D, F), wb),
            pl.BlockSpec((1, 1, F, D), wb),
        ],
        out_specs=pl.BlockSpec((tmx * SUBLANES, LANES), lambda i, ea, eb, used: (i, 0)),
    )
    return pl.pallas_call(
        _moe_kernel,
        grid_spec=grid_spec,
        out_shape=jax.ShapeDtypeStruct((rows_pad * SUBLANES, LANES), F32),
        compiler_params=_params(("arbitrary",)),
        name="moe_experts",
    )(tile_ea, tile_eb, n_used, rows, w1, w3, w2, w1, w3, w2)


def _combine_kernel(pos_ref, pos_next_ref, x_ref, y_ref, out_ref, buf_ref, sem):
    tc = x_ref.shape[0]
    slot = _gathered_rows(pos_ref, pos_next_ref, y_ref, buf_ref, sem, tc)
    for c in range(SUBLANES):
        cols = slice(c * LANES, (c + 1) * LANES)
        out_ref[:, cols] = x_ref[:, cols] + buf_ref[slot, _subl(c, tc), :]


def _combine_call(pos3, xn, y, *, tc):
    T, D = xn.shape
    n = T // tc
    return pl.pallas_call(
        _combine_kernel,
        grid=(n,),
        in_specs=[
            pl.BlockSpec((1, 1, tc), lambda i: (i, 0, 0), memory_space=pltpu.SMEM),
            pl.BlockSpec((1, 1, tc), lambda i: (jnp.minimum(i + 1, n - 1), 0, 0), memory_space=pltpu.SMEM),
            pl.BlockSpec((tc, D), lambda i: (i, 0)),
            pl.BlockSpec(memory_space=pl.ANY),
        ],
        out_specs=pl.BlockSpec((tc, D), lambda i: (i, 0)),
        out_shape=jax.ShapeDtypeStruct((T, D), F32),
        scratch_shapes=[pltpu.VMEM((2, tc * SUBLANES, LANES), F32),
                        pltpu.SemaphoreType.DMA((2,))],
        compiler_params=_params(("arbitrary",)),
        name="moe_combine",
    )(pos3, pos3, xn, y)


def _rope_tables(seq):
    half = ROPE_AXIS_DIM // 2
    freqs = ROPE_THETA ** (-(jnp.arange(half, dtype=F32) * 2.0 / ROPE_AXIS_DIM))
    t = jnp.arange(seq)
    ang_row = (t // GRID_W).astype(F32)[:, None] * freqs[None, :]
    ang_col = (t % GRID_W).astype(F32)[:, None] * freqs[None, :]

    def axis_tables(ang):
        c, s = jnp.cos(ang), jnp.sin(ang)
        return jnp.concatenate([c, c], axis=-1), jnp.concatenate([-s, s], axis=-1)

    cr, sr = axis_tables(ang_row)
    cc, sc = axis_tables(ang_col)
    cos = jnp.concatenate([cr, cc], axis=-1)
    sin = jnp.concatenate([sr, sc], axis=-1)
    reps = LANES // HEAD_DIM
    return jnp.tile(cos, (1, reps)), jnp.tile(sin, (1, reps))


def _t5_bucket_table():
    rel = jnp.arange(3 * Q_BLOCK)[None, :] - Q_BLOCK - jnp.arange(Q_BLOCK)[:, None]
    nb = N_REL_BUCKETS // 2
    max_exact = nb // 2
    ret = jnp.where(rel > 0, nb, 0)
    n = jnp.abs(rel)
    large = max_exact + (jnp.log(jnp.maximum(n, 1).astype(F32) / max_exact)
                         / math.log(REL_MAX_DIST / max_exact) * (nb - max_exact)).astype(I32)
    large = jnp.minimum(large, nb - 1)
    return (ret + jnp.where(n < max_exact, n, large)).astype(I32)


def _tile_tables(counts, tmx, n_tiles):
    tiles_per_class = (counts + tmx - 1) // tmx
    tile_end = jnp.cumsum(tiles_per_class)
    row_start = ((tile_end - tiles_per_class) * tmx).astype(I32)
    n_used = tile_end[-1].astype(I32)
    tile_id = jnp.minimum(jnp.arange(n_tiles, dtype=I32), jnp.maximum(n_used - 1, 0))
    tile_cls = jnp.minimum(jnp.sum(tile_id[:, None] >= tile_end[None, :], axis=1), N_CLASSES - 1).astype(I32)
    group = tile_cls // PAIRS_PER_GROUP
    pair = tile_cls % PAIRS_PER_GROUP
    tile_ea = group * EXPERTS_PER_GROUP + jnp.asarray(PAIR_LO, I32)[pair]
    tile_eb = group * EXPERTS_PER_GROUP + jnp.asarray(PAIR_HI, I32)[pair]
    return row_start, tile_ea.astype(I32), tile_eb.astype(I32), n_used.reshape(1)


def _tile(n, pref):
    t = min(n, pref)
    assert n % t == 0, (n, pref)
    return t


def kernel(x, ln_mix, w_qkv, q_norm, k_norm, w_o, rel_bias, sinks, ln_ffn, w_group, w_expert, w1, w3, w2):
    B, S, D = x.shape
    depth = w_qkv.shape[0]
    T = B * S
    assert S % GRID_W == 0 and S % Q_BLOCK == 0 and D == N_HEADS * HEAD_DIM

    tm = _tile(S, 512)
    tq_flash = _tile(S, 2048)
    tk_flash = _tile(S, 512)
    tq_win = _tile(S, 1024)
    ts = _tile(T, 2048)
    tc = _tile(T, 1024)
    tmx = _tile(T, 256)
    n_tiles = T // tmx + N_CLASSES
    rows_pad = n_tiles * tmx

    cos, sin = _rope_tables(S)
    cos_t, sin_t = cos[:, :HEAD_DIM].T, sin[:, :HEAD_DIM].T
    bias = _bias_call(rel_bias.astype(F32), _t5_bucket_table())

    w_qkv_b = w_qkv.astype(BF16)
    w_o_b = w_o.astype(BF16)
    w1_b, w3_b, w2_b = w1.astype(BF16), w3.astype(BF16), w2.astype(BF16)
    reps = MXU_DIM // HEAD_DIM
    router_f32 = jnp.concatenate(
        [w_group, w_expert,
         jnp.zeros((depth, D, LANES - N_GROUPS - N_EXPERTS), F32)], axis=-1).astype(F32)
    router_hi = router_f32.astype(BF16)
    router_lo = (router_f32 - router_hi.astype(F32)).astype(BF16)
    router = jnp.concatenate([router_hi, router_lo], axis=-1)

    x2 = x.reshape(T, D).astype(F32)
    sorted_rows = jnp.zeros((rows_pad * SUBLANES, LANES), U32)
    pending = None
    for i in range(depth):
        outs = _qkv_call(
            x2, ln_mix[i].reshape(1, D).astype(F32), w_qkv_b[i],
            jnp.tile(q_norm[i].astype(F32), reps).reshape(1, MXU_DIM),
            jnp.tile(k_norm[i].astype(F32), reps).reshape(1, MXU_DIM),
            cos, sin, cos_t, sin_t, batch=B, seq=S, rope=(i % N_MIXERS == 0), tm=tm, combine=pending)
        q, k, v = outs[:3]
        if pending is not None:
            x2 = outs[3]
        if i % N_MIXERS == 0:
            o = _flash_call(q, k, v, tq=tq_flash, tk=tk_flash)
        else:
            o = _window_call(sinks[i // N_MIXERS].astype(F32), q, k, v, bias, tq=tq_win)
        x2, rows, cls, rank, counts = _post_call(
            x2, o.reshape(T, D), w_o_b[i], ln_ffn[i].reshape(1, D).astype(F32), router[i], tm=tm)

        row_start, tile_ea, tile_eb, n_used = _tile_tables(
            counts[:N_CLASSES, 0].astype(I32), tmx, n_tiles)
        pos = _pos_call(row_start, cls.reshape(T // LANES, LANES), rank.reshape(T // LANES, LANES))
        sorted_rows = _dispatch_call(pos.reshape(T // ts, 1, ts), rows, sorted_rows, ts=ts)
        y = _moe_call(tile_ea, tile_eb, n_used, sorted_rows, w1_b, w3_b, w2_b, layer=i, tmx=tmx)
        pending = (pos, y)
    pos, y = pending
    x2 = _combine_call(pos.reshape(T // tc, 1, tc), x2, y, tc=tc)
    return x2.reshape(B, S, D).astype(x.dtype)
```

```python
import functools
import math

import jax
import jax.numpy as jnp
from jax import lax
from jax.experimental import pallas as pl
from jax.experimental.pallas import tpu as pltpu

N_HEADS = 16
N_KV_HEADS = 4
HEAD_DIM = 64
KV_GROUP = N_HEADS // N_KV_HEADS
SCALE = HEAD_DIM ** -0.5
GRID_W = 64
ROPE_THETA = 10000.0
ROPE_AXIS_DIM = HEAD_DIM // 2
Q_BLOCK = 128
WINDOW = 128
N_MIXERS = 2
N_REL_BUCKETS = 32
REL_MAX_DIST = 128
N_GROUPS = 4
EXPERTS_PER_GROUP = 4
N_EXPERTS = N_GROUPS * EXPERTS_PER_GROUP
NORM_EPS = 1e-6
NEG_INF = -1e30
LOG2_E = 1.4426950408889634
BF16_TILE_ROWS = 16
V_ROWS = HEAD_DIM + BF16_TILE_ROWS

PAIRS_PER_GROUP = 6
N_CLASSES = N_GROUPS * PAIRS_PER_GROUP
PAIR_LO = (0, 0, 0, 1, 1, 2)
PAIR_HI = (1, 2, 3, 2, 3, 3)

LANES = 128
SUBLANES = 8
MXU_DIM = 256
VMEM_LIMIT = 48 * 1024 * 1024

F32 = jnp.float32
BF16 = jnp.bfloat16
U32 = jnp.uint32
I32 = jnp.int32


def _params(sem, vmem_limit=VMEM_LIMIT):
    return pltpu.CompilerParams(dimension_semantics=sem, vmem_limit_bytes=vmem_limit)


ROW_DMA_UNROLL = 8


def _subl(c, n_rows):
    return pl.ds(c, n_rows, stride=SUBLANES)


def _row_tile(ref, row):
    return ref.at[pl.ds(pl.multiple_of(row * SUBLANES, SUBLANES), SUBLANES)]


def _issue_row_copies(n_rows, make_copy):
    def group(g, carry):
        for u in range(ROW_DMA_UNROLL):
            make_copy(g * ROW_DMA_UNROLL + u).start(priority=u % 2)
        return carry

    lax.fori_loop(0, n_rows // ROW_DMA_UNROLL, group, 0)


def _gathered_rows(pos_ref, pos_next_ref, y_ref, buf_ref, sem, n_rows):
    i = pl.program_id(0)
    slot = i % 2

    def gather(p_ref, s):
        _issue_row_copies(
            n_rows, lambda r: pltpu.make_async_copy(
                _row_tile(y_ref, p_ref[0, 0, r]), _row_tile(buf_ref.at[s], r), sem.at[s]))

    @pl.when(i == 0)
    def _():
        gather(pos_ref, 0)

    @pl.when(i + 1 < pl.num_programs(0))
    def _():
        gather(pos_next_ref, 1 - slot)

    pltpu.make_async_copy(y_ref.at[pl.ds(0, n_rows * SUBLANES)], buf_ref.at[slot], sem.at[slot]).wait()
    return slot


def _segment_sum_matrix():
    r = lax.broadcasted_iota(I32, (MXU_DIM, MXU_DIM), 0) // HEAD_DIM
    c = lax.broadcasted_iota(I32, (MXU_DIM, MXU_DIM), 1) // HEAD_DIM
    return (r == c).astype(BF16)


def _head_rmsnorm(t, gain, seg):
    outs = []
    for c in range(t.shape[1] // MXU_DIM):
        tc = t[:, c * MXU_DIM:(c + 1) * MXU_DIM]
        sq = tc * tc
        hi = sq.astype(BF16)
        lo = (sq - hi.astype(F32)).astype(BF16)
        ss = (jnp.dot(hi, seg, preferred_element_type=F32)
              + jnp.dot(lo, seg, preferred_element_type=F32))
        outs.append(tc * lax.rsqrt(ss * (1.0 / HEAD_DIM) + NORM_EPS) * gain)
    return outs


def _rope(chunks, cos, sin):
    lane = lax.broadcasted_iota(I32, cos.shape, 1)
    upper = (lane & (ROPE_AXIS_DIM // 2)) != 0
    outs = []
    for tc in chunks:
        halves = []
        for j in range(MXU_DIM // LANES):
            xc = tc[:, j * LANES:(j + 1) * LANES]
            partner = jnp.where(upper,
                                pltpu.roll(xc, ROPE_AXIS_DIM // 2, 1),
                                pltpu.roll(xc, LANES - ROPE_AXIS_DIM // 2, 1))
            halves.append(xc * cos + partner * sin)
        outs.append(jnp.concatenate(halves, axis=-1))
    return outs


def _qkv_kernel(*refs, rope, combine):
    if combine:
        (pos_ref, pos_next_ref, x_ref, y_ref, g_ref, w_ref, qg_ref, kg_ref, cos_ref, sin_ref,
         cost_ref, sint_ref, q_ref, k_ref, v_ref, xo_ref, buf_ref, sem) = refs
        tm = x_ref.shape[0]
        slot = _gathered_rows(pos_ref, pos_next_ref, y_ref, buf_ref, sem, tm)
        x = jnp.concatenate(
            [x_ref[:, c * LANES:(c + 1) * LANES] + buf_ref[slot, _subl(c, tm), :]
             for c in range(SUBLANES)], axis=-1)
        xo_ref[...] = x
    else:
        (x_ref, g_ref, w_ref, qg_ref, kg_ref, cos_ref, sin_ref, cost_ref, sint_ref,
         q_ref, k_ref, v_ref) = refs
        x = x_ref[...]
    ms = jnp.mean(x * x, axis=-1, keepdims=True)
    h = (x * lax.rsqrt(ms + NORM_EPS) * g_ref[...]).astype(BF16)
    qkv = jnp.dot(h, w_ref[...], preferred_element_type=F32)
    nq = N_HEADS * HEAD_DIM
    nk = N_KV_HEADS * HEAD_DIM
    seg = _segment_sum_matrix()
    qs = _head_rmsnorm(qkv[:, :nq], qg_ref[...], seg)
    ks = _head_rmsnorm(qkv[:, nq:nq + nk], kg_ref[...], seg)
    if rope:
        ks = _rope(ks, cos_ref[...], sin_ref[...])
    heads_per_chunk = MXU_DIM // HEAD_DIM
    v = qkv[:, nq + nk:]
    for c, kc in enumerate(ks):
        for j in range(heads_per_chunk):
            k_ref[0, c * heads_per_chunk + j] = kc[:, j * HEAD_DIM:(j + 1) * HEAD_DIM].astype(BF16)
    half = ROPE_AXIS_DIM // 2
    for c, qc in enumerate(qs):
        qt = (qc * (SCALE * LOG2_E)).T
        if rope:
            partner = jnp.concatenate(
                [qt[r0 + (half if (r0 // half) % 2 == 0 else -half):][:half]
                 for r0 in range(0, MXU_DIM, half)], axis=0)
            cos_t = jnp.concatenate([cost_ref[...]] * heads_per_chunk, axis=0)
            sin_t = jnp.concatenate([sint_ref[...]] * heads_per_chunk, axis=0)
            qt = qt * cos_t + partner * sin_t
        for j in range(heads_per_chunk):
            q_ref[0, c * heads_per_chunk + j] = qt[j * HEAD_DIM:(j + 1) * HEAD_DIM].astype(BF16)
    vt = v.T
    ones = jnp.ones((V_ROWS - HEAD_DIM, vt.shape[1]), F32)
    for j in range(N_KV_HEADS):
        v_ref[0, j] = jnp.concatenate([vt[j * HEAD_DIM:(j + 1) * HEAD_DIM], ones], axis=0).astype(BF16)


def _qkv_call(x2, gain, w, qg, kg, cos, sin, cos_t, sin_t, *, batch, seq, rope, tm, combine=None):
    T, D = x2.shape
    spt = seq // tm
    n = T // tm
    qkv_dim = w.shape[1]
    kern = functools.partial(_qkv_kernel, rope=rope, combine=combine is not None)
    head_map = lambda i: (i // spt, 0, i % spt, 0)
    feat_map = lambda i: (i // spt, 0, 0, i % spt)
    in_specs = [
        pl.BlockSpec((tm, D), lambda i: (i, 0)),
        pl.BlockSpec((1, D), lambda i: (0, 0)),
        pl.BlockSpec((D, qkv_dim), lambda i: (0, 0)),
        pl.BlockSpec((1, MXU_DIM), lambda i: (0, 0)),
        pl.BlockSpec((1, MXU_DIM), lambda i: (0, 0)),
        pl.BlockSpec((tm, LANES), lambda i: (i % spt, 0)),
        pl.BlockSpec((tm, LANES), lambda i: (i % spt, 0)),
        pl.BlockSpec((HEAD_DIM, tm), lambda i: (0, i % spt)),
        pl.BlockSpec((HEAD_DIM, tm), lambda i: (0, i % spt)),
    ]
    out_specs = [
        pl.BlockSpec((1, N_HEADS, HEAD_DIM, tm), feat_map),
        pl.BlockSpec((1, N_KV_HEADS, tm, HEAD_DIM), head_map),
        pl.BlockSpec((1, N_KV_HEADS, V_ROWS, tm), feat_map),
    ]
    out_shape = [
        jax.ShapeDtypeStruct((batch, N_HEADS, HEAD_DIM, seq), BF16),
        jax.ShapeDtypeStruct((batch, N_KV_HEADS, seq, HEAD_DIM), BF16),
        jax.ShapeDtypeStruct((batch, N_KV_HEADS, V_ROWS, seq), BF16),
    ]
    args = [x2, gain, w, qg, kg, cos, sin, cos_t, sin_t]
    scratch = []
    if combine is not None:
        pos, y = combine
        pos3 = pos.reshape(n, 1, tm)
        in_specs = [
            pl.BlockSpec((1, 1, tm), lambda i: (i, 0, 0), memory_space=pltpu.SMEM),
            pl.BlockSpec((1, 1, tm), lambda i: (jnp.minimum(i + 1, n - 1), 0, 0), memory_space=pltpu.SMEM),
            in_specs[0],
            pl.BlockSpec(memory_space=pl.ANY),
        ] + in_specs[1:]
        out_specs.append(pl.BlockSpec((tm, D), lambda i: (i, 0)))
        out_shape.append(jax.ShapeDtypeStruct((T, D), F32))
        args = [pos3, pos3, x2, y] + args[1:]
        scratch = [pltpu.VMEM((2, tm * SUBLANES, LANES), F32), pltpu.SemaphoreType.DMA((2,))]
    return pl.pallas_call(
        kern,
        grid=(n,),
        in_specs=in_specs,
        out_specs=out_specs,
        out_shape=out_shape,
        scratch_shapes=scratch,
        compiler_params=_params(("arbitrary",)),
        name="qkv_proj",
    )(*args)


FLASH_RING = 2
FLASH_VMEM_LIMIT = 56 * 1024 * 1024


def _heads_to_rows(o, width):
    pairs = []
    for g in range(0, KV_GROUP, 2):
        two_heads = jnp.concatenate(
            [o[:, g * width:(g + 1) * width], o[:, (g + 1) * width:(g + 2) * width]], axis=0)
        pairs.append(two_heads.T)
    return jnp.concatenate(pairs, axis=-1)


def _flash_kernel(qt_ref, k_ref, vt_ref, o_ref, acc_ref, *s_refs, tk):
    ring = len(s_refs)
    tq = qt_ref.shape[3]
    seq = k_ref.shape[2]
    n_chunks = seq // tk
    key_tiles = tk // MXU_DIM
    units = [(g, g * tq + h * MXU_DIM, h * MXU_DIM) for g in range(KV_GROUP) for h in range(tq // MXU_DIM)]

    def tile(c0, kt):
        return (slice(kt * MXU_DIM, (kt + 1) * MXU_DIM), slice(c0, c0 + MXU_DIM))

    def qk_unit(chunk, unit, dst_ref):
        g, c0, q0 = unit
        mx = None
        for kt in range(key_tiles):
            off = pl.multiple_of(chunk * tk + kt * MXU_DIM, MXU_DIM)
            s = jnp.dot(k_ref[0, 0, pl.ds(off, MXU_DIM), :], qt_ref[0, g, :, q0:q0 + MXU_DIM],
                        preferred_element_type=F32)
            dst_ref[tile(c0, kt)] = s
            t = jnp.max(s, axis=0, keepdims=True)
            mx = t if mx is None else jnp.maximum(mx, t)
        return mx

    def attend_unit(chunk, unit, src_ref, mx, m):
        _, c0, _ = unit
        m_new = jnp.maximum(m, mx)
        pv = None
        for kt in range(key_tiles):
            off = pl.multiple_of(chunk * tk + kt * MXU_DIM, MXU_DIM)
            p = jnp.exp2(src_ref[tile(c0, kt)] - m_new).astype(BF16)
            d = jnp.dot(vt_ref[0, 0, :, pl.ds(off, MXU_DIM)], p, preferred_element_type=F32)
            pv = d if pv is None else pv + d
        cols = slice(c0, c0 + MXU_DIM)
        acc_ref[:, cols] = jnp.exp2(m - m_new) * acc_ref[:, cols] + pv
        return m_new

    def step(chunk, u, mx, m, last=False):
        mx_next, m_out = [], []
        for i, unit in enumerate(units):
            if not last:
                mx_next.append(qk_unit(chunk + 1, unit, s_refs[(u + 1) % ring]))
            m_out.append(attend_unit(chunk, unit, s_refs[u], mx[i], m[i]))
        return tuple(mx_next), tuple(m_out)

    def body(j, carry):
        mx, m = carry
        for u in range(ring):
            mx, m = step(ring * j + u, u, mx, m)
        return mx, m

    acc_ref[...] = jnp.zeros_like(acc_ref)
    m = tuple(jnp.full((1, MXU_DIM), NEG_INF, F32) for _ in units)
    mx = tuple(qk_unit(0, unit, s_refs[0]) for unit in units)
    loops = (n_chunks - 1) // ring
    mx, m = lax.fori_loop(0, loops, body, (mx, m))
    for c in range(loops * ring, n_chunks - 1):
        mx, m = step(c, c % ring, mx, m)
    step(n_chunks - 1, (n_chunks - 1) % ring, mx, m, last=True)
    acc = acc_ref[...]
    o_ref[0] = _heads_to_rows(acc[:HEAD_DIM] / acc[HEAD_DIM:HEAD_DIM + 1], tq).astype(BF16)


def _flash_call(q, k, v, *, tq, tk):
    B, _, _, S = q.shape
    kern = functools.partial(_flash_kernel, tk=tk)
    return pl.pallas_call(
        kern,
        grid=(B, N_KV_HEADS, S // tq),
        in_specs=[
            pl.BlockSpec((1, KV_GROUP, HEAD_DIM, tq), lambda b, h, i: (b, h, 0, i)),
            pl.BlockSpec((1, 1, S, HEAD_DIM), lambda b, h, i: (b, h, 0, 0)),
            pl.BlockSpec((1, 1, V_ROWS, S), lambda b, h, i: (b, h, 0, 0)),
        ],
        out_specs=pl.BlockSpec((1, tq, KV_GROUP * HEAD_DIM), lambda b, h, i: (b, i, h)),
        out_shape=jax.ShapeDtypeStruct((B, S, N_HEADS * HEAD_DIM), BF16),
        scratch_shapes=([pltpu.VMEM((V_ROWS, KV_GROUP * tq), F32)]
                        + [pltpu.VMEM((tk, KV_GROUP * tq), F32)] * FLASH_RING),
        compiler_params=_params(("parallel", "parallel", "parallel"), FLASH_VMEM_LIMIT),
        name="flash_global",
    )(q, k, v)


EDGE_VARIANTS = 4


def _bias_kernel(rb_ref, bucket_ref, out_ref):
    h = pl.program_id(0)
    b = bucket_ref[...]
    bias = jnp.zeros(b.shape, F32)
    for j in range(N_REL_BUCKETS):
        bias = jnp.where(b == j, rb_ref[h, j], bias)
    qpos = lax.broadcasted_iota(I32, b.shape, 0)
    col = lax.broadcasted_iota(I32, b.shape, 1)
    in_band = jnp.abs(col - Q_BLOCK - qpos) <= WINDOW
    for e in range(EDGE_VARIANTS):
        valid = in_band
        if e & 1:
            valid = valid & (col >= Q_BLOCK)
        if e & 2:
            valid = valid & (col < 2 * Q_BLOCK)
        out_ref[e, 0] = (jnp.where(valid, bias, NEG_INF) * LOG2_E).T


def _bias_call(rel_bias, bucket):
    table = (bucket.shape[1], bucket.shape[0])
    return pl.pallas_call(
        _bias_kernel,
        grid=(N_HEADS,),
        in_specs=[
            pl.BlockSpec(memory_space=pltpu.SMEM),
            pl.BlockSpec(bucket.shape, lambda h: (0, 0)),
        ],
        out_specs=pl.BlockSpec((EDGE_VARIANTS, 1) + table, lambda h: (0, h, 0, 0)),
        out_shape=jax.ShapeDtypeStruct((EDGE_VARIANTS, N_HEADS) + table, F32),
        compiler_params=_params(("parallel",)),
        name="rel_bias_table",
    )(rel_bias, bucket)


def _window_kernel(sink_ref, qt_ref, k_ref, vt_ref, bias_ref, o_ref, *s_refs, blocks_per_tile):
    hk = pl.program_id(1)
    tile = pl.program_id(2)
    nb = k_ref.shape[2] // Q_BLOCK
    ring = len(s_refs)
    pairs = range(KV_GROUP // 2)

    sinks = []
    for p in pairs:
        first = lax.broadcasted_iota(I32, (1, MXU_DIM), 1) < Q_BLOCK
        sinks.append(jnp.where(first, sink_ref[hk * KV_GROUP + 2 * p],
                               sink_ref[hk * KV_GROUP + 2 * p + 1]) * LOG2_E)

    def key_starts(j):
        n = tile * blocks_per_tile + j
        return n, [pl.multiple_of(jnp.maximum(n - 1, 0) * Q_BLOCK, Q_BLOCK),
                   pl.multiple_of(n * Q_BLOCK, Q_BLOCK),
                   pl.multiple_of(jnp.minimum(n + 1, nb - 1) * Q_BLOCK, Q_BLOCK)]

    def qk_block(j, dst_ref):
        n, starts = key_starts(j)
        kcat = jnp.concatenate([k_ref[0, 0, pl.ds(s, Q_BLOCK), :] for s in starts], axis=0)
        edge = (n == 0).astype(I32) + 2 * (n == nb - 1).astype(I32)
        maxes = []
        for p in pairs:
            heads = (2 * p, 2 * p + 1)
            qt = jnp.concatenate(
                [qt_ref[0, h, :, j * Q_BLOCK:(j + 1) * Q_BLOCK] for h in heads], axis=-1)
            bias = jnp.concatenate([bias_ref[edge, h] for h in heads], axis=-1)
            s = jnp.dot(kcat, qt, preferred_element_type=F32) + bias
            dst_ref[:, p * MXU_DIM:(p + 1) * MXU_DIM] = s
            maxes.append(jnp.max(s, axis=0, keepdims=True))
        return maxes

    def attend_block(j, src_ref, maxes):
        _, starts = key_starts(j)
        vcat = jnp.concatenate([vt_ref[0, 0, :, pl.ds(s, Q_BLOCK)] for s in starts], axis=-1)
        for p in pairs:
            m = jnp.maximum(maxes[p], sinks[p])
            pr = jnp.exp2(src_ref[:, p * MXU_DIM:(p + 1) * MXU_DIM] - m).astype(BF16)
            ov = jnp.dot(vcat, pr, preferred_element_type=F32)
            o = ov[:HEAD_DIM] / (ov[HEAD_DIM:HEAD_DIM + 1] + jnp.exp2(sinks[p] - m))
            two_heads = jnp.concatenate([o[:, :Q_BLOCK], o[:, Q_BLOCK:]], axis=0)
            o_ref[0, j * Q_BLOCK:(j + 1) * Q_BLOCK, p * LANES:(p + 1) * LANES] = two_heads.T.astype(BF16)

    maxes = qk_block(0, s_refs[0])
    for j in range(blocks_per_tile):
        nxt = qk_block(j + 1, s_refs[(j + 1) % ring]) if j + 1 < blocks_per_tile else None
        attend_block(j, s_refs[j % ring], maxes)
        maxes = nxt


def _window_call(sink, q, k, v, bias, *, tq):
    B, _, _, S = q.shape
    kern = functools.partial(_window_kernel, blocks_per_tile=tq // Q_BLOCK)
    return pl.pallas_call(
        kern,
        grid=(B, N_KV_HEADS, S // tq),
        in_specs=[
            pl.BlockSpec(memory_space=pltpu.SMEM),
            pl.BlockSpec((1, KV_GROUP, HEAD_DIM, tq), lambda b, h, i: (b, h, 0, i)),
            pl.BlockSpec((1, 1, S, HEAD_DIM), lambda b, h, i: (b, h, 0, 0)),
            pl.BlockSpec((1, 1, V_ROWS, S), lambda b, h, i: (b, h, 0, 0)),
            pl.BlockSpec((EDGE_VARIANTS, KV_GROUP, 3 * Q_BLOCK, Q_BLOCK), lambda b, h, i: (0, h, 0, 0)),
        ],
        out_specs=pl.BlockSpec((1, tq, KV_GROUP * HEAD_DIM), lambda b, h, i: (b, i, h)),
        out_shape=jax.ShapeDtypeStruct((B, S, N_HEADS * HEAD_DIM), BF16),
        scratch_shapes=[pltpu.VMEM((3 * Q_BLOCK, KV_GROUP * Q_BLOCK), F32)] * 2,
        compiler_params=_params(("parallel", "parallel", "parallel")),
        name="window_attn",
    )(sink, q, k, v, bias)


def _f32_bits(x):
    return lax.bitcast_convert_type(x, U32)


def _pack_bf16_pair(lo_half, hi_half):
    lo = lax.shift_right_logical(_f32_bits(lo_half.astype(BF16).astype(F32)), jnp.uint32(16))
    hi = _f32_bits(hi_half.astype(BF16).astype(F32)) & jnp.uint32(0xFFFF0000)
    return lo | hi


def _unpack_bf16_pair(words):
    lo = lax.bitcast_convert_type(lax.shift_left(words, jnp.uint32(16)), F32)
    hi = lax.bitcast_convert_type(words & jnp.uint32(0xFFFF0000), F32)
    return lo, hi


ROUTER_ROWS = 32


POST_INPUT_SLOTS = 3


def _post_kernel(x_hbm, o_hbm, wo_ref, g_ref, wr_ref,
                 xn_ref, row_ref, cls_ref, rank_ref, cnt_ref, run_ref, xbuf, obuf, sem, *, n_steps):
    step = pl.program_id(0)
    tm = xn_ref.shape[0]
    half = xn_ref.shape[1] // 2

    def tile_copies(s, slot):
        rows = pl.ds(pl.multiple_of(s * tm, tm), tm)
        return (pltpu.make_async_copy(x_hbm.at[rows], xbuf.at[slot], sem.at[0, slot]),
                pltpu.make_async_copy(o_hbm.at[rows], obuf.at[slot], sem.at[1, slot]))

    def fetch(s, slot):
        for copy in tile_copies(s, slot):
            copy.start()

    @pl.when(step == 0)
    def _():
        run_ref[...] = jnp.zeros_like(run_ref)
        for s in range(min(POST_INPUT_SLOTS - 1, n_steps)):
            fetch(s, s)

    @pl.when(step + POST_INPUT_SLOTS - 1 < n_steps)
    def _():
        ahead = step + POST_INPUT_SLOTS - 1
        fetch(ahead, ahead % POST_INPUT_SLOTS)

    slot = step % POST_INPUT_SLOTS
    for copy in tile_copies(step, slot):
        copy.wait()

    xn = xbuf[slot] + jnp.dot(obuf[slot], wo_ref[...], preferred_element_type=F32)
    xn_ref[...] = xn
    ms = jnp.mean(xn * xn, axis=-1, keepdims=True)
    t = xn * lax.rsqrt(ms + NORM_EPS) * g_ref[...]
    t_hi = t.astype(BF16)
    t_lo = (t - t_hi.astype(F32)).astype(BF16)
    hi_both = jnp.dot(t_hi, wr_ref[...], preferred_element_type=F32)
    lo_hi = jnp.dot(t_lo, wr_ref[:, :LANES], preferred_element_type=F32)
    logits = hi_both[:, :LANES] + (hi_both[:, LANES:] + lo_hi)

    lt = logits.T[:ROUTER_ROWS]
    row = lax.broadcasted_iota(I32, (ROUTER_ROWS, tm), 0).astype(F32)
    big = jnp.float32(ROUTER_ROWS)

    def first_argmax(vals):
        top = jnp.max(vals, axis=0, keepdims=True)
        idx = jnp.min(jnp.where(vals == top, row, big), axis=0, keepdims=True)
        return top, idx

    is_group = row < N_GROUPS
    g_top, g_idx = first_argmax(jnp.where(is_group, lt, -jnp.inf))
    g_sum = jnp.sum(jnp.where(is_group, jnp.exp(lt - g_top), 0.0), axis=0, keepdims=True)
    g_prob = 1.0 / g_sum

    base = N_GROUPS + EXPERTS_PER_GROUP * g_idx
    in_group = (row >= base) & (row < base + EXPERTS_PER_GROUP)
    e_logits = jnp.where(in_group, lt, -jnp.inf)
    e1, i1 = first_argmax(e_logits)
    e2, i2 = first_argmax(jnp.where(row == i1, -jnp.inf, e_logits))
    r = jnp.exp(e2 - e1)
    w_first = (1.0 / (1.0 + r)) * g_prob
    w_second = (r / (1.0 + r)) * g_prob

    j1 = i1 - base
    j2 = i2 - base
    first_is_lo = j1 < j2
    a = jnp.minimum(j1, j2)
    b = jnp.maximum(j1, j2)
    pair = a * 3.0 - a * (a - 1.0) * 0.5 + (b - a - 1.0)
    cls_f = g_idx * PAIRS_PER_GROUP + pair
    w_lo = jnp.where(first_is_lo, w_first, w_second)
    w_hi = jnp.where(first_is_lo, w_second, w_first)

    onehot = row == cls_f
    earlier_tok = (lax.broadcasted_iota(I32, (tm, tm), 0) < lax.broadcasted_iota(I32, (tm, tm), 1))
    earlier = jnp.dot(onehot.astype(BF16), earlier_tok.astype(BF16), preferred_element_type=F32)
    rank = jnp.sum(jnp.where(onehot, earlier + run_ref[:, 0:1], 0.0), axis=0, keepdims=True)
    run_ref[...] += jnp.sum(onehot.astype(F32), axis=1, keepdims=True)
    cnt_ref[...] = run_ref[...]

    def lanes_to_rows(v):
        return jnp.concatenate([v[:, c * LANES:(c + 1) * LANES] for c in range(tm // LANES)], axis=0)

    cls_ref[0] = lanes_to_rows(cls_f).astype(I32)
    rank_ref[0] = lanes_to_rows(rank).astype(I32)

    words = _pack_bf16_pair(t[:, :half], t[:, half:])
    x_subl = half // LANES
    for c in range(x_subl):
        row_ref[_subl(c, tm), :] = words[:, c * LANES:(c + 1) * LANES]
    weights = jnp.concatenate([w_lo, w_hi, jnp.zeros((LANES - 2, tm), F32)], axis=0).T
    row_ref[_subl(x_subl, tm), :] = _f32_bits(weights)
    for c in range(x_subl + 1, SUBLANES):
        row_ref[_subl(c, tm), :] = jnp.zeros((tm, LANES), U32)


def _post_call(x2, o2, wo, gain, wr, *, tm):
    T, D = x2.shape
    assert D // 2 + LANES <= SUBLANES * LANES
    n_steps = T // tm
    return pl.pallas_call(
        functools.partial(_post_kernel, n_steps=n_steps),
        grid=(n_steps,),
        in_specs=[
            pl.BlockSpec(memory_space=pl.ANY),
            pl.BlockSpec(memory_space=pl.ANY),
            pl.BlockSpec((D, D), lambda i: (0, 0)),
            pl.BlockSpec((1, D), lambda i: (0, 0)),
            pl.BlockSpec((D, 2 * LANES), lambda i: (0, 0)),
        ],
        out_specs=[
            pl.BlockSpec((tm, D), lambda i: (i, 0)),
            pl.BlockSpec((tm * SUBLANES, LANES), lambda i: (i, 0)),
            pl.BlockSpec((1, tm // LANES, LANES), lambda i: (i, 0, 0)),
            pl.BlockSpec((1, tm // LANES, LANES), lambda i: (i, 0, 0)),
            pl.BlockSpec((ROUTER_ROWS, LANES), lambda i: (0, 0)),
        ],
        out_shape=[
            jax.ShapeDtypeStruct((T, D), F32),
            jax.ShapeDtypeStruct((T * SUBLANES, LANES), U32),
            jax.ShapeDtypeStruct((T // tm, tm // LANES, LANES), I32),
            jax.ShapeDtypeStruct((T // tm, tm // LANES, LANES), I32),
            jax.ShapeDtypeStruct((ROUTER_ROWS, LANES), F32),
        ],
        scratch_shapes=[pltpu.VMEM((ROUTER_ROWS, LANES), F32),
                        pltpu.VMEM((POST_INPUT_SLOTS, tm, D), F32),
                        pltpu.VMEM((POST_INPUT_SLOTS, tm, D), BF16),
                        pltpu.SemaphoreType.DMA((2, POST_INPUT_SLOTS))],
        compiler_params=_params(("arbitrary",)),
        name="post_attn_router",
    )(x2, o2, wo, gain, wr)


def _pos_kernel(start_ref, cls_ref, rank_ref, pos_ref):
    cls = cls_ref[...]
    base = jnp.zeros(cls.shape, I32)
    for c in range(N_CLASSES):
        base = jnp.where(cls == c, start_ref[c], base)
    pos_ref[...] = base + rank_ref[...]


def _pos_call(row_start, cls, rank):
    return pl.pallas_call(
        _pos_kernel,
        in_specs=[
            pl.BlockSpec(memory_space=pltpu.SMEM),
            pl.BlockSpec(memory_space=pltpu.VMEM),
            pl.BlockSpec(memory_space=pltpu.VMEM),
        ],
        out_specs=pl.BlockSpec(memory_space=pltpu.VMEM),
        out_shape=jax.ShapeDtypeStruct(cls.shape, I32),
        name="moe_positions",
    )(row_start, cls, rank)


def _dispatch_kernel(pos_ref, row_ref, init_ref, out_ref, sem):
    del init_ref
    ts = row_ref.shape[0] // SUBLANES
    _issue_row_copies(
        ts, lambda r: pltpu.make_async_copy(
            _row_tile(row_ref, r), _row_tile(out_ref, pos_ref[0, 0, r]), sem))
    pltpu.make_async_copy(row_ref, out_ref.at[pl.ds(0, ts * SUBLANES)], sem).wait()


def _dispatch_call(pos3, rows, init, *, ts):
    T = rows.shape[0] // SUBLANES
    return pl.pallas_call(
        _dispatch_kernel,
        grid=(T // ts,),
        in_specs=[
            pl.BlockSpec((1, 1, ts), lambda i: (i, 0, 0), memory_space=pltpu.SMEM),
            pl.BlockSpec((ts * SUBLANES, LANES), lambda i: (i, 0)),
            pl.BlockSpec(memory_space=pl.ANY),
        ],
        out_specs=pl.BlockSpec(memory_space=pl.ANY),
        out_shape=jax.ShapeDtypeStruct(init.shape, init.dtype),
        scratch_shapes=[pltpu.SemaphoreType.DMA(())],
        input_output_aliases={2: 0},
        compiler_params=_params(("arbitrary",)),
        name="moe_dispatch",
    )(pos3, rows, init)


def _moe_kernel(ea_ref, eb_ref, used_ref, row_ref, w1a_ref, w3a_ref, w2a_ref,
                w1b_ref, w3b_ref, w2b_ref, y_ref):
    del ea_ref, eb_ref
    i = pl.program_id(0)
    x_subl = w1a_ref.shape[2] // (2 * LANES)

    @pl.when(i < used_ref[0])
    def _():
        tmx = y_ref.shape[0] // SUBLANES
        words = jnp.concatenate([row_ref[_subl(c, tmx), :] for c in range(x_subl)], axis=-1)
        lo, hi = _unpack_bf16_pair(words)
        x = jnp.concatenate([lo, hi], axis=-1).astype(BF16)
        wts = lax.bitcast_convert_type(row_ref[_subl(x_subl, tmx), :], F32)
        w_lo = wts[:, 0:1]
        w_hi = wts[:, 1:2]

        ff = w1a_ref.shape[3]

        def hidden(w1_ref, w3_ref):
            blocks = []
            for n0 in range(0, ff, MXU_DIM):
                z = jnp.dot(x, w1_ref[0, 0, :, n0:n0 + MXU_DIM], preferred_element_type=F32)
                u = jnp.dot(x, w3_ref[0, 0, :, n0:n0 + MXU_DIM], preferred_element_type=F32)
                blocks.append(((z * (1.0 / (1.0 + jnp.exp(-z)))) * u).astype(BF16))
            return jnp.concatenate(blocks, axis=-1)

        ha = hidden(w1a_ref, w3a_ref)
        hb = hidden(w1b_ref, w3b_ref)
        for n0 in range(0, SUBLANES * LANES, MXU_DIM):
            wide = slice(n0, n0 + MXU_DIM)
            y = (w_lo * jnp.dot(ha, w2a_ref[0, 0, :, wide], preferred_element_type=F32)
                 + w_hi * jnp.dot(hb, w2b_ref[0, 0, :, wide], preferred_element_type=F32))
            for c in range(MXU_DIM // LANES):
                y_ref[_subl(n0 // LANES + c, tmx), :] = y[:, c * LANES:(c + 1) * LANES]

    @pl.when(i >= used_ref[0])
    def _():
        y_ref[...] = jnp.zeros_like(y_ref)


def _moe_call(tile_ea, tile_eb, n_used, rows, w1, w3, w2, *, layer, tmx):
    rows_pad = rows.shape[0] // SUBLANES
    _, _, D, F = w1.shape
    assert D == SUBLANES * LANES
    wa = lambda i, ea, eb, used: (layer, ea[i], 0, 0)
    wb = lambda i, ea, eb, used: (layer, eb[i], 0, 0)
    grid_spec = pltpu.PrefetchScalarGridSpec(
        num_scalar_prefetch=3,
        grid=(rows_pad // tmx,),
        in_specs=[
            pl.BlockSpec((tmx * SUBLANES, LANES), lambda i, ea, eb, used: (i, 0)),
            pl.BlockSpec((1, 1, D, F), wa),
            pl.BlockSpec((1, 1, D, F), wa),
            pl.BlockSpec((1, 1, F, D), wa),
            pl.BlockSpec((1, 1, D, F), wb),
            pl.BlockSpec((1, 1, D, F), wb),
            pl.BlockSpec((1, 1, F, D), wb),
        ],
        out_specs=pl.BlockSpec((tmx * SUBLANES, LANES), lambda i, ea, eb, used: (i, 0)),
    )
    return pl.pallas_call(
        _moe_kernel,
        grid_spec=grid_spec,
        out_shape=jax.ShapeDtypeStruct((rows_pad * SUBLANES, LANES), F32),
        compiler_params=_params(("arbitrary",)),
        name="moe_experts",
    )(tile_ea, tile_eb, n_used, rows, w1, w3, w2, w1, w3, w2)


def _combine_kernel(pos_ref, pos_next_ref, x_ref, y_ref, out_ref, buf_ref, sem):
    tc = x_ref.shape[0]
    slot = _gathered_rows(pos_ref, pos_next_ref, y_ref, buf_ref, sem, tc)
    for c in range(SUBLANES):
        cols = slice(c * LANES, (c + 1) * LANES)
        out_ref[:, cols] = x_ref[:, cols] + buf_ref[slot, _subl(c, tc), :]


def _combine_call(pos3, xn, y, *, tc):
    T, D = xn.shape
    n = T // tc
    return pl.pallas_call(
        _combine_kernel,
        grid=(n,),
        in_specs=[
            pl.BlockSpec((1, 1, tc), lambda i: (i, 0, 0), memory_space=pltpu.SMEM),
            pl.BlockSpec((1, 1, tc), lambda i: (jnp.minimum(i + 1, n - 1), 0, 0), memory_space=pltpu.SMEM),
            pl.BlockSpec((tc, D), lambda i: (i, 0)),
            pl.BlockSpec(memory_space=pl.ANY),
        ],
        out_specs=pl.BlockSpec((tc, D), lambda i: (i, 0)),
        out_shape=jax.ShapeDtypeStruct((T, D), F32),
        scratch_shapes=[pltpu.VMEM((2, tc * SUBLANES, LANES), F32),
                        pltpu.SemaphoreType.DMA((2,))],
        compiler_params=_params(("arbitrary",)),
        name="moe_combine",
    )(pos3, pos3, xn, y)


def _rope_tables(seq):
    half = ROPE_AXIS_DIM // 2
    freqs = ROPE_THETA ** (-(jnp.arange(half, dtype=F32) * 2.0 / ROPE_AXIS_DIM))
    t = jnp.arange(seq)
    ang_row = (t // GRID_W).astype(F32)[:, None] * freqs[None, :]
    ang_col = (t % GRID_W).astype(F32)[:, None] * freqs[None, :]

    def axis_tables(ang):
        c, s = jnp.cos(ang), jnp.sin(ang)
        return jnp.concatenate([c, c], axis=-1), jnp.concatenate([-s, s], axis=-1)

    cr, sr = axis_tables(ang_row)
    cc, sc = axis_tables(ang_col)
    cos = jnp.concatenate([cr, cc], axis=-1)
    sin = jnp.concatenate([sr, sc], axis=-1)
    reps = LANES // HEAD_DIM
    return jnp.tile(cos, (1, reps)), jnp.tile(sin, (1, reps))


def _t5_bucket_table():
    rel = jnp.arange(3 * Q_BLOCK)[None, :] - Q_BLOCK - jnp.arange(Q_BLOCK)[:, None]
    nb = N_REL_BUCKETS // 2
    max_exact = nb // 2
    ret = jnp.where(rel > 0, nb, 0)
    n = jnp.abs(rel)
    large = max_exact + (jnp.log(jnp.maximum(n, 1).astype(F32) / max_exact)
                         / math.log(REL_MAX_DIST / max_exact) * (nb - max_exact)).astype(I32)
    large = jnp.minimum(large, nb - 1)
    return (ret + jnp.where(n < max_exact, n, large)).astype(I32)


def _tile_tables(counts, tmx, n_tiles):
    tiles_per_class = (counts + tmx - 1) // tmx
    tile_end = jnp.cumsum(tiles_per_class)
    row_start = ((tile_end - tiles_per_class) * tmx).astype(I32)
    n_used = tile_end[-1].astype(I32)
    tile_id = jnp.minimum(jnp.arange(n_tiles, dtype=I32), jnp.maximum(n_used - 1, 0))
    tile_cls = jnp.minimum(jnp.sum(tile_id[:, None] >= tile_end[None, :], axis=1), N_CLASSES - 1).astype(I32)
    group = tile_cls // PAIRS_PER_GROUP
    pair = tile_cls % PAIRS_PER_GROUP
    tile_ea = group * EXPERTS_PER_GROUP + jnp.asarray(PAIR_LO, I32)[pair]
    tile_eb = group * EXPERTS_PER_GROUP + jnp.asarray(PAIR_HI, I32)[pair]
    return row_start, tile_ea.astype(I32), tile_eb.astype(I32), n_used.reshape(1)


def _tile(n, pref):
    t = min(n, pref)
    assert n % t == 0, (n, pref)
    return t


def kernel(x, ln_mix, w_qkv, q_norm, k_norm, w_o, rel_bias, sinks, ln_ffn, w_group, w_expert, w1, w3, w2):
    B, S, D = x.shape
    depth = w_qkv.shape[0]
    T = B * S
    assert S % GRID_W == 0 and S % Q_BLOCK == 0 and D == N_HEADS * HEAD_DIM

    tm = _tile(S, 512)
    tq_flash = _tile(S, 2048)
    tk_flash = _tile(S, 512)
    tq_win = _tile(S, 1024)
    ts = _tile(T, 2048)
    tc = _tile(T, 1024)
    tmx = _tile(T, 256)
    n_tiles = T // tmx + N_CLASSES
    rows_pad = n_tiles * tmx

    cos, sin = _rope_tables(S)
    cos_t, sin_t = cos[:, :HEAD_DIM].T, sin[:, :HEAD_DIM].T
    bias = _bias_call(rel_bias.astype(F32), _t5_bucket_table())

    w_qkv_b = w_qkv.astype(BF16)
    w_o_b = w_o.astype(BF16)
    w1_b, w3_b, w2_b = w1.astype(BF16), w3.astype(BF16), w2.astype(BF16)
    reps = MXU_DIM // HEAD_DIM
    router_f32 = jnp.concatenate(
        [w_group, w_expert,
         jnp.zeros((depth, D, LANES - N_GROUPS - N_EXPERTS), F32)], axis=-1).astype(F32)
    router_hi = router_f32.astype(BF16)
    router_lo = (router_f32 - router_hi.astype(F32)).astype(BF16)
    router = jnp.concatenate([router_hi, router_lo], axis=-1)

    x2 = x.reshape(T, D).astype(F32)
    sorted_rows = jnp.zeros((rows_pad * SUBLANES, LANES), U32)
    pending = None
    for i in range(depth):
        outs = _qkv_call(
            x2, ln_mix[i].reshape(1, D).astype(F32), w_qkv_b[i],
            jnp.tile(q_norm[i].astype(F32), reps).reshape(1, MXU_DIM),
            jnp.tile(k_norm[i].astype(F32), reps).reshape(1, MXU_DIM),
            cos, sin, cos_t, sin_t, batch=B, seq=S, rope=(i % N_MIXERS == 0), tm=tm, combine=pending)
        q, k, v = outs[:3]
        if pending is not None:
            x2 = outs[3]
        if i % N_MIXERS == 0:
            o = _flash_call(q, k, v, tq=tq_flash, tk=tk_flash)
        else:
            o = _window_call(sinks[i // N_MIXERS].astype(F32), q, k, v, bias, tq=tq_win)
        x2, rows, cls, rank, counts = _post_call(
            x2, o.reshape(T, D), w_o_b[i], ln_ffn[i].reshape(1, D).astype(F32), router[i], tm=tm)

        row_start, tile_ea, tile_eb, n_used = _tile_tables(
            counts[:N_CLASSES, 0].astype(I32), tmx, n_tiles)
        pos = _pos_call(row_start, cls.reshape(T // LANES, LANES), rank.reshape(T // LANES, LANES))
        sorted_rows = _dispatch_call(pos.reshape(T // ts, 1, ts), rows, sorted_rows, ts=ts)
        y = _moe_call(tile_ea, tile_eb, n_used, sorted_rows, w1_b, w3_b, w2_b, layer=i, tmx=tmx)
        pending = (pos, y)
    pos, y = pending
    x2 = _combine_call(pos.reshape(T // tc, 1, tc), x2, y, tc=tc)
    return x2.reshape(B, S, D).astype(x.dtype)
```

```python
import functools
import math

import jax
import jax.numpy as jnp
from jax import lax
from jax.experimental import pallas as pl
from jax.experimental.pallas import tpu as pltpu

N_HEADS = 16
N_KV_HEADS = 4
HEAD_DIM = 64
KV_GROUP = N_HEADS // N_KV_HEADS
SCALE = HEAD_DIM ** -0.5
GRID_W = 64
ROPE_THETA = 10000.0
ROPE_AXIS_DIM = HEAD_DIM // 2
Q_BLOCK = 128
WINDOW = 128
N_MIXERS = 2
N_REL_BUCKETS = 32
REL_MAX_DIST = 128
N_GROUPS = 4
EXPERTS_PER_GROUP = 4
N_EXPERTS = N_GROUPS * EXPERTS_PER_GROUP
NORM_EPS = 1e-6
NEG_INF = -1e30
LOG2_E = 1.4426950408889634
BF16_TILE_ROWS = 16
V_ROWS = HEAD_DIM + BF16_TILE_ROWS

PAIRS_PER_GROUP = 6
N_CLASSES = N_GROUPS * PAIRS_PER_GROUP
PAIR_LO = (0, 0, 0, 1, 1, 2)
PAIR_HI = (1, 2, 3, 2, 3, 3)

LANES = 128
SUBLANES = 8
MXU_DIM = 256
VMEM_LIMIT = 48 * 1024 * 1024

F32 = jnp.float32
BF16 = jnp.bfloat16
U32 = jnp.uint32
I32 = jnp.int32


def _params(sem, vmem_limit=VMEM_LIMIT):
    return pltpu.CompilerParams(dimension_semantics=sem, vmem_limit_bytes=vmem_limit)


ROW_DMA_UNROLL = 8


def _subl(c, n_rows):
    return pl.ds(c, n_rows, stride=SUBLANES)


def _row_tile(ref, row):
    return ref.at[pl.ds(pl.multiple_of(row * SUBLANES, SUBLANES), SUBLANES)]


def _issue_row_copies(n_rows, make_copy):
    def group(g, carry):
        for u in range(ROW_DMA_UNROLL):
            make_copy(g * ROW_DMA_UNROLL + u).start(priority=u % 2)
        return carry

    lax.fori_loop(0, n_rows // ROW_DMA_UNROLL, group, 0)


def _gathered_rows(pos_ref, pos_next_ref, y_ref, buf_ref, sem, n_rows):
    i = pl.program_id(0)
    slot = i % 2

    def gather(p_ref, s):
        _issue_row_copies(
            n_rows, lambda r: pltpu.make_async_copy(
                _row_tile(y_ref, p_ref[0, 0, r]), _row_tile(buf_ref.at[s], r), sem.at[s]))

    @pl.when(i == 0)
    def _():
        gather(pos_ref, 0)

    @pl.when(i + 1 < pl.num_programs(0))
    def _():
        gather(pos_next_ref, 1 - slot)

    pltpu.make_async_copy(y_ref.at[pl.ds(0, n_rows * SUBLANES)], buf_ref.at[slot], sem.at[slot]).wait()
    return slot


def _segment_sum_matrix():
    r = lax.broadcasted_iota(I32, (MXU_DIM, MXU_DIM), 0) // HEAD_DIM
    c = lax.broadcasted_iota(I32, (MXU_DIM, MXU_DIM), 1) // HEAD_DIM
    return (r == c).astype(BF16)


def _head_rmsnorm(t, gain, seg):
    outs = []
    for c in range(t.shape[1] // MXU_DIM):
        tc = t[:, c * MXU_DIM:(c + 1) * MXU_DIM]
        sq = tc * tc
        hi = sq.astype(BF16)
        lo = (sq - hi.astype(F32)).astype(BF16)
        ss = (jnp.dot(hi, seg, preferred_element_type=F32)
              + jnp.dot(lo, seg, preferred_element_type=F32))
        outs.append(tc * lax.rsqrt(ss * (1.0 / HEAD_DIM) + NORM_EPS) * gain)
    return outs


def _rope(chunks, cos, sin):
    lane = lax.broadcasted_iota(I32, cos.shape, 1)
    upper = (lane & (ROPE_AXIS_DIM // 2)) != 0
    outs = []
    for tc in chunks:
        halves = []
        for j in range(MXU_DIM // LANES):
            xc = tc[:, j * LANES:(j + 1) * LANES]
            partner = jnp.where(upper,
                                pltpu.roll(xc, ROPE_AXIS_DIM // 2, 1),
                                pltpu.roll(xc, LANES - ROPE_AXIS_DIM // 2, 1))
            halves.append(xc * cos + partner * sin)
        outs.append(jnp.concatenate(halves, axis=-1))
    return outs


def _qkv_kernel(*refs, rope, combine):
    if combine:
        (pos_ref, pos_next_ref, x_ref, y_ref, g_ref, w_ref, qg_ref, kg_ref, cos_ref, sin_ref,
         cost_ref, sint_ref, q_ref, k_ref, v_ref, xo_ref, buf_ref, sem) = refs
        tm = x_ref.shape[0]
        slot = _gathered_rows(pos_ref, pos_next_ref, y_ref, buf_ref, sem, tm)
        x = jnp.concatenate(
            [x_ref[:, c * LANES:(c + 1) * LANES] + buf_ref[slot, _subl(c, tm), :]
             for c in range(SUBLANES)], axis=-1)
        xo_ref[...] = x
    else:
        (x_ref, g_ref, w_ref, qg_ref, kg_ref, cos_ref, sin_ref, cost_ref, sint_ref,
         q_ref, k_ref, v_ref) = refs
        x = x_ref[...]
    ms = jnp.mean(x * x, axis=-1, keepdims=True)
    h = (x * lax.rsqrt(ms + NORM_EPS) * g_ref[...]).astype(BF16)
    qkv = jnp.dot(h, w_ref[...], preferred_element_type=F32)
    nq = N_HEADS * HEAD_DIM
    nk = N_KV_HEADS * HEAD_DIM
    seg = _segment_sum_matrix()
    qs = _head_rmsnorm(qkv[:, :nq], qg_ref[...], seg)
    ks = _head_rmsnorm(qkv[:, nq:nq + nk], kg_ref[...], seg)
    if rope:
        ks = _rope(ks, cos_ref[...], sin_ref[...])
    heads_per_chunk = MXU_DIM // HEAD_DIM
    v = qkv[:, nq + nk:]
    for c, kc in enumerate(ks):
        for j in range(heads_per_chunk):
            k_ref[0, c * heads_per_chunk + j] = kc[:, j * HEAD_DIM:(j + 1) * HEAD_DIM].astype(BF16)
    half = ROPE_AXIS_DIM // 2
    for c, qc in enumerate(qs):
        qt = (qc * (SCALE * LOG2_E)).T
        if rope:
            partner = jnp.concatenate(
                [qt[r0 + (half if (r0 // half) % 2 == 0 else -half):][:half]
                 for r0 in range(0, MXU_DIM, half)], axis=0)
            cos_t = jnp.concatenate([cost_ref[...]] * heads_per_chunk, axis=0)
            sin_t = jnp.concatenate([sint_ref[...]] * heads_per_chunk, axis=0)
            qt = qt * cos_t + partner * sin_t
        for j in range(heads_per_chunk):
            q_ref[0, c * heads_per_chunk + j] = qt[j * HEAD_DIM:(j + 1) * HEAD_DIM].astype(BF16)
    vt = v.T
    ones = jnp.ones((V_ROWS - HEAD_DIM, vt.shape[1]), F32)
    for j in range(N_KV_HEADS):
        v_ref[0, j] = jnp.concatenate([vt[j * HEAD_DIM:(j + 1) * HEAD_DIM], ones], axis=0).astype(BF16)


def _qkv_call(x2, gain, w, qg, kg, cos, sin, cos_t, sin_t, *, batch, seq, rope, tm, combine=None):
    T, D = x2.shape
    spt = seq // tm
    n = T // tm
    qkv_dim = w.shape[1]
    kern = functools.partial(_qkv_kernel, rope=rope, combine=combine is not None)
    head_map = lambda i: (i // spt, 0, i % spt, 0)
    feat_map = lambda i: (i // spt, 0, 0, i % spt)
    in_specs = [
        pl.BlockSpec((tm, D), lambda i: (i, 0)),
        pl.BlockSpec((1, D), lambda i: (0, 0)),
        pl.BlockSpec((D, qkv_dim), lambda i: (0, 0)),
        pl.BlockSpec((1, MXU_DIM), lambda i: (0, 0)),
        pl.BlockSpec((1, MXU_DIM), lambda i: (0, 0)),
        pl.BlockSpec((tm, LANES), lambda i: (i % spt, 0)),
        pl.BlockSpec((tm, LANES), lambda i: (i % spt, 0)),
        pl.BlockSpec((HEAD_DIM, tm), lambda i: (0, i % spt)),
        pl.BlockSpec((HEAD_DIM, tm), lambda i: (0, i % spt)),
    ]
    out_specs = [
        pl.BlockSpec((1, N_HEADS, HEAD_DIM, tm), feat_map),
        pl.BlockSpec((1, N_KV_HEADS, tm, HEAD_DIM), head_map),
        pl.BlockSpec((1, N_KV_HEADS, V_ROWS, tm), feat_map),
    ]
    out_shape = [
        jax.ShapeDtypeStruct((batch, N_HEADS, HEAD_DIM, seq), BF16),
        jax.ShapeDtypeStruct((batch, N_KV_HEADS, seq, HEAD_DIM), BF16),
        jax.ShapeDtypeStruct((batch, N_KV_HEADS, V_ROWS, seq), BF16),
    ]
    args = [x2, gain, w, qg, kg, cos, sin, cos_t, sin_t]
    scratch = []
    if combine is not None:
        pos, y = combine
        pos3 = pos.reshape(n, 1, tm)
        in_specs = [
            pl.BlockSpec((1, 1, tm), lambda i: (i, 0, 0), memory_space=pltpu.SMEM),
            pl.BlockSpec((1, 1, tm), lambda i: (jnp.minimum(i + 1, n - 1), 0, 0), memory_space=pltpu.SMEM),
            in_specs[0],
            pl.BlockSpec(memory_space=pl.ANY),
        ] + in_specs[1:]
        out_specs.append(pl.BlockSpec((tm, D), lambda i: (i, 0)))
        out_shape.append(jax.ShapeDtypeStruct((T, D), F32))
        args = [pos3, pos3, x2, y] + args[1:]
        scratch = [pltpu.VMEM((2, tm * SUBLANES, LANES), F32), pltpu.SemaphoreType.DMA((2,))]
    return pl.pallas_call(
        kern,
        grid=(n,),
        in_specs=in_specs,
        out_specs=out_specs,
        out_shape=out_shape,
        scratch_shapes=scratch,
        compiler_params=_params(("arbitrary",)),
        name="qkv_proj",
    )(*args)


FLASH_RING = 2
FLASH_VMEM_LIMIT = 56 * 1024 * 1024


def _heads_to_rows(o, width):
    pairs = []
    for g in range(0, KV_GROUP, 2):
        two_heads = jnp.concatenate(
            [o[:, g * width:(g + 1) * width], o[:, (g + 1) * width:(g + 2) * width]], axis=0)
        pairs.append(two_heads.T)
    return jnp.concatenate(pairs, axis=-1)


def _flash_kernel(qt_ref, k_ref, vt_ref, o_ref, acc_ref, *s_refs, tk):
    ring = len(s_refs)
    tq = qt_ref.shape[3]
    seq = k_ref.shape[2]
    n_chunks = seq // tk
    key_tiles = tk // MXU_DIM
    units = [(g, g * tq + h * MXU_DIM, h * MXU_DIM) for g in range(KV_GROUP) for h in range(tq // MXU_DIM)]

    def tile(c0, kt):
        return (slice(kt * MXU_DIM, (kt + 1) * MXU_DIM), slice(c0, c0 + MXU_DIM))

    def qk_unit(chunk, unit, dst_ref):
        g, c0, q0 = unit
        mx = None
        for kt in range(key_tiles):
            off = pl.multiple_of(chunk * tk + kt * MXU_DIM, MXU_DIM)
            s = jnp.dot(k_ref[0, 0, pl.ds(off, MXU_DIM), :], qt_ref[0, g, :, q0:q0 + MXU_DIM],
                        preferred_element_type=F32)
            dst_ref[tile(c0, kt)] = s
            t = jnp.max(s, axis=0, keepdims=True)
            mx = t if mx is None else jnp.maximum(mx, t)
        return mx

    def attend_unit(chunk, unit, src_ref, mx, m):
        _, c0, _ = unit
        m_new = jnp.maximum(m, mx)
        pv = None
        for kt in range(key_tiles):
            off = pl.multiple_of(chunk * tk + kt * MXU_DIM, MXU_DIM)
            p = jnp.exp2(src_ref[tile(c0, kt)] - m_new).astype(BF16)
            d = jnp.dot(vt_ref[0, 0, :, pl.ds(off, MXU_DIM)], p, preferred_element_type=F32)
            pv = d if pv is None else pv + d
        cols = slice(c0, c0 + MXU_DIM)
        acc_ref[:, cols] = jnp.exp2(m - m_new) * acc_ref[:, cols] + pv
        return m_new

    def step(chunk, u, mx, m, last=False):
        mx_next, m_out = [], []
        for i, unit in enumerate(units):
            if not last:
                mx_next.append(qk_unit(chunk + 1, unit, s_refs[(u + 1) % ring]))
            m_out.append(attend_unit(chunk, unit, s_refs[u], mx[i], m[i]))
        return tuple(mx_next), tuple(m_out)

    def body(j, carry):
        mx, m = carry
        for u in range(ring):
            mx, m = step(ring * j + u, u, mx, m)
        return mx, m

    acc_ref[...] = jnp.zeros_like(acc_ref)
    m = tuple(jnp.full((1, MXU_DIM), NEG_INF, F32) for _ in units)
    mx = tuple(qk_unit(0, unit, s_refs[0]) for unit in units)
    loops = (n_chunks - 1) // ring
    mx, m = lax.fori_loop(0, loops, body, (mx, m))
    for c in range(loops * ring, n_chunks - 1):
        mx, m = step(c, c % ring, mx, m)
    step(n_chunks - 1, (n_chunks - 1) % ring, mx, m, last=True)
    acc = acc_ref[...]
    o_ref[0] = _heads_to_rows(acc[:HEAD_DIM] / acc[HEAD_DIM:HEAD_DIM + 1], tq).astype(BF16)


def _flash_call(q, k, v, *, tq, tk):
    B, _, _, S = q.shape
    kern = functools.partial(_flash_kernel, tk=tk)
    return pl.pallas_call(
        kern,
        grid=(B, N_KV_HEADS, S // tq),
        in_specs=[
            pl.BlockSpec((1, KV_GROUP, HEAD_DIM, tq), lambda b, h, i: (b, h, 0, i)),
            pl.BlockSpec((1, 1, S, HEAD_DIM), lambda b, h, i: (b, h, 0, 0)),
            pl.BlockSpec((1, 1, V_ROWS, S), lambda b, h, i: (b, h, 0, 0)),
        ],
        out_specs=pl.BlockSpec((1, tq, KV_GROUP * HEAD_DIM), lambda b, h, i: (b, i, h)),
        out_shape=jax.ShapeDtypeStruct((B, S, N_HEADS * HEAD_DIM), BF16),
        scratch_shapes=([pltpu.VMEM((V_ROWS, KV_GROUP * tq), F32)]
                        + [pltpu.VMEM((tk, KV_GROUP * tq), F32)] * FLASH_RING),
        compiler_params=_params(("parallel", "parallel", "parallel"), FLASH_VMEM_LIMIT),
        name="flash_global",
    )(q, k, v)


EDGE_VARIANTS = 4


def _bias_kernel(rb_ref, bucket_ref, out_ref):
    h = pl.program_id(0)
    b = bucket_ref[...]
    bias = jnp.zeros(b.shape, F32)
    for j in range(N_REL_BUCKETS):
        bias = jnp.where(b == j, rb_ref[h, j], bias)
    qpos = lax.broadcasted_iota(I32, b.shape, 0)
    col = lax.broadcasted_iota(I32, b.shape, 1)
    in_band = jnp.abs(col - Q_BLOCK - qpos) <= WINDOW
    for e in range(EDGE_VARIANTS):
        valid = in_band
        if e & 1:
            valid = valid & (col >= Q_BLOCK)
        if e & 2:
            valid = valid & (col < 2 * Q_BLOCK)
        out_ref[e, 0] = (jnp.where(valid, bias, NEG_INF) * LOG2_E).T


def _bias_call(rel_bias, bucket):
    table = (bucket.shape[1], bucket.shape[0])
    return pl.pallas_call(
        _bias_kernel,
        grid=(N_HEADS,),
        in_specs=[
            pl.BlockSpec(memory_space=pltpu.SMEM),
            pl.BlockSpec(bucket.shape, lambda h: (0, 0)),
        ],
        out_specs=pl.BlockSpec((EDGE_VARIANTS, 1) + table, lambda h: (0, h, 0, 0)),
        out_shape=jax.ShapeDtypeStruct((EDGE_VARIANTS, N_HEADS) + table, F32),
        compiler_params=_params(("parallel",)),
        name="rel_bias_table",
    )(rel_bias, bucket)


def _window_kernel(sink_ref, qt_ref, k_ref, vt_ref, bias_ref, o_ref, *s_refs, blocks_per_tile):
    hk = pl.program_id(1)
    tile = pl.program_id(2)
    nb = k_ref.shape[2] // Q_BLOCK
    ring = len(s_refs)
    pairs = range(KV_GROUP // 2)

    sinks = []
    for p in pairs:
        first = lax.broadcasted_iota(I32, (1, MXU_DIM), 1) < Q_BLOCK
        sinks.append(jnp.where(first, sink_ref[hk * KV_GROUP + 2 * p],
                               sink_ref[hk * KV_GROUP + 2 * p + 1]) * LOG2_E)

    def key_starts(j):
        n = tile * blocks_per_tile + j
        return n, [pl.multiple_of(jnp.maximum(n - 1, 0) * Q_BLOCK, Q_BLOCK),
                   pl.multiple_of(n * Q_BLOCK, Q_BLOCK),
                   pl.multiple_of(jnp.minimum(n + 1, nb - 1) * Q_BLOCK, Q_BLOCK)]

    def qk_block(j, dst_ref):
        n, starts = key_starts(j)
        kcat = jnp.concatenate([k_ref[0, 0, pl.ds(s, Q_BLOCK), :] for s in starts], axis=0)
        edge = (n == 0).astype(I32) + 2 * (n == nb - 1).astype(I32)
        maxes = []
        for p in pairs:
            heads = (2 * p, 2 * p + 1)
            qt = jnp.concatenate(
                [qt_ref[0, h, :, j * Q_BLOCK:(j + 1) * Q_BLOCK] for h in heads], axis=-1)
            bias = jnp.concatenate([bias_ref[edge, h] for h in heads], axis=-1)
            s = jnp.dot(kcat, qt, preferred_element_type=F32) + bias
            dst_ref[:, p * MXU_DIM:(p + 1) * MXU_DIM] = s
            maxes.append(jnp.max(s, axis=0, keepdims=True))
        return maxes

    def attend_block(j, src_ref, maxes):
        _, starts = key_starts(j)
        vcat = jnp.concatenate([vt_ref[0, 0, :, pl.ds(s, Q_BLOCK)] for s in starts], axis=-1)
        for p in pairs:
            m = jnp.maximum(maxes[p], sinks[p])
            pr = jnp.exp2(src_ref[:, p * MXU_DIM:(p + 1) * MXU_DIM] - m).astype(BF16)
            ov = jnp.dot(vcat, pr, preferred_element_type=F32)
            o = ov[:HEAD_DIM] / (ov[HEAD_DIM:HEAD_DIM + 1] + jnp.exp2(sinks[p] - m))
            two_heads = jnp.concatenate([o[:, :Q_BLOCK], o[:, Q_BLOCK:]], axis=0)
            o_ref[0, j * Q_BLOCK:(j + 1) * Q_BLOCK, p * LANES:(p + 1) * LANES] = two_heads.T.astype(BF16)

    maxes = qk_block(0, s_refs[0])
    for j in range(blocks_per_tile):
        nxt = qk_block(j + 1, s_refs[(j + 1) % ring]) if j + 1 < blocks_per_tile else None
        attend_block(j, s_refs[j % ring], maxes)
        maxes = nxt


def _window_call(sink, q, k, v, bias, *, tq):
    B, _, _, S = q.shape
    kern = functools.partial(_window_kernel, blocks_per_tile=tq // Q_BLOCK)
    return pl.pallas_call(
        kern,
        grid=(B, N_KV_HEADS, S // tq),
        in_specs=[
            pl.BlockSpec(memory_space=pltpu.SMEM),
            pl.BlockSpec((1, KV_GROUP, HEAD_DIM, tq), lambda b, h, i: (b, h, 0, i)),
            pl.BlockSpec((1, 1, S, HEAD_DIM), lambda b, h, i: (b, h, 0, 0)),
            pl.BlockSpec((1, 1, V_ROWS, S), lambda b, h, i: (b, h, 0, 0)),
            pl.BlockSpec((EDGE_VARIANTS, KV_GROUP, 3 * Q_BLOCK, Q_BLOCK), lambda b, h, i: (0, h, 0, 0)),
        ],
        out_specs=pl.BlockSpec((1, tq, KV_GROUP * HEAD_DIM), lambda b, h, i: (b, i, h)),
        out_shape=jax.ShapeDtypeStruct((B, S, N_HEADS * HEAD_DIM), BF16),
        scratch_shapes=[pltpu.VMEM((3 * Q_BLOCK, KV_GROUP * Q_BLOCK), F32)] * 2,
        compiler_params=_params(("parallel", "parallel", "parallel")),
        name="window_attn",
    )(sink, q, k, v, bias)


def _f32_bits(x):
    return lax.bitcast_convert_type(x, U32)


def _pack_bf16_pair(lo_half, hi_half):
    lo = lax.shift_right_logical(_f32_bits(lo_half.astype(BF16).astype(F32)), jnp.uint32(16))
    hi = _f32_bits(hi_half.astype(BF16).astype(F32)) & jnp.uint32(0xFFFF0000)
    return lo | hi


def _unpack_bf16_pair(words):
    lo = lax.bitcast_convert_type(lax.shift_left(words, jnp.uint32(16)), F32)
    hi = lax.bitcast_convert_type(words & jnp.uint32(0xFFFF0000), F32)
    return lo, hi


ROUTER_ROWS = 32


POST_INPUT_SLOTS = 3


def _post_kernel(x_hbm, o_hbm, wo_ref, g_ref, wr_ref,
                 xn_ref, row_ref, cls_ref, rank_ref, cnt_ref, run_ref, xbuf, obuf, sem, *, n_steps):
    step = pl.program_id(0)
    tm = xn_ref.shape[0]
    half = xn_ref.shape[1] // 2

    def tile_copies(s, slot):
        rows = pl.ds(pl.multiple_of(s * tm, tm), tm)
        return (pltpu.make_async_copy(x_hbm.at[rows], xbuf.at[slot], sem.at[0, slot]),
                pltpu.make_async_copy(o_hbm.at[rows], obuf.at[slot], sem.at[1, slot]))

    def fetch(s, slot):
        for copy in tile_copies(s, slot):
            copy.start()

    @pl.when(step == 0)
    def _():
        run_ref[...] = jnp.zeros_like(run_ref)
        for s in range(min(POST_INPUT_SLOTS - 1, n_steps)):
            fetch(s, s)

    @pl.when(step + POST_INPUT_SLOTS - 1 < n_steps)
    def _():
        ahead = step + POST_INPUT_SLOTS - 1
        fetch(ahead, ahead % POST_INPUT_SLOTS)

    slot = step % POST_INPUT_SLOTS
    for copy in tile_copies(step, slot):
        copy.wait()

    xn = xbuf[slot] + jnp.dot(obuf[slot], wo_ref[...], preferred_element_type=F32)
    xn_ref[...] = xn
    ms = jnp.mean(xn * xn, axis=-1, keepdims=True)
    t = xn * lax.rsqrt(ms + NORM_EPS) * g_ref[...]
    t_hi = t.astype(BF16)
    t_lo = (t - t_hi.astype(F32)).astype(BF16)
    hi_both = jnp.dot(t_hi, wr_ref[...], preferred_element_type=F32)
    lo_hi = jnp.dot(t_lo, wr_ref[:, :LANES], preferred_element_type=F32)
    logits = hi_both[:, :LANES] + (hi_both[:, LANES:] + lo_hi)

    lt = logits.T[:ROUTER_ROWS]
    row = lax.broadcasted_iota(I32, (ROUTER_ROWS, tm), 0).astype(F32)
    big = jnp.float32(ROUTER_ROWS)

    def first_argmax(vals):
        top = jnp.max(vals, axis=0, keepdims=True)
        idx = jnp.min(jnp.where(vals == top, row, big), axis=0, keepdims=True)
        return top, idx

    is_group = row < N_GROUPS
    g_top, g_idx = first_argmax(jnp.where(is_group, lt, -jnp.inf))
    g_sum = jnp.sum(jnp.where(is_group, jnp.exp(lt - g_top), 0.0), axis=0, keepdims=True)
    g_prob = 1.0 / g_sum

    base = N_GROUPS + EXPERTS_PER_GROUP * g_idx
    in_group = (row >= base) & (row < base + EXPERTS_PER_GROUP)
    e_logits = jnp.where(in_group, lt, -jnp.inf)
    e1, i1 = first_argmax(e_logits)
    e2, i2 = first_argmax(jnp.where(row == i1, -jnp.inf, e_logits))
    r = jnp.exp(e2 - e1)
    w_first = (1.0 / (1.0 + r)) * g_prob
    w_second = (r / (1.0 + r)) * g_prob

    j1 = i1 - base
    j2 = i2 - base
    first_is_lo = j1 < j2
    a = jnp.minimum(j1, j2)
    b = jnp.maximum(j1, j2)
    pair = a * 3.0 - a * (a - 1.0) * 0.5 + (b - a - 1.0)
    cls_f = g_idx * PAIRS_PER_GROUP + pair
    w_lo = jnp.where(first_is_lo, w_first, w_second)
    w_hi = jnp.where(first_is_lo, w_second, w_first)

    onehot = row == cls_f
    earlier_tok = (lax.broadcasted_iota(I32, (tm, tm), 0) < lax.broadcasted_iota(I32, (tm, tm), 1))
    earlier = jnp.dot(onehot.astype(BF16), earlier_tok.astype(BF16), preferred_element_type=F32)
    rank = jnp.sum(jnp.where(onehot, earlier + run_ref[:, 0:1], 0.0), axis=0, keepdims=True)
    run_ref[...] += jnp.sum(onehot.astype(F32), axis=1, keepdims=True)
    cnt_ref[...] = run_ref[...]

    def lanes_to_rows(v):
        return jnp.concatenate([v[:, c * LANES:(c + 1) * LANES] for c in range(tm // LANES)], axis=0)

    cls_ref[0] = lanes_to_rows(cls_f).astype(I32)
    rank_ref[0] = lanes_to_rows(rank).astype(I32)

    words = _pack_bf16_pair(t[:, :half], t[:, half:])
    x_subl = half // LANES
    for c in range(x_subl):
        row_ref[_subl(c, tm), :] = words[:, c * LANES:(c + 1) * LANES]
    weights = jnp.concatenate([w_lo, w_hi, jnp.zeros((LANES - 2, tm), F32)], axis=0).T
    row_ref[_subl(x_subl, tm), :] = _f32_bits(weights)
    for c in range(x_subl + 1, SUBLANES):
        row_ref[_subl(c, tm), :] = jnp.zeros((tm, LANES), U32)


def _post_call(x2, o2, wo, gain, wr, *, tm):
    T, D = x2.shape
    assert D // 2 + LANES <= SUBLANES * LANES
    n_steps = T // tm
    return pl.pallas_call(
        functools.partial(_post_kernel, n_steps=n_steps),
        grid=(n_steps,),
        in_specs=[
            pl.BlockSpec(memory_space=pl.ANY),
            pl.BlockSpec(memory_space=pl.ANY),
            pl.BlockSpec((D, D), lambda i: (0, 0)),
            pl.BlockSpec((1, D), lambda i: (0, 0)),
            pl.BlockSpec((D, 2 * LANES), lambda i: (0, 0)),
        ],
        out_specs=[
            pl.BlockSpec((tm, D), lambda i: (i, 0)),
            pl.BlockSpec((tm * SUBLANES, LANES), lambda i: (i, 0)),
            pl.BlockSpec((1, tm // LANES, LANES), lambda i: (i, 0, 0)),
            pl.BlockSpec((1, tm // LANES, LANES), lambda i: (i, 0, 0)),
            pl.BlockSpec((ROUTER_ROWS, LANES), lambda i: (0, 0)),
        ],
        out_shape=[
            jax.ShapeDtypeStruct((T, D), F32),
            jax.ShapeDtypeStruct((T * SUBLANES, LANES), U32),
            jax.ShapeDtypeStruct((T // tm, tm // LANES, LANES), I32),
            jax.ShapeDtypeStruct((T // tm, tm // LANES, LANES), I32),
            jax.ShapeDtypeStruct((ROUTER_ROWS, LANES), F32),
        ],
        scratch_shapes=[pltpu.VMEM((ROUTER_ROWS, LANES), F32),
                        pltpu.VMEM((POST_INPUT_SLOTS, tm, D), F32),
                        pltpu.VMEM((POST_INPUT_SLOTS, tm, D), BF16),
                        pltpu.SemaphoreType.DMA((2, POST_INPUT_SLOTS))],
        compiler_params=_params(("arbitrary",)),
        name="post_attn_router",
    )(x2, o2, wo, gain, wr)


def _pos_kernel(start_ref, cls_ref, rank_ref, pos_ref):
    cls = cls_ref[...]
    base = jnp.zeros(cls.shape, I32)
    for c in range(N_CLASSES):
        base = jnp.where(cls == c, start_ref[c], base)
    pos_ref[...] = base + rank_ref[...]


def _pos_call(row_start, cls, rank):
    return pl.pallas_call(
        _pos_kernel,
        in_specs=[
            pl.BlockSpec(memory_space=pltpu.SMEM),
            pl.BlockSpec(memory_space=pltpu.VMEM),
            pl.BlockSpec(memory_space=pltpu.VMEM),
        ],
        out_specs=pl.BlockSpec(memory_space=pltpu.VMEM),
        out_shape=jax.ShapeDtypeStruct(cls.shape, I32),
        name="moe_positions",
    )(row_start, cls, rank)


def _dispatch_kernel(pos_ref, row_ref, init_ref, out_ref, sem):
    del init_ref
    ts = row_ref.shape[0] // SUBLANES
    _issue_row_copies(
        ts, lambda r: pltpu.make_async_copy(
            _row_tile(row_ref, r), _row_tile(out_ref, pos_ref[0, 0, r]), sem))
    pltpu.make_async_copy(row_ref, out_ref.at[pl.ds(0, ts * SUBLANES)], sem).wait()


def _dispatch_call(pos3, rows, init, *, ts):
    T = rows.shape[0] // SUBLANES
    return pl.pallas_call(
        _dispatch_kernel,
        grid=(T // ts,),
        in_specs=[
            pl.BlockSpec((1, 1, ts), lambda i: (i, 0, 0), memory_space=pltpu.SMEM),
            pl.BlockSpec((ts * SUBLANES, LANES), lambda i: (i, 0)),
            pl.BlockSpec(memory_space=pl.ANY),
        ],
        out_specs=pl.BlockSpec(memory_space=pl.ANY),
        out_shape=jax.ShapeDtypeStruct(init.shape, init.dtype),
        scratch_shapes=[pltpu.SemaphoreType.DMA(())],
        input_output_aliases={2: 0},
        compiler_params=_params(("arbitrary",)),
        name="moe_dispatch",
    )(pos3, rows, init)


def _moe_kernel(ea_ref, eb_ref, used_ref, row_ref, w1a_ref, w3a_ref, w2a_ref,
                w1b_ref, w3b_ref, w2b_ref, y_ref):
    del ea_ref, eb_ref
    i = pl.program_id(0)
    x_subl = w1a_ref.shape[2] // (2 * LANES)

    @pl.when(i < used_ref[0])
    def _():
        tmx = y_ref.shape[0] // SUBLANES
        words = jnp.concatenate([row_ref[_subl(c, tmx), :] for c in range(x_subl)], axis=-1)
        lo, hi = _unpack_bf16_pair(words)
        x = jnp.concatenate([lo, hi], axis=-1).astype(BF16)
        wts = lax.bitcast_convert_type(row_ref[_subl(x_subl, tmx), :], F32)
        w_lo = wts[:, 0:1]
        w_hi = wts[:, 1:2]

        ff = w1a_ref.shape[3]

        def hidden(w1_ref, w3_ref):
            blocks = []
            for n0 in range(0, ff, MXU_DIM):
                z = jnp.dot(x, w1_ref[0, 0, :, n0:n0 + MXU_DIM], preferred_element_type=F32)
                u = jnp.dot(x, w3_ref[0, 0, :, n0:n0 + MXU_DIM], preferred_element_type=F32)
                blocks.append(((z * (1.0 / (1.0 + jnp.exp(-z)))) * u).astype(BF16))
            return jnp.concatenate(blocks, axis=-1)

        ha = hidden(w1a_ref, w3a_ref)
        hb = hidden(w1b_ref, w3b_ref)
        for n0 in range(0, SUBLANES * LANES, MXU_DIM):
            wide = slice(n0, n0 + MXU_DIM)
            y = (w_lo * jnp.dot(ha, w2a_ref[0, 0, :, wide], preferred_element_type=F32)
                 + w_hi * jnp.dot(hb, w2b_ref[0, 0, :, wide], preferred_element_type=F32))
            for c in range(MXU_DIM // LANES):
                y_ref[_subl(n0 // LANES + c, tmx), :] = y[:, c * LANES:(c + 1) * LANES]

    @pl.when(i >= used_ref[0])
    def _():
        y_ref[...] = jnp.zeros_like(y_ref)


def _moe_call(tile_ea, tile_eb, n_used, rows, w1, w3, w2, *, layer, tmx):
    rows_pad = rows.shape[0] // SUBLANES
    _, _, D, F = w1.shape
    assert D == SUBLANES * LANES
    wa = lambda i, ea, eb, used: (layer, ea[i], 0, 0)
    wb = lambda i, ea, eb, used: (layer, eb[i], 0, 0)
    grid_spec = pltpu.PrefetchScalarGridSpec(
        num_scalar_prefetch=3,
        grid=(rows_pad // tmx,),
        in_specs=[
            pl.BlockSpec((tmx * SUBLANES, LANES), lambda i, ea, eb, used: (i, 0)),
            pl.BlockSpec((1, 1, D, F), wa),
            pl.BlockSpec((1, 1, D, F), wa),
            pl.BlockSpec((1, 1, F, D), wa),
            pl.BlockSpec((1, 1, D, F), wb),
            pl.BlockSpec((1, 1, D, F), wb),
            pl.BlockSpec((1, 1, F, D), wb),
        ],
        out_specs=pl.BlockSpec((tmx * SUBLANES, LANES), lambda i, ea, eb, used: (i, 0)),
    )
    return pl.pallas_call(
        _moe_kernel,
        grid_spec=grid_spec,
        out_shape=jax.ShapeDtypeStruct((rows_pad * SUBLANES, LANES), F32),
        compiler_params=_params(("arbitrary",)),
        name="moe_experts",
    )(tile_ea, tile_eb, n_used, rows, w1, w3, w2, w1, w3, w2)


def _combine_kernel(pos_ref, pos_next_ref, x_ref, y_ref, out_ref, buf_ref, sem):
    tc = x_ref.shape[0]
    slot = _gathered_rows(pos_ref, pos_next_ref, y_ref, buf_ref, sem, tc)
    for c in range(SUBLANES):
        cols = slice(c * LANES, (c + 1) * LANES)
        out_ref[:, cols] = x_ref[:, cols] + buf_ref[slot, _subl(c, tc), :]


def _combine_call(pos3, xn, y, *, tc):
    T, D = xn.shape
    n = T // tc
    return pl.pallas_call(
        _combine_kernel,
        grid=(n,),
        in_specs=[
            pl.BlockSpec((1, 1, tc), lambda i: (i, 0, 0), memory_space=pltpu.SMEM),
            pl.BlockSpec((1, 1, tc), lambda i: (jnp.minimum(i + 1, n - 1), 0, 0), memory_space=pltpu.SMEM),
            pl.BlockSpec((tc, D), lambda i: (i, 0)),
            pl.BlockSpec(memory_space=pl.ANY),
        ],
        out_specs=pl.BlockSpec((tc, D), lambda i: (i, 0)),
        out_shape=jax.ShapeDtypeStruct((T, D), F32),
        scratch_shapes=[pltpu.VMEM((2, tc * SUBLANES, LANES), F32),
                        pltpu.SemaphoreType.DMA((2,))],
        compiler_params=_params(("arbitrary",)),
        name="moe_combine",
    )(pos3, pos3, xn, y)


def _rope_tables(seq):
    half = ROPE_AXIS_DIM // 2
    freqs = ROPE_THETA ** (-(jnp.arange(half, dtype=F32) * 2.0 / ROPE_AXIS_DIM))
    t = jnp.arange(seq)
    ang_row = (t // GRID_W).astype(F32)[:, None] * freqs[None, :]
    ang_col = (t % GRID_W).astype(F32)[:, None] * freqs[None, :]

    def axis_tables(ang):
        c, s = jnp.cos(ang), jnp.sin(ang)
        return jnp.concatenate([c, c], axis=-1), jnp.concatenate([-s, s], axis=-1)

    cr, sr = axis_tables(ang_row)
    cc, sc = axis_tables(ang_col)
    cos = jnp.concatenate([cr, cc], axis=-1)
    sin = jnp.concatenate([sr, sc], axis=-1)
    reps = LANES // HEAD_DIM
    return jnp.tile(cos, (1, reps)), jnp.tile(sin, (1, reps))


def _t5_bucket_table():
    rel = jnp.arange(3 * Q_BLOCK)[None, :] - Q_BLOCK - jnp.arange(Q_BLOCK)[:, None]
    nb = N_REL_BUCKETS // 2
    max_exact = nb // 2
    ret = jnp.where(rel > 0, nb, 0)
    n = jnp.abs(rel)
    large = max_exact + (jnp.log(jnp.maximum(n, 1).astype(F32) / max_exact)
                         / math.log(REL_MAX_DIST / max_exact) * (nb - max_exact)).astype(I32)
    large = jnp.minimum(large, nb - 1)
    return (ret + jnp.where(n < max_exact, n, large)).astype(I32)


def _tile_tables(counts, tmx, n_tiles):
    tiles_per_class = (counts + tmx - 1) // tmx
    tile_end = jnp.cumsum(tiles_per_class)
    row_start = ((tile_end - tiles_per_class) * tmx).astype(I32)
    n_used = tile_end[-1].astype(I32)
    tile_id = jnp.minimum(jnp.arange(n_tiles, dtype=I32), jnp.maximum(n_used - 1, 0))
    tile_cls = jnp.minimum(jnp.sum(tile_id[:, None] >= tile_end[None, :], axis=1), N_CLASSES - 1).astype(I32)
    group = tile_cls // PAIRS_PER_GROUP
    pair = tile_cls % PAIRS_PER_GROUP
    tile_ea = group * EXPERTS_PER_GROUP + jnp.asarray(PAIR_LO, I32)[pair]
    tile_eb = group * EXPERTS_PER_GROUP + jnp.asarray(PAIR_HI, I32)[pair]
    return row_start, tile_ea.astype(I32), tile_eb.astype(I32), n_used.reshape(1)


def _tile(n, pref):
    t = min(n, pref)
    assert n % t == 0, (n, pref)
    return t


def kernel(x, ln_mix, w_qkv, q_norm, k_norm, w_o, rel_bias, sinks, ln_ffn, w_group, w_expert, w1, w3, w2):
    B, S, D = x.shape
    depth = w_qkv.shape[0]
    T = B * S
    assert S % GRID_W == 0 and S % Q_BLOCK == 0 and D == N_HEADS * HEAD_DIM

    tm = _tile(S, 512)
    tq_flash = _tile(S, 2048)
    tk_flash = _tile(S, 512)
    tq_win = _tile(S, 2048)
    ts = _tile(T, 2048)
    tc = _tile(T, 1024)
    tmx = _tile(T, 256)
    n_tiles = T // tmx + N_CLASSES
    rows_pad = n_tiles * tmx

    cos, sin = _rope_tables(S)
    cos_t, sin_t = cos[:, :HEAD_DIM].T, sin[:, :HEAD_DIM].T
    bias = _bias_call(rel_bias.astype(F32), _t5_bucket_table())

    w_qkv_b = w_qkv.astype(BF16)
    w_o_b = w_o.astype(BF16)
    w1_b, w3_b, w2_b = w1.astype(BF16), w3.astype(BF16), w2.astype(BF16)
    reps = MXU_DIM // HEAD_DIM
    router_f32 = jnp.concatenate(
        [w_group, w_expert,
         jnp.zeros((depth, D, LANES - N_GROUPS - N_EXPERTS), F32)], axis=-1).astype(F32)
    router_hi = router_f32.astype(BF16)
    router_lo = (router_f32 - router_hi.astype(F32)).astype(BF16)
    router = jnp.concatenate([router_hi, router_lo], axis=-1)

    x2 = x.reshape(T, D).astype(F32)
    sorted_rows = jnp.zeros((rows_pad * SUBLANES, LANES), U32)
    pending = None
    for i in range(depth):
        outs = _qkv_call(
            x2, ln_mix[i].reshape(1, D).astype(F32), w_qkv_b[i],
            jnp.tile(q_norm[i].astype(F32), reps).reshape(1, MXU_DIM),
            jnp.tile(k_norm[i].astype(F32), reps).reshape(1, MXU_DIM),
            cos, sin, cos_t, sin_t, batch=B, seq=S, rope=(i % N_MIXERS == 0), tm=tm, combine=pending)
        q, k, v = outs[:3]
        if pending is not None:
            x2 = outs[3]
        if i % N_MIXERS == 0:
            o = _flash_call(q, k, v, tq=tq_flash, tk=tk_flash)
        else:
            o = _window_call(sinks[i // N_MIXERS].astype(F32), q, k, v, bias, tq=tq_win)
        x2, rows, cls, rank, counts = _post_call(
            x2, o.reshape(T, D), w_o_b[i], ln_ffn[i].reshape(1, D).astype(F32), router[i], tm=tm)

        row_start, tile_ea, tile_eb, n_used = _tile_tables(
            counts[:N_CLASSES, 0].astype(I32), tmx, n_tiles)
        pos = _pos_call(row_start, cls.reshape(T // LANES, LANES), rank.reshape(T // LANES, LANES))
        sorted_rows = _dispatch_call(pos.reshape(T // ts, 1, ts), rows, sorted_rows, ts=ts)
        y = _moe_call(tile_ea, tile_eb, n_used, sorted_rows, w1_b, w3_b, w2_b, layer=i, tmx=tmx)
        pending = (pos, y)
    pos, y = pending
    x2 = _combine_call(pos.reshape(T // tc, 1, tc), x2, y, tc=tc)
    return x2.reshape(B, S, D).astype(x.dtype)
```
